```python
import jax, jax.numpy as jnp
from jax import lax
import numpy as np

D_MODEL = 1024
BATCH = 8
SEQ = 2048
DEPTH = 4
DEC_BATCH = 128
DEC_SEQ = 1
PAST_LEN = 16384
PAGE_SIZE = 128

N_MIXERS = 3
N_RGLRU = len(range(0, DEPTH, N_MIXERS))
N_GDN = len(range(1, DEPTH, N_MIXERS))
N_SSD = len(range(2, DEPTH, N_MIXERS))
CONV_W = 4
EPS = 1e-6
MLP_HIDDEN = 4 * D_MODEL
RG_WIDTH = D_MODEL
RG_BLOCK = 256
RG_BLOCKS = RG_WIDTH // RG_BLOCK
RG_C = 8.0
GDN_DK = 128
GDN_DV = 128
GDN_HEADS = D_MODEL // GDN_DK
GDN_QKV = GDN_HEADS * (2 * GDN_DK + GDN_DV)
GDN_IN = GDN_QKV + GDN_HEADS * GDN_DV + 2 * GDN_HEADS
GDN_CHUNK = 64
SSM_D_INNER = 2 * D_MODEL
SSM_HEADDIM = 64
SSM_HEADS = SSM_D_INNER // SSM_HEADDIM
SSM_STATE = 128
SSM_GROUPS = 4
SSM_HPG = SSM_HEADS // SSM_GROUPS
SSM_CONV_DIM = SSM_D_INNER + 2 * SSM_GROUPS * SSM_STATE
SSM_IN = SSM_D_INNER + SSM_CONV_DIM + SSM_HEADS
SSM_CHUNK = 64

kernel_name = 'hybrid_rglru_gdn_ssd_adaln_step'

F32 = jnp.float32


def rmsnorm(x):
    xf = x.astype(F32)
    return (xf * lax.rsqrt(jnp.mean(xf * xf, axis=-1, keepdims=True) + EPS)).astype(x.dtype)


def l2norm(x):
    xf = x.astype(F32)
    return xf * lax.rsqrt(jnp.sum(xf * xf, axis=-1, keepdims=True) + EPS)


def causal_conv(x, buf, w, b=None):
    L = x.shape[1]
    xp = jnp.concatenate([buf.astype(x.dtype), x], axis=1)
    y = xp[:, 0:L] * w[0]
    for k in range(1, CONV_W):
        y = y + xp[:, k:k + L] * w[k]
    if b is not None:
        y = y + b
    return y, xp[:, L:]


def lin_combine(left, right):
    a_l, b_l = left
    a_r, b_r = right
    return a_l * a_r, a_r * b_l + b_r


def rglru_mixer(u, conv_buf, h0, w_in, conv_w, conv_b, gate_w, gate_b, lam, w_out):
    Bsz, L, _ = u.shape
    proj = u @ w_in
    y_br = jax.nn.gelu(proj[..., :RG_WIDTH])
    x_br, new_buf = causal_conv(proj[..., RG_WIDTH:], conv_buf, conv_w, conv_b)
    xb = x_br.reshape(Bsz, L, RG_BLOCKS, RG_BLOCK)
    gates = jnp.einsum('blnj,gnjk->gblnk', xb, gate_w).reshape(2, Bsz, L, RG_WIDTH) + gate_b[:, None, None, :]
    r = jax.nn.sigmoid(gates[0].astype(F32))
    i = jax.nn.sigmoid(gates[1].astype(F32))
    log_a = RG_C * r * jax.nn.log_sigmoid(lam.astype(F32))
    a = jnp.exp(log_a)
    bx = jnp.sqrt(-jnp.expm1(2.0 * log_a)) * i * x_br.astype(F32)
    a_cum, b_cum = lax.associative_scan(lin_combine, (a, bx), axis=1)
    h = a_cum * h0.astype(F32)[:, None, :] + b_cum
    out = (h.astype(u.dtype) * y_br) @ w_out
    return out, new_buf, h[:, -1]


def gated_delta_chunked(q, k, v, g, beta, S0):
    Bsz, L, H, DK = q.shape
    DV = v.shape[-1]
    C = GDN_CHUNK if L % GDN_CHUNK == 0 else L
    n = L // C

    def chunks(t):
        return t.astype(F32).reshape(Bsz, n, C, H, -1).transpose(0, 3, 1, 2, 4)

    q, k, v = chunks(q), chunks(k), chunks(v)
    g = g.reshape(Bsz, n, C, H).transpose(0, 3, 1, 2)
    beta = beta.reshape(Bsz, n, C, H).transpose(0, 3, 1, 2)
    G = jnp.cumsum(g, axis=-1)
    causal = jnp.tril(jnp.ones((C, C), dtype=bool))
    strict = jnp.tril(jnp.ones((C, C), dtype=bool), -1)
    diff = G[..., :, None] - G[..., None, :]
    decay = jnp.where(causal, jnp.exp(jnp.where(causal, diff, 0.0)), 0.0)
    kb = k * beta[..., None]
    A = jnp.where(strict, jnp.einsum('bhnik,bhnjk->bhnij', kb, k) * decay, 0.0)
    lhs = A + jnp.eye(C, dtype=F32)
    rhs = jnp.concatenate([v * beta[..., None], kb * jnp.exp(G)[..., None]], axis=-1)
    sol = lax.linalg.triangular_solve(lhs, rhs, left_side=True, lower=True)
    U, W = sol[..., :DV], sol[..., DV:]
    qk = jnp.einsum('bhnik,bhnjk->bhnij', q, k) * decay
    q_dec = q * jnp.exp(G)[..., None]
    k_dec = k * jnp.exp(G[..., -1:] - G)[..., None]
    g_last = jnp.exp(G[..., -1])

    def step(S, inp):
        U_c, W_c, qk_c, qd_c, kd_c, gl_c = inp
        v_new = U_c - jnp.einsum('bhck,bhkv->bhcv', W_c, S)
        o_c = jnp.einsum('bhck,bhkv->bhcv', qd_c, S) + jnp.einsum('bhij,bhjv->bhiv', qk_c, v_new)
        S = gl_c[..., None, None] * S + jnp.einsum('bhck,bhcv->bhkv', kd_c, v_new)
        return S, o_c

    mv = lambda t: jnp.moveaxis(t, 2, 0)
    S_fin, o = lax.scan(step, S0.astype(F32), (mv(U), mv(W), mv(qk), mv(q_dec), mv(k_dec), mv(g_last)))
    o = o.transpose(1, 0, 3, 2, 4).reshape(Bsz, L, H, DV)
    return o, S_fin


def gdn_mixer(u, conv_buf, S0, w_in, conv_w, A_log, dt_bias, norm_g, w_out):
    Bsz, L, _ = u.shape
    HK = GDN_HEADS * GDN_DK
    HV = GDN_HEADS * GDN_DV
    proj = u @ w_in
    qkv = proj[..., :GDN_QKV]
    gate = proj[..., GDN_QKV:GDN_QKV + HV]
    a_in = proj[..., GDN_QKV + HV:GDN_QKV + HV + GDN_HEADS]
    b_in = proj[..., GDN_QKV + HV + GDN_HEADS:]
    qkv, new_buf = causal_conv(qkv, conv_buf, conv_w)
    qkv = jax.nn.silu(qkv)
    q = l2norm(qkv[..., :HK].reshape(Bsz, L, GDN_HEADS, GDN_DK)) * (GDN_DK ** -0.5)
    k = l2norm(qkv[..., HK:2 * HK].reshape(Bsz, L, GDN_HEADS, GDN_DK))
    v = qkv[..., 2 * HK:].reshape(Bsz, L, GDN_HEADS, GDN_DV)
    beta = jax.nn.sigmoid(b_in.astype(F32))
    g = -jnp.exp(A_log.astype(F32)) * jax.nn.softplus(a_in.astype(F32) + dt_bias)
    o, S_fin = gated_delta_chunked(q, k, v, g, beta, S0)
    o = rmsnorm(o.astype(u.dtype)) * norm_g * jax.nn.silu(gate.reshape(Bsz, L, GDN_HEADS, GDN_DV))
    return o.reshape(Bsz, L, HV) @ w_out, new_buf, S_fin


def ssd_chunked(x, dt, A, Bm, Cm, h0):
    Bsz, L, H, P = x.shape
    C = SSM_CHUNK if L % SSM_CHUNK == 0 else L
    n = L // C
    G, Hg, N = SSM_GROUPS, SSM_HPG, SSM_STATE
    x = x.astype(F32).reshape(Bsz, n, C, G, Hg, P)
    dt = dt.reshape(Bsz, n, C, G, Hg)
    Bm = Bm.astype(F32).reshape(Bsz, n, C, G, N)
    Cm = Cm.astype(F32).reshape(Bsz, n, C, G, N)
    a = (dt * A.reshape(G, Hg)).transpose(0, 1, 3, 4, 2)
    Acs = jnp.cumsum(a, axis=-1)
    causal = jnp.tril(jnp.ones((C, C), dtype=bool))
    seg = Acs[..., :, None] - Acs[..., None, :]
    Lmat = jnp.where(causal, jnp.exp(jnp.where(causal, seg, 0.0)), 0.0)
    xdt = x * dt[..., None]
    CB = jnp.einsum('bnlgd,bnsgd->bngls', Cm, Bm)
    y_diag = jnp.einsum('bnghls,bnsghp->bnlghp', CB[:, :, :, None] * Lmat, xdt)
    decay_states = jnp.exp(Acs[..., -1:] - Acs)
    states = jnp.einsum('bnsgd,bnghs,bnsghp->bnghpd', Bm, decay_states, xdt)
    chunk_decay = jnp.exp(Acs[..., -1])

    def step(h, inp):
        dec_c, st_c = inp
        return dec_c[..., None, None] * h + st_c, h

    h_fin, h_prev = lax.scan(step, h0.astype(F32).reshape(Bsz, G, Hg, P, N),
                             (jnp.moveaxis(chunk_decay, 1, 0), jnp.moveaxis(states, 1, 0)))
    h_prev = jnp.moveaxis(h_prev, 0, 1)
    y_off = jnp.einsum('bnlgd,bnghpd,bnghl->bnlghp', Cm, h_prev, jnp.exp(Acs))
    y = (y_diag + y_off).reshape(Bsz, L, H, P)
    return y, h_fin.reshape(Bsz, H, P, N)


def ssd_mixer(u, conv_buf, h0, w_in, conv_w, conv_b, A_log, dt_bias, D_skip, norm_g, w_out):
    Bsz, L, _ = u.shape
    GN = SSM_GROUPS * SSM_STATE
    proj = u @ w_in
    z = proj[..., :SSM_D_INNER]
    xBC = proj[..., SSM_D_INNER:SSM_D_INNER + SSM_CONV_DIM]
    dt_in = proj[..., SSM_D_INNER + SSM_CONV_DIM:]
    xBC, new_buf = causal_conv(xBC, conv_buf, conv_w, conv_b)
    xBC = jax.nn.silu(xBC)
    xs = xBC[..., :SSM_D_INNER].reshape(Bsz, L, SSM_HEADS, SSM_HEADDIM)
    Bm = xBC[..., SSM_D_INNER:SSM_D_INNER + GN].reshape(Bsz, L, SSM_GROUPS, SSM_STATE)
    Cm = xBC[..., SSM_D_INNER + GN:].reshape(Bsz, L, SSM_GROUPS, SSM_STATE)
    dt = jax.nn.softplus(dt_in.astype(F32) + dt_bias)
    A = -jnp.exp(A_log.astype(F32))
    y, h_fin = ssd_chunked(xs, dt, A, Bm, Cm, h0)
    y = y + D_skip[:, None] * xs.astype(F32)
    y = (y.reshape(Bsz, L, SSM_D_INNER) * jax.nn.silu(z.astype(F32))).astype(u.dtype)
    y = rmsnorm(y.reshape(Bsz, L, SSM_GROUPS, SSM_D_INNER // SSM_GROUPS)).reshape(Bsz, L, SSM_D_INNER) * norm_g
    return y @ w_out, new_buf, h_fin


def sq_relu_mlp(u, w_up, w_down):
    h = jax.nn.relu(u @ w_up)
    return (h * h) @ w_down


def trunk(x, c, rg_conv, rg_h, gdn_conv, gdn_S, ssd_conv, ssd_h, p):
    new_rg_conv, new_rg_h, new_gdn_conv, new_gdn_S, new_ssd_conv, new_ssd_h = [], [], [], [], [], []
    for i in range(DEPTH):
        j = i // N_MIXERS
        mod = jax.nn.silu(c) @ p['w_mod'][i] + p['b_mod'][i]
        sh1, sc1, g1, sh2, sc2, g2 = jnp.split(mod[:, None, :], 6, axis=-1)
        u = rmsnorm(x) * (1.0 + sc1) + sh1
        kind = i % N_MIXERS
        if kind == 0:
            out, cb, st = rglru_mixer(u, rg_conv[j], rg_h[j], p['rg_w_in'][j], p['rg_conv_w'][j], p['rg_conv_b'][j],
                                      p['rg_gate_w'][j], p['rg_gate_b'][j], p['rg_lambda'][j], p['rg_w_out'][j])
            new_rg_conv.append(cb)
            new_rg_h.append(st)
        elif kind == 1:
            out, cb, st = gdn_mixer(u, gdn_conv[j], gdn_S[j], p['gdn_w_in'][j], p['gdn_conv_w'][j], p['gdn_A_log'][j],
                                    p['gdn_dt_bias'][j], p['gdn_norm_g'][j], p['gdn_w_out'][j])
            new_gdn_conv.append(cb)
            new_gdn_S.append(st)
        else:
            out, cb, st = ssd_mixer(u, ssd_conv[j], ssd_h[j], p['ssd_w_in'][j], p['ssd_conv_w'][j], p['ssd_conv_b'][j],
                                    p['ssd_A_log'][j], p['ssd_dt_bias'][j], p['ssd_D'][j], p['ssd_norm_g'][j], p['ssd_w_out'][j])
            new_ssd_conv.append(cb)
            new_ssd_h.append(st)
        x = x + g1 * out
        u = rmsnorm(x) * (1.0 + sc2) + sh2
        x = x + g2 * sq_relu_mlp(u, p['w_mlp_up'][i], p['w_mlp_down'][i])
    y = rmsnorm(x) * p['final_norm_g']
    return (y, jnp.stack(new_rg_conv), jnp.stack(new_rg_h), jnp.stack(new_gdn_conv), jnp.stack(new_gdn_S),
            jnp.stack(new_ssd_conv), jnp.stack(new_ssd_h))


def setup_inputs(seed: int = 0) -> dict:
    key = jax.random.key(seed)
    ks = iter(jax.random.split(key, 48))

    def nrm(shape, scale):
        return jax.random.normal(next(ks), shape, F32) * scale

    def unif(shape, lo, hi):
        return jax.random.uniform(next(ks), shape, F32, lo, hi)

    def dt_bias_init(shape):
        dt = jnp.exp(unif(shape, float(np.log(1e-3)), float(np.log(1e-1))))
        return dt + jnp.log(-jnp.expm1(-dt))

    D = D_MODEL
    s_lam = unif((N_RGLRU, RG_WIDTH), 0.9, 0.999)
    return {
        'x_prompt': nrm((BATCH, SEQ, D), 1.0),
        'x_sample': nrm((DEC_BATCH, DEC_SEQ, D), 1.0),
        'state_rglru_conv': nrm((N_RGLRU, DEC_BATCH, CONV_W - 1, RG_WIDTH), 1.0),
        'state_rglru_h': nrm((N_RGLRU, DEC_BATCH, RG_WIDTH), 0.5),
        'state_gdn_conv': nrm((N_GDN, DEC_BATCH, CONV_W - 1, GDN_QKV), 1.0),
        'state_gdn_S': nrm((N_GDN, DEC_BATCH, GDN_HEADS, GDN_DK, GDN_DV), 0.1),
        'state_ssd_conv': nrm((N_SSD, DEC_BATCH, CONV_W - 1, SSM_CONV_DIM), 1.0),
        'state_ssd_h': nrm((N_SSD, DEC_BATCH, SSM_HEADS, SSM_HEADDIM, SSM_STATE), 0.1),
        'c_prompt': nrm((BATCH, D), 1.0),
        'c_sample': nrm((DEC_BATCH, D), 1.0),
        'w_mod': nrm((DEPTH, D, 6 * D), D ** -0.5),
        'b_mod': nrm((DEPTH, 6 * D), 0.02),
        'w_mlp_up': nrm((DEPTH, D, MLP_HIDDEN), D ** -0.5),
        'w_mlp_down': nrm((DEPTH, MLP_HIDDEN, D), MLP_HIDDEN ** -0.5),
        'final_norm_g': 1.0 + nrm((D,), 0.02),
        'rg_w_in': nrm((N_RGLRU, D, 2 * RG_WIDTH), D ** -0.5),
        'rg_conv_w': nrm((N_RGLRU, CONV_W, RG_WIDTH), CONV_W ** -0.5),
        'rg_conv_b': nrm((N_RGLRU, RG_WIDTH), 0.02),
        'rg_gate_w': nrm((N_RGLRU, 2, RG_BLOCKS, RG_BLOCK, RG_BLOCK), RG_BLOCK ** -0.5),
        'rg_gate_b': nrm((N_RGLRU, 2, RG_WIDTH), 0.02),
        'rg_lambda': jnp.log(s_lam) - jnp.log1p(-s_lam),
        'rg_w_out': nrm((N_RGLRU, RG_WIDTH, D), RG_WIDTH ** -0.5),
        'gdn_w_in': nrm((N_GDN, D, GDN_IN), D ** -0.5),
        'gdn_conv_w': nrm((N_GDN, CONV_W, GDN_QKV), CONV_W ** -0.5),
        'gdn_A_log': jnp.log(unif((N_GDN, GDN_HEADS), 1.0, 16.0)),
        'gdn_dt_bias': dt_bias_init((N_GDN, GDN_HEADS)),
        'gdn_norm_g': 1.0 + nrm((N_GDN, GDN_DV), 0.02),
        'gdn_w_out': nrm((N_GDN, GDN_HEADS * GDN_DV, D), (GDN_HEADS * GDN_DV) ** -0.5),
        'ssd_w_in': nrm((N_SSD, D, SSM_IN), D ** -0.5),
        'ssd_conv_w': nrm((N_SSD, CONV_W, SSM_CONV_DIM), CONV_W ** -0.5),
        'ssd_conv_b': nrm((N_SSD, SSM_CONV_DIM), 0.02),
        'ssd_A_log': jnp.log(unif((N_SSD, SSM_HEADS), 1.0, 16.0)),
        'ssd_dt_bias': dt_bias_init((N_SSD, SSM_HEADS)),
        'ssd_D': 1.0 + nrm((N_SSD, SSM_HEADS), 0.02),
        'ssd_norm_g': 1.0 + nrm((N_SSD, SSM_D_INNER), 0.02),
        'ssd_w_out': nrm((N_SSD, SSM_D_INNER, D), SSM_D_INNER ** -0.5),
    }


def reference(x_prompt, x_sample, state_rglru_conv, state_rglru_h, state_gdn_conv, state_gdn_S,
              state_ssd_conv, state_ssd_h, c_prompt, c_sample, w_mod, b_mod, w_mlp_up, w_mlp_down,
              final_norm_g, rg_w_in, rg_conv_w, rg_conv_b, rg_gate_w, rg_gate_b, rg_lambda, rg_w_out,
              gdn_w_in, gdn_conv_w, gdn_A_log, gdn_dt_bias, gdn_norm_g, gdn_w_out,
              ssd_w_in, ssd_conv_w, ssd_conv_b, ssd_A_log, ssd_dt_bias, ssd_D, ssd_norm_g, ssd_w_out):
    p = dict(w_mod=w_mod, b_mod=b_mod, w_mlp_up=w_mlp_up, w_mlp_down=w_mlp_down, final_norm_g=final_norm_g,
             rg_w_in=rg_w_in, rg_conv_w=rg_conv_w, rg_conv_b=rg_conv_b, rg_gate_w=rg_gate_w, rg_gate_b=rg_gate_b,
             rg_lambda=rg_lambda, rg_w_out=rg_w_out, gdn_w_in=gdn_w_in, gdn_conv_w=gdn_conv_w, gdn_A_log=gdn_A_log,
             gdn_dt_bias=gdn_dt_bias, gdn_norm_g=gdn_norm_g, gdn_w_out=gdn_w_out, ssd_w_in=ssd_w_in,
             ssd_conv_w=ssd_conv_w, ssd_conv_b=ssd_conv_b, ssd_A_log=ssd_A_log, ssd_dt_bias=ssd_dt_bias,
             ssd_D=ssd_D, ssd_norm_g=ssd_norm_g, ssd_w_out=ssd_w_out)
    Bp = x_prompt.shape[0]
    dtp = x_prompt.dtype
    z_rg_conv = jnp.zeros((N_RGLRU, Bp, CONV_W - 1, RG_WIDTH), dtp)
    z_rg_h = jnp.zeros((N_RGLRU, Bp, RG_WIDTH), dtp)
    z_gdn_conv = jnp.zeros((N_GDN, Bp, CONV_W - 1, GDN_QKV), dtp)
    z_gdn_S = jnp.zeros((N_GDN, Bp, GDN_HEADS, GDN_DK, GDN_DV), dtp)
    z_ssd_conv = jnp.zeros((N_SSD, Bp, CONV_W - 1, SSM_CONV_DIM), dtp)
    z_ssd_h = jnp.zeros((N_SSD, Bp, SSM_HEADS, SSM_HEADDIM, SSM_STATE), dtp)
    (y_prompt, p_rg_conv, p_rg_h, p_gdn_conv, p_gdn_S, p_ssd_conv, p_ssd_h) = trunk(
        x_prompt, c_prompt, z_rg_conv, z_rg_h, z_gdn_conv, z_gdn_S, z_ssd_conv, z_ssd_h, p)
    (y_sample, s_rg_conv, s_rg_h, s_gdn_conv, s_gdn_S, s_ssd_conv, s_ssd_h) = trunk(
        x_sample, c_sample, state_rglru_conv, state_rglru_h, state_gdn_conv, state_gdn_S,
        state_ssd_conv, state_ssd_h, p)
    return (y_prompt, y_sample, p_rg_conv, p_rg_h, p_gdn_conv, p_gdn_S, p_ssd_conv, p_ssd_h,
            s_rg_conv, s_rg_h, s_gdn_conv, s_gdn_S, s_ssd_conv, s_ssd_h)
```

```python
import functools

import jax
import jax.numpy as jnp
from jax import lax
from jax.experimental import pallas as pl
from jax.experimental.pallas import tpu as pltpu

F32 = jnp.float32
BF16 = jnp.bfloat16

DEPTH = 4
N_MIXERS = 3
EPS = 1e-6
RG_C = 8.0
RG_BLOCK = 256
GDN_DK = 128
GDN_DV = 128
SSM_HEADDIM = 64
SSM_STATE = 128
SSM_GROUPS = 4

SUBLANES = 8
LANES = 128

VMEM_LIMIT = 56 * 1024 * 1024
ROW_TILE = 512
GDN_CHUNK = 64
GDN_STEP = 128
SSD_CHUNK = 128
MLP_HCHUNK = 1024


def _params(*sem):
    return pltpu.CompilerParams(dimension_semantics=sem, vmem_limit_bytes=VMEM_LIMIT)


def _const_spec(shape):
    nd = len(shape)
    return pl.BlockSpec(shape, lambda *_: (0,) * nd, pipeline_mode=pl.Buffered(1))


def _bdot(a, b):
    return jnp.dot(a.astype(BF16), b.astype(BF16), preferred_element_type=F32)


def _bdot_nt(a, b):
    return lax.dot_general(a.astype(BF16), b.astype(BF16), (((1,), (1,)), ((), ())),
                           preferred_element_type=F32)


def _split(a):
    hi = a.astype(BF16)
    lo = (a - hi.astype(F32)).astype(BF16)
    return hi, lo


def _sdot_l(a, b_exact):
    hi, lo = _split(a)
    return (jnp.dot(hi, b_exact, preferred_element_type=F32)
            + jnp.dot(lo, b_exact, preferred_element_type=F32))


def _sdot_r(a_exact, b):
    hi, lo = _split(b)
    return (jnp.dot(a_exact, hi, preferred_element_type=F32)
            + jnp.dot(a_exact, lo, preferred_element_type=F32))


def _modnorm(x, sc, sh):
    ms = jnp.mean(x * x, axis=-1, keepdims=True)
    return x * lax.rsqrt(ms + EPS) * (1.0 + sc) + sh


def _silu(x):
    return x * jax.nn.sigmoid(x)


def _conv_seq(xpre, tail, w_ref, bias):
    xcat = jnp.concatenate([tail, xpre], axis=0)
    acc = xpre * w_ref[3:4, :]
    for k in (1, 2, 3):
        shifted = pltpu.roll(xcat, k, 0)[SUBLANES:]
        acc = acc + shifted * w_ref[3 - k:4 - k, :]
    if bias is not None:
        acc = acc + bias
    return acc


def _block_tri(n, chunk, lower):
    r = lax.broadcasted_iota(jnp.int32, (n, n), 0)
    c = lax.broadcasted_iota(jnp.int32, (n, n), 1)
    same = (r // chunk) == (c // chunk)
    tri = (c <= r) if lower else (r <= c)
    return jnp.where(same & tri, 1.0, 0.0).astype(BF16)


def _mod_kernel(c_ref, w_ref, b_ref, o_ref):
    c = c_ref[...]
    o_ref[...] = _bdot(_silu(c), w_ref[...]) + b_ref[...]


def _modulation(c_all, w_mod, b_mod):
    nb, d = c_all.shape
    n6 = w_mod.shape[-1]
    tn = 1024
    return pl.pallas_call(
        _mod_kernel,
        grid=(DEPTH, n6 // tn),
        in_specs=[pl.BlockSpec((nb, d), lambda l, n: (0, 0)),
                  pl.BlockSpec((None, d, tn), lambda l, n: (l, 0, n)),
                  pl.BlockSpec((None, 1, tn), lambda l, n: (l, 0, n))],
        out_specs=pl.BlockSpec((None, nb, tn), lambda l, n: (l, 0, n)),
        out_shape=jax.ShapeDtypeStruct((DEPTH, nb, n6), F32),
        compiler_params=_params("parallel", "parallel"),
        name="modulation",
    )(c_all, w_mod, b_mod.reshape(DEPTH, 1, n6))


def _outproj_mlp_kernel(x_ref, y_ref, mod_ref, wo_ref, wu_ref, wd_ref, fg_ref, o_ref, *, final):
    x1 = x_ref[...] + mod_ref[2] * jnp.dot(y_ref[...], wo_ref[...], preferred_element_type=F32)
    u = _modnorm(x1, mod_ref[4], mod_ref[3]).astype(BF16)
    hidden = wu_ref.shape[1]
    acc = None
    for c in range(hidden // MLP_HCHUNK):
        cs = slice(c * MLP_HCHUNK, (c + 1) * MLP_HCHUNK)
        h = jnp.maximum(jnp.dot(u, wu_ref[:, cs], preferred_element_type=F32), 0.0)
        p = jnp.dot((h * h).astype(BF16), wd_ref[cs, :], preferred_element_type=F32)
        acc = p if acc is None else acc + p
    x2 = x1 + mod_ref[5] * acc
    if final:
        ms = jnp.mean(x2 * x2, axis=-1, keepdims=True)
        x2 = x2 * lax.rsqrt(ms + EPS) * fg_ref[...]
    o_ref[...] = x2


def _outproj_mlp(x, y, mod, w_out, w_up, w_down, final_g, *, tm, rows_per_mod, final):
    n, d = x.shape
    k = y.shape[1]
    hid = w_up.shape[1]
    if rows_per_mod == 1:
        mod_spec = pl.BlockSpec((6, tm, d), lambda i: (0, i, 0))
    else:
        per = rows_per_mod // tm
        mod_spec = pl.BlockSpec((6, None, 1, d), lambda i: (0, i // per, 0, 0))
    return pl.pallas_call(
        functools.partial(_outproj_mlp_kernel, final=final),
        grid=(n // tm,),
        in_specs=[pl.BlockSpec((tm, d), lambda i: (i, 0)),
                  pl.BlockSpec((tm, k), lambda i: (i, 0)),
                  mod_spec,
                  _const_spec((k, d)), _const_spec((d, hid)), _const_spec((hid, d)),
                  _const_spec((1, d))],
        out_specs=pl.BlockSpec((tm, d), lambda i: (i, 0)),
        out_shape=jax.ShapeDtypeStruct((n, d), F32),
        compiler_params=_params("parallel"),
        name="outproj_mlp",
    )(x, y, mod, w_out, w_up, w_down, final_g)


def _rg_gates(xbr, gwr_ref, gwi_ref, gb_ref, logsig, a_s, b_s):
    xb = xbr.astype(BF16)
    nblk = xbr.shape[1] // RG_BLOCK
    for n in range(nblk):
        sl = slice(n * RG_BLOCK, (n + 1) * RG_BLOCK)
        gr = jnp.dot(xb[:, sl], gwr_ref[n], preferred_element_type=F32) + gb_ref[0:1, sl]
        gi = jnp.dot(xb[:, sl], gwi_ref[n], preferred_element_type=F32) + gb_ref[1:2, sl]
        log_a = (RG_C * logsig[:, sl]) * jax.nn.sigmoid(gr)
        a_s[:, sl] = jnp.exp(log_a)
        th = jnp.tanh(log_a)
        b_s[:, sl] = jnp.sqrt(-2.0 * th / (1.0 - th)) * jax.nn.sigmoid(gi) * xbr[:, sl]


def _rg_prompt_kernel(x_ref, mod_ref, wy_ref, wx_ref, cw_ref, cb_ref, gwr_ref, gwi_ref, gb_ref,
                      lam_ref, y_ref, tail_ref, h_ref, a_s, b_s):
    t, w = a_s.shape

    @pl.when(pl.program_id(1) == 0)
    def _():
        tail_ref[...] = jnp.zeros_like(tail_ref)
        h_ref[...] = jnp.zeros_like(h_ref)

    u = _modnorm(x_ref[...], mod_ref[1], mod_ref[0]).astype(BF16)
    xpre = jnp.dot(u, wx_ref[...], preferred_element_type=F32)
    xbr = _conv_seq(xpre, tail_ref[...], cw_ref, cb_ref[...])
    tail_ref[...] = xpre[t - SUBLANES:, :]
    _rg_gates(xbr, gwr_ref, gwi_ref, gb_ref, jax.nn.log_sigmoid(lam_ref[...]), a_s, b_s)

    row = lax.broadcasted_iota(jnp.int32, (SUBLANES, w), 0)

    def group(gi, h):
        r0 = pl.multiple_of(gi * SUBLANES, SUBLANES)
        a = a_s[pl.ds(r0, SUBLANES), :]
        b = b_s[pl.ds(r0, SUBLANES), :]
        for s in (1, 2, 4):
            a_sh = jnp.where(row >= s, pltpu.roll(a, s, 0), 1.0)
            b_sh = jnp.where(row >= s, pltpu.roll(b, s, 0), 0.0)
            b = a * b_sh + b
            a = a * a_sh
        hg = a * h + b
        b_s[pl.ds(r0, SUBLANES), :] = hg
        return hg[SUBLANES - 1:SUBLANES, :]

    h_ref[...] = lax.fori_loop(0, t // SUBLANES, group, h_ref[...], unroll=2)
    y_br = jax.nn.gelu(jnp.dot(u, wy_ref[...], preferred_element_type=F32))
    y_ref[...] = (b_s[...] * y_br).astype(BF16)


def _rg_prompt(x, mod, p):
    b, l, d = x.shape
    w = p["wx"].shape[1]
    t = ROW_TILE
    nblk = w // RG_BLOCK
    return pl.pallas_call(
        _rg_prompt_kernel,
        grid=(b, l // t),
        in_specs=[pl.BlockSpec((None, t, d), lambda i, c: (i, c, 0)),
                  pl.BlockSpec((6, None, 1, d), lambda i, c: (0, i, 0, 0)),
                  _const_spec((d, w)), _const_spec((d, w)),
                  _const_spec((4, w)), _const_spec((1, w)),
                  _const_spec((nblk, RG_BLOCK, RG_BLOCK)), _const_spec((nblk, RG_BLOCK, RG_BLOCK)),
                  _const_spec((2, w)), _const_spec((1, w))],
        out_specs=[pl.BlockSpec((None, t, w), lambda i, c: (i, c, 0)),
                   pl.BlockSpec((None, SUBLANES, w), lambda i, c: (i, 0, 0)),
                   pl.BlockSpec((None, 1, w), lambda i, c: (i, 0, 0))],
        out_shape=[jax.ShapeDtypeStruct((b, l, w), BF16),
                   jax.ShapeDtypeStruct((b, SUBLANES, w), F32),
                   jax.ShapeDtypeStruct((b, 1, w), F32)],
        scratch_shapes=[pltpu.VMEM((t, w), F32), pltpu.VMEM((t, w), F32)],
        compiler_params=_params("parallel", "arbitrary"),
        name="rglru_prompt",
    )(x, mod, p["wy"], p["wx"], p["cw"], p["cb"], p["gwr"], p["gwi"], p["gb"], p["lam"])


def _rg_decode_kernel(x_ref, mod_ref, wy_ref, wx_ref, cw_ref, cb_ref, gwr_ref, gwi_ref, gb_ref,
                      lam_ref, cs_ref, h0_ref, y_ref, nb_ref, h_ref, a_s, b_s):
    u = _modnorm(x_ref[...], mod_ref[1], mod_ref[0]).astype(BF16)
    xpre = jnp.dot(u, wx_ref[...], preferred_element_type=F32)
    xbr = (cs_ref[0] * cw_ref[0:1, :] + cs_ref[1] * cw_ref[1:2, :] + cs_ref[2] * cw_ref[2:3, :]
           + xpre * cw_ref[3:4, :] + cb_ref[...])
    nb_ref[0] = cs_ref[1]
    nb_ref[1] = cs_ref[2]
    nb_ref[2] = xpre
    _rg_gates(xbr, gwr_ref, gwi_ref, gb_ref, jax.nn.log_sigmoid(lam_ref[...]), a_s, b_s)
    h = a_s[...] * h0_ref[...] + b_s[...]
    h_ref[...] = h
    y_br = jax.nn.gelu(jnp.dot(u, wy_ref[...], preferred_element_type=F32))
    y_ref[...] = (h * y_br).astype(BF16)


def _rg_decode(x, mod, p, conv_state, h0):
    n, d = x.shape
    w = p["wx"].shape[1]
    nblk = w // RG_BLOCK
    full = lambda shape: pl.BlockSpec(shape, lambda i: (0,) * len(shape))
    return pl.pallas_call(
        _rg_decode_kernel,
        grid=(1,),
        in_specs=[full((n, d)), full((6, n, d)), full((d, w)), full((d, w)), full((4, w)), full((1, w)),
                  full((nblk, RG_BLOCK, RG_BLOCK)), full((nblk, RG_BLOCK, RG_BLOCK)), full((2, w)),
                  full((1, w)), full((3, n, w)), full((n, w))],
        out_specs=[full((n, w)), full((3, n, w)), full((n, w))],
        out_shape=[jax.ShapeDtypeStruct((n, w), BF16),
                   jax.ShapeDtypeStruct((3, n, w), F32),
                   jax.ShapeDtypeStruct((n, w), F32)],
        scratch_shapes=[pltpu.VMEM((n, w), F32), pltpu.VMEM((n, w), F32)],
        compiler_params=_params("arbitrary"),
        name="rglru_decode",
    )(x, mod, p["wy"], p["wx"], p["cw"], p["cb"], p["gwr"], p["gwi"], p["gb"], p["lam"], conv_state, h0)


def _gdn_qkv(u, wqkv_ref, conv_fn, q_ref, k_ref, v_ref):
    hk = q_ref.shape[-1]
    nh = hk // GDN_DK
    for j, (dst, scale) in enumerate(((q_ref, GDN_DK ** -0.5), (k_ref, 1.0), (v_ref, None))):
        pre = jnp.dot(u, wqkv_ref[:, j * hk:(j + 1) * hk], preferred_element_type=F32)
        act = _silu(conv_fn(pre, j))
        if scale is None:
            dst[...] = act
        else:
            for h in range(nh):
                hs = slice(h * GDN_DK, (h + 1) * GDN_DK)
                xh = act[:, hs]
                ss = jnp.sum(xh * xh, axis=-1, keepdims=True)
                dst[:, hs] = xh * (lax.rsqrt(ss + EPS) * scale)


def _gdn_in_kernel(x_ref, mod_ref, wqkv_ref, wg_ref, wab_ref, wabt_ref, cw_ref, alr_ref, dtr_ref,
                   alc_ref, dtc_ref, q_ref, k_ref, v_ref, gz_ref, gb_ref, gr_ref, tail_ref):
    t = x_ref.shape[0]
    hk = q_ref.shape[-1]
    nh = hk // GDN_DK

    @pl.when(pl.program_id(1) == 0)
    def _():
        tail_ref[...] = jnp.zeros_like(tail_ref)

    u = _modnorm(x_ref[...], mod_ref[1], mod_ref[0]).astype(BF16)

    def conv_fn(pre, j):
        cs = slice(j * hk, (j + 1) * hk)
        out = _conv_seq(pre, tail_ref[:, cs], cw_ref.at[:, cs], None)
        tail_ref[:, cs] = pre[t - SUBLANES:, :]
        return out

    _gdn_qkv(u, wqkv_ref, conv_fn, q_ref, k_ref, v_ref)
    gz_ref[...] = _silu(jnp.dot(u, wg_ref[...], preferred_element_type=F32))

    ab = jnp.dot(u, wab_ref[...], preferred_element_type=F32)
    abt = lax.dot_general(wabt_ref[...], u, (((1,), (1,)), ((), ())),
                          preferred_element_type=F32)
    g_col = -jnp.exp(alr_ref[...]) * jax.nn.softplus(ab + dtr_ref[...])
    g_row = -jnp.exp(alc_ref[...]) * jax.nn.softplus(abt + dtc_ref[...])
    gc = _sdot_r(_block_tri(t, GDN_CHUNK, True), g_col)
    gr = _sdot_l(g_row, _block_tri(t, GDN_CHUNK, False))
    lane = lax.broadcasted_iota(jnp.int32, ab.shape, 1)
    gb_ref[...] = jnp.where(lane < nh, gc, jax.nn.sigmoid(ab))
    gr_ref[...] = gr[:SUBLANES, :]


def _gdn_in(x, mod, p):
    b, l, d = x.shape
    hk = p["wqkv"].shape[1] // 3
    t = ROW_TILE
    tok = lambda width: pl.BlockSpec((None, t, width), lambda i, c: (i, c, 0))
    return pl.pallas_call(
        _gdn_in_kernel,
        grid=(b, l // t),
        in_specs=[tok(d), pl.BlockSpec((6, None, 1, d), lambda i, c: (0, i, 0, 0)),
                  _const_spec((d, 3 * hk)), _const_spec((d, hk)), _const_spec((d, LANES)),
                  _const_spec((LANES, d)), _const_spec((4, 3 * hk)),
                  _const_spec((1, LANES)), _const_spec((1, LANES)),
                  _const_spec((LANES, 1)), _const_spec((LANES, 1))],
        out_specs=[tok(hk), tok(hk), tok(hk), tok(hk), tok(LANES),
                   pl.BlockSpec((None, SUBLANES, t), lambda i, c: (i, 0, c)),
                   pl.BlockSpec((None, SUBLANES, 3 * hk), lambda i, c: (i, 0, 0))],
        out_shape=[jax.ShapeDtypeStruct((b, l, hk), F32)] * 4
        + [jax.ShapeDtypeStruct((b, l, LANES), F32),
           jax.ShapeDtypeStruct((b, SUBLANES, l), F32),
           jax.ShapeDtypeStruct((b, SUBLANES, 3 * hk), F32)],
        compiler_params=_params("parallel", "arbitrary"),
        name="gdn_in",
    )(x, mod, p["wqkv"], p["wg"], p["wab"], p["wabt"], p["cw"], p["alr"], p["dtr"], p["alc"], p["dtc"])


def _gdn_scan_kernel(q_ref, k_ref, v_ref, gz_ref, gb_ref, gr_ref, ng_ref, y_ref, s_ref):
    c = GDN_CHUNK
    nh = s_ref.shape[0]

    @pl.when(pl.program_id(1) == 0)
    def _():
        s_ref[...] = jnp.zeros_like(s_ref)

    r = lax.broadcasted_iota(jnp.int32, (c, c), 0)
    cc = lax.broadcasted_iota(jnp.int32, (c, c), 1)
    causal = r >= cc
    strict = r > cc
    eye = jnp.where(r == cc, 1.0, 0.0)
    merge = []
    s = 1
    while s < c:
        merge.append((r // (2 * s) == cc // (2 * s)) & ((r // s) % 2 == 1) & ((cc // s) % 2 == 0))
        s *= 2

    for j in range(q_ref.shape[0] // c):
        rs = slice(j * c, (j + 1) * c)
        gbj = gb_ref[rs, :]
        for h in range(nh):
            hs = slice(h * GDN_DK, (h + 1) * GDN_DK)
            q = q_ref[rs, hs]
            k = k_ref[rs, hs]
            v = v_ref[rs, hs]
            g_c = gbj[:, h:h + 1]
            beta = gbj[:, nh + h:nh + h + 1]
            g_r = gr_ref[h:h + 1, rs]
            g_last = g_r[:, c - 1:c]
            decay = jnp.where(causal, jnp.exp(jnp.where(causal, g_c - g_r, 0.0)), 0.0)
            e_g = jnp.exp(g_c)
            kb = k * beta
            kk = _bdot_nt(jnp.concatenate([kb, q], axis=0), k)
            a = jnp.where(strict, kk[:c] * decay, 0.0)
            qk = kk[c:] * decay
            x = eye - jnp.where(merge[0], a, 0.0)
            for m in merge[1:]:
                x = x - _bdot(x, _bdot(jnp.where(m, a, 0.0), x))
            uw = _bdot(x, jnp.concatenate([v * beta, kb * e_g], axis=1))
            u_ = uw[:, :GDN_DV]
            w_ = uw[:, GDN_DV:]
            s = s_ref[h]
            ws_qs = _bdot(jnp.concatenate([w_, q * e_g], axis=0), s)
            v_new = u_ - ws_qs[:c]
            o = ws_qs[c:] + _bdot(qk, v_new)
            k_dec = k * jnp.exp(g_last - g_c)
            s_ref[h] = jnp.exp(g_last) * s + lax.dot_general(
                k_dec.astype(BF16), v_new.astype(BF16), (((0,), (0,)), ((), ())),
                preferred_element_type=F32)
            ms = jnp.mean(o * o, axis=-1, keepdims=True)
            y_ref[rs, hs] = (o * lax.rsqrt(ms + EPS) * ng_ref[...] * gz_ref[rs, hs]).astype(BF16)


def _gdn_scan(q, k, v, gz, gb, gr, norm_g):
    b, l, hk = q.shape
    nh = hk // GDN_DK
    t = GDN_STEP
    tok = lambda width: pl.BlockSpec((None, t, width), lambda i, c: (i, c, 0))
    return pl.pallas_call(
        _gdn_scan_kernel,
        grid=(b, l // t),
        in_specs=[tok(hk), tok(hk), tok(hk), tok(hk), tok(LANES),
                  pl.BlockSpec((None, SUBLANES, t), lambda i, c: (i, 0, c)),
                  _const_spec((1, GDN_DV))],
        out_specs=[tok(hk), pl.BlockSpec((None, nh, GDN_DK, GDN_DV), lambda i, c: (i, 0, 0, 0))],
        out_shape=[jax.ShapeDtypeStruct((b, l, hk), BF16),
                   jax.ShapeDtypeStruct((b, nh, GDN_DK, GDN_DV), F32)],
        compiler_params=_params("parallel", "arbitrary"),
        name="gdn_scan",
    )(q, k, v, gz, gb, gr, norm_g)


def _gdn_dec_in_kernel(x_ref, mod_ref, wqkv_ref, wg_ref, wab_ref, cw_ref, alr_ref, dtr_ref, cs_ref,
                       qt_ref, kt_ref, v_ref, gz_ref, eg_ref, be_ref, nb_ref, q_s, k_s):
    hk = v_ref.shape[-1]
    nh = hk // GDN_DK
    u = _modnorm(x_ref[...], mod_ref[1], mod_ref[0]).astype(BF16)

    def conv_fn(pre, j):
        cs = slice(j * hk, (j + 1) * hk)
        nb_ref[0, :, cs] = cs_ref[1, :, cs]
        nb_ref[1, :, cs] = cs_ref[2, :, cs]
        nb_ref[2, :, cs] = pre
        return (cs_ref[0, :, cs] * cw_ref[0:1, cs] + cs_ref[1, :, cs] * cw_ref[1:2, cs]
                + cs_ref[2, :, cs] * cw_ref[2:3, cs] + pre * cw_ref[3:4, cs])

    _gdn_qkv(u, wqkv_ref, conv_fn, q_s, k_s, v_ref)
    gz_ref[...] = _silu(jnp.dot(u, wg_ref[...], preferred_element_type=F32))
    ab = jnp.dot(u, wab_ref[...], preferred_element_type=F32)
    e_g = jnp.exp(-jnp.exp(alr_ref[...]) * jax.nn.softplus(ab + dtr_ref[...]))
    beta = jax.nn.sigmoid(ab)
    n = ab.shape[0]
    for h in range(nh):
        hs = slice(h * GDN_DK, (h + 1) * GDN_DK)
        qt_ref[h] = q_s[:, hs].T
        kt_ref[h] = k_s[:, hs].T
        eg_ref[h] = jnp.broadcast_to(e_g[:, h:h + 1], (n, LANES))
        be_ref[h] = jnp.broadcast_to(beta[:, nh + h:nh + h + 1], (n, LANES))


def _gdn_dec_in(x, mod, p, conv_state):
    n, d = x.shape
    hk = p["wqkv"].shape[1] // 3
    nh = hk // GDN_DK
    full = lambda shape: pl.BlockSpec(shape, lambda i: (0,) * len(shape))
    return pl.pallas_call(
        _gdn_dec_in_kernel,
        grid=(1,),
        in_specs=[full((n, d)), full((6, n, d)), full((d, 3 * hk)), full((d, hk)), full((d, LANES)),
                  full((4, 3 * hk)), full((1, LANES)), full((1, LANES)), full((3, n, 3 * hk))],
        out_specs=[full((nh, GDN_DK, n)), full((nh, GDN_DK, n)), full((n, hk)), full((n, hk)),
                   full((nh, n, LANES)), full((nh, n, LANES)), full((3, n, 3 * hk))],
        out_shape=[jax.ShapeDtypeStruct((nh, GDN_DK, n), F32)] * 2
        + [jax.ShapeDtypeStruct((n, hk), F32)] * 2
        + [jax.ShapeDtypeStruct((nh, n, LANES), F32)] * 2
        + [jax.ShapeDtypeStruct((3, n, 3 * hk), F32)],
        scratch_shapes=[pltpu.VMEM((n, hk), F32), pltpu.VMEM((n, hk), F32)],
        compiler_params=_params("arbitrary"),
        name="gdn_decode_in",
    )(x, mod, p["wqkv"], p["wg"], p["wab"], p["cw"], p["alr"], p["dtr"], conv_state)


def _gdn_dec_state_kernel(qt_ref, kt_ref, v_ref, gz_ref, eg_ref, be_ref, ng_ref, s0_ref,
                          y_ref, s_ref, o_s):
    n = s0_ref.shape[0]
    for b in range(n):
        kc = kt_ref[:, b:b + 1]
        qc = qt_ref[:, b:b + 1]
        s0 = s0_ref[b]
        k_s0 = jnp.sum(kc * s0, axis=0, keepdims=True)
        q_s0 = jnp.sum(qc * s0, axis=0, keepdims=True)
        qk = jnp.sum(qc * kc, axis=0, keepdims=True)
        e_g = eg_ref[b:b + 1, :]
        v_new = be_ref[b:b + 1, :] * (v_ref[b:b + 1, :] - e_g * k_s0)
        o_s[b:b + 1, :] = e_g * q_s0 + qk * v_new
        s_ref[b] = e_g * s0 + kc * v_new
    o = o_s[...]
    ms = jnp.mean(o * o, axis=-1, keepdims=True)
    y_ref[...] = (o * lax.rsqrt(ms + EPS) * ng_ref[...] * gz_ref[...]).astype(BF16)


def _gdn_dec_state(qt, kt, v, gz, eg, be, norm_g, s0):
    n, nh, dk, dv = s0.shape
    head = lambda a, bb: pl.BlockSpec((None, a, bb), lambda h: (h, 0, 0))
    col = pl.BlockSpec((n, dv), lambda h: (0, h))
    sspec = pl.BlockSpec((n, None, dk, dv), lambda h: (0, h, 0, 0))
    return pl.pallas_call(
        _gdn_dec_state_kernel,
        grid=(nh,),
        in_specs=[head(dk, n), head(dk, n), col, col, head(n, LANES), head(n, LANES),
                  pl.BlockSpec((1, dv), lambda h: (0, 0)), sspec],
        out_specs=[col, sspec],
        out_shape=[jax.ShapeDtypeStruct((n, nh * dv), BF16), jax.ShapeDtypeStruct(s0.shape, F32)],
        scratch_shapes=[pltpu.VMEM((n, dv), F32)],
        compiler_params=_params("parallel"),
        name="gdn_decode_state",
    )(qt, kt, v, gz, eg, be, norm_g, s0)


def _ssd_xbc(u, wx_ref, conv_fn, xs_ref, bm_ref, cm_ref):
    di = xs_ref.shape[-1]
    gn = bm_ref.shape[-1]
    blk = 1024
    for j in range(di // blk):
        cs = slice(j * blk, (j + 1) * blk)
        pre = jnp.dot(u, wx_ref[:, cs], preferred_element_type=F32)
        xs_ref[:, cs] = _silu(conv_fn(pre, cs))
    for dst, off in ((bm_ref, di), (cm_ref, di + gn)):
        cs = slice(off, off + gn)
        pre = jnp.dot(u, wx_ref[:, cs], preferred_element_type=F32)
        dst[...] = _silu(conv_fn(pre, cs))


def _ssd_in_kernel(x_ref, mod_ref, wz_ref, wx_ref, wdt_ref, wdtt_ref, cw_ref, cb_ref, dtr_ref, alr_ref,
                   dtc_ref, alc_ref, zs_ref, xs_ref, bm_ref, cm_ref, da_ref, at_ref, tail_ref):
    t = x_ref.shape[0]
    nh = at_ref.shape[0]

    @pl.when(pl.program_id(1) == 0)
    def _():
        tail_ref[...] = jnp.zeros_like(tail_ref)

    u = _modnorm(x_ref[...], mod_ref[1], mod_ref[0]).astype(BF16)
    di = zs_ref.shape[-1]
    for j in range(di // 1024):
        cs = slice(j * 1024, (j + 1) * 1024)
        zs_ref[:, cs] = _silu(jnp.dot(u, wz_ref[:, cs], preferred_element_type=F32))

    def conv_fn(pre, cs):
        out = _conv_seq(pre, tail_ref[:, cs], cw_ref.at[:, cs], cb_ref[:, cs])
        tail_ref[:, cs] = pre[t - SUBLANES:, :]
        return out

    _ssd_xbc(u, wx_ref, conv_fn, xs_ref, bm_ref, cm_ref)

    dt_c = jax.nn.softplus(jnp.dot(u, wdt_ref[...], preferred_element_type=F32) + dtr_ref[...])
    dt_r = jax.nn.softplus(lax.dot_general(wdtt_ref[...], u, (((1,), (1,)), ((), ())),
                                           preferred_element_type=F32) + dtc_ref[...])
    acs_c = _sdot_r(_block_tri(t, SSD_CHUNK, True), dt_c * -jnp.exp(alr_ref[...]))
    acs_r = _sdot_l(dt_r * -jnp.exp(alc_ref[...]), _block_tri(t, SSD_CHUNK, False))
    lane = lax.broadcasted_iota(jnp.int32, dt_c.shape, 1)
    da_ref[...] = jnp.where(lane < nh, dt_c, pltpu.roll(acs_c, nh, 1))
    at_ref[...] = acs_r[:nh, :]


def _ssd_in(x, mod, p):
    b, l, d = x.shape
    di = p["wz"].shape[1]
    cd = p["wx"].shape[1]
    gn = (cd - di) // 2
    nh = di // SSM_HEADDIM
    t = ROW_TILE
    tok = lambda width: pl.BlockSpec((None, t, width), lambda i, c: (i, c, 0))
    return pl.pallas_call(
        _ssd_in_kernel,
        grid=(b, l // t),
        in_specs=[tok(d), pl.BlockSpec((6, None, 1, d), lambda i, c: (0, i, 0, 0)),
                  _const_spec((d, di)), _const_spec((d, cd)), _const_spec((d, LANES)),
                  _const_spec((LANES, d)), _const_spec((4, cd)), _const_spec((1, cd)),
                  _const_spec((1, LANES)), _const_spec((1, LANES)),
                  _const_spec((LANES, 1)), _const_spec((LANES, 1))],
        out_specs=[tok(di), tok(di), tok(gn), tok(gn), tok(LANES),
                   pl.BlockSpec((None, nh, t), lambda i, c: (i, 0, c)),
                   pl.BlockSpec((None, SUBLANES, cd), lambda i, c: (i, 0, 0))],
        out_shape=[jax.ShapeDtypeStruct((b, l, di), F32), jax.ShapeDtypeStruct((b, l, di), F32),
                   jax.ShapeDtypeStruct((b, l, gn), F32), jax.ShapeDtypeStruct((b, l, gn), F32),
                   jax.ShapeDtypeStruct((b, l, LANES), F32),
                   jax.ShapeDtypeStruct((b, nh, l), F32),
                   jax.ShapeDtypeStruct((b, SUBLANES, cd), F32)],
        compiler_params=_params("parallel", "arbitrary"),
        name="ssd_in",
    )(x, mod, p["wz"], p["wx"], p["wdt"], p["wdtt"], p["cw"], p["cb"], p["dtr"], p["alr"], p["dtc"], p["alc"])


def _ssd_scan_kernel(zs_ref, xs_ref, bm_ref, cm_ref, da_ref, at_ref, e1_ref, e2_ref, dsk_ref, ng_ref,
                     y_ref, hout_ref, ht_s, y_s):
    c = xs_ref.shape[0]
    di = xs_ref.shape[1]
    n = SSM_STATE
    pdim = SSM_HEADDIM
    gw = di // SSM_GROUPS
    hpg = gw // pdim
    nh_total = di // pdim
    ci = pl.program_id(1)

    @pl.when(ci == 0)
    def _():
        ht_s[...] = jnp.zeros_like(ht_s)

    da = da_ref[...]
    lane_a = lax.broadcasted_iota(jnp.int32, da.shape, 1)
    is_acs = (lane_a >= nh_total) & (lane_a < 2 * nh_total)
    acs = jnp.where(is_acs, da, 0.0)
    acs_last = acs[c - 1:c, :]
    dt_e = _sdot_l(jnp.where(lane_a < nh_total, da, 0.0), e1_ref[...])
    eacs_e = _sdot_l(jnp.exp(acs), e2_ref[...])
    ds_e = _sdot_l(jnp.exp(acs_last - acs), e2_ref[...])
    xs = xs_ref[...]
    xdt = xs * dt_e
    xd = xdt * ds_e

    r = lax.broadcasted_iota(jnp.int32, (c, c), 0)
    cc = lax.broadcasted_iota(jnp.int32, (c, c), 1)
    causal = r >= cc
    lane = lax.broadcasted_iota(jnp.int32, (c, 2 * pdim), 1)

    for g in range(SSM_GROUPS):
        gs = slice(g * gw, (g + 1) * gw)
        bg = bm_ref[:, g * n:(g + 1) * n]
        cg = cm_ref[:, g * n:(g + 1) * n]
        cb = _bdot_nt(cg, bg)
        ht = ht_s[g]
        y_off = _bdot(cg, ht) * eacs_e[:, gs]
        ht_s[g] = eacs_e[c - 1:c, gs] * ht + _bdot(bg.T, xd[:, gs])
        for pr in range(hpg // 2):
            ms = []
            for hh in (2 * pr, 2 * pr + 1):
                h = g * hpg + hh
                seg = da[:, nh_total + h:nh_total + h + 1] - at_ref[h:h + 1, :]
                lm = jnp.where(causal, jnp.exp(jnp.where(causal, seg, 0.0)), 0.0)
                ms.append((cb * lm).astype(BF16))
            ps = slice(g * gw + pr * 2 * pdim, g * gw + (pr + 1) * 2 * pdim)
            xp = xdt[:, ps].astype(BF16)
            zero = jnp.zeros_like(xp)
            rhs = jnp.concatenate([jnp.where(lane < pdim, xp, zero), jnp.where(lane >= pdim, xp, zero)], axis=0)
            y_d = jnp.dot(jnp.concatenate(ms, axis=1), rhs, preferred_element_type=F32)
            os_ = slice(pr * 2 * pdim, (pr + 1) * 2 * pdim)
            y_s[:, os_] = y_d + y_off[:, os_] + dsk_ref[:, ps] * xs[:, ps]
        yz = y_s[...] * zs_ref[:, gs]
        ms_ = jnp.mean(yz * yz, axis=-1, keepdims=True)
        y_ref[:, gs] = (yz * lax.rsqrt(ms_ + EPS) * ng_ref[:, gs]).astype(BF16)

    @pl.when(ci == pl.num_programs(1) - 1)
    def _():
        for g in range(SSM_GROUPS):
            hout_ref[g * gw:(g + 1) * gw, :] = ht_s[g].T


def _ssd_scan(zs, xs, bm, cm, da, at, p):
    b, l, di = xs.shape
    gn = bm.shape[-1]
    nh = di // SSM_HEADDIM
    gw = di // SSM_GROUPS
    t = SSD_CHUNK
    tok = lambda width: pl.BlockSpec((None, t, width), lambda i, c: (i, c, 0))
    return pl.pallas_call(
        _ssd_scan_kernel,
        grid=(b, l // t),
        in_specs=[tok(di), tok(di), tok(gn), tok(gn), tok(LANES),
                  pl.BlockSpec((None, nh, t), lambda i, c: (i, 0, c)),
                  _const_spec((LANES, di)), _const_spec((LANES, di)),
                  _const_spec((1, di)), _const_spec((1, di))],
        out_specs=[tok(di), pl.BlockSpec((None, di, SSM_STATE), lambda i, c: (i, 0, 0))],
        out_shape=[jax.ShapeDtypeStruct((b, l, di), BF16),
                   jax.ShapeDtypeStruct((b, di, SSM_STATE), F32)],
        scratch_shapes=[pltpu.VMEM((SSM_GROUPS, SSM_STATE, gw), F32), pltpu.VMEM((t, gw), F32)],
        compiler_params=_params("parallel", "arbitrary"),
        name="ssd_scan",
    )(zs, xs, bm, cm, da, at, p["e1"], p["e2"], p["dsk"], p["ng"])


def _ssd_dec_in_kernel(x_ref, mod_ref, wz_ref, wx_ref, wdt_ref, cw_ref, cb_ref, dtr_ref, alr_ref,
                       e1_ref, cs_ref, zs_ref, xs_ref, bm_ref, cm_ref, xt_ref, ea_ref, xdt_ref, eae_ref,
                       nb_ref):
    n = x_ref.shape[0]
    di = zs_ref.shape[-1]
    nh = di // SSM_HEADDIM
    u = _modnorm(x_ref[...], mod_ref[1], mod_ref[0]).astype(BF16)
    for j in range(di // 1024):
        cs = slice(j * 1024, (j + 1) * 1024)
        zs_ref[:, cs] = _silu(jnp.dot(u, wz_ref[:, cs], preferred_element_type=F32))

    def conv_fn(pre, cs):
        nb_ref[0, :, cs] = cs_ref[1, :, cs]
        nb_ref[1, :, cs] = cs_ref[2, :, cs]
        nb_ref[2, :, cs] = pre
        return (cs_ref[0, :, cs] * cw_ref[0:1, cs] + cs_ref[1, :, cs] * cw_ref[1:2, cs]
                + cs_ref[2, :, cs] * cw_ref[2:3, cs] + pre * cw_ref[3:4, cs] + cb_ref[:, cs])

    _ssd_xbc(u, wx_ref, conv_fn, xs_ref, bm_ref, cm_ref)
    dt = jax.nn.softplus(jnp.dot(u, wdt_ref[...], preferred_element_type=F32) + dtr_ref[...])
    e_a = jnp.exp(dt * -jnp.exp(alr_ref[...]))
    xdt_ref[...] = xs_ref[...] * _sdot_l(dt, e1_ref[...])
    eae_ref[...] = _sdot_l(e_a, e1_ref[...])
    for j in range(di // LANES):
        xt_ref[j * LANES:(j + 1) * LANES, :] = xdt_ref[:, j * LANES:(j + 1) * LANES].T
    for h in range(nh):
        ea_ref[h] = jnp.broadcast_to(e_a[:, h:h + 1], (n, LANES))


def _ssd_dec_in(x, mod, p, conv_state):
    n, d = x.shape
    di = p["wz"].shape[1]
    cd = p["wx"].shape[1]
    gn = (cd - di) // 2
    nh = di // SSM_HEADDIM
    full = lambda shape: pl.BlockSpec(shape, lambda i: (0,) * len(shape))
    return pl.pallas_call(
        _ssd_dec_in_kernel,
        grid=(1,),
        in_specs=[full((n, d)), full((6, n, d)), full((d, di)), full((d, cd)), full((d, LANES)),
                  full((4, cd)), full((1, cd)), full((1, LANES)), full((1, LANES)),
                  full((LANES, di)), full((3, n, cd))],
        out_specs=[full((n, di)), full((n, di)), full((n, gn)), full((n, gn)), full((di, n)),
                   full((nh, n, LANES)), full((n, di)), full((n, di)), full((3, n, cd))],
        out_shape=[jax.ShapeDtypeStruct((n, di), F32), jax.ShapeDtypeStruct((n, di), F32),
                   jax.ShapeDtypeStruct((n, gn), F32), jax.ShapeDtypeStruct((n, gn), F32),
                   jax.ShapeDtypeStruct((di, n), F32),
                   jax.ShapeDtypeStruct((nh, n, LANES), F32),
                   jax.ShapeDtypeStruct((n, di), F32), jax.ShapeDtypeStruct((n, di), F32),
                   jax.ShapeDtypeStruct((3, n, cd), F32)],
        compiler_params=_params("arbitrary"),
        name="ssd_decode_in",
    )(x, mod, p["wz"], p["wx"], p["wdt"], p["cw"], p["cb"], p["dtr"], p["alr"], p["e1"], conv_state)


def _ssd_dec_state_kernel(xt_ref, bm_ref, cm_ref, ea_ref, h0_ref, yo_ref, h_ref):
    n = h0_ref.shape[0]
    pdim = h0_ref.shape[2]
    for b in range(n):
        bb = bm_ref[b:b + 1, :]
        c8 = jnp.broadcast_to(cm_ref[b:b + 1, :], (SUBLANES, bb.shape[1]))
        outs = []
        for hh in range(2):
            h0 = h0_ref[b, hh]
            outs.append(_bdot_nt(c8, h0)[0:1, :])
            xc = xt_ref[hh * pdim:(hh + 1) * pdim, b:b + 1]
            h_ref[b, hh] = ea_ref[hh, b:b + 1, :] * h0 + xc * bb
        yo_ref[b:b + 1, :] = jnp.concatenate(outs, axis=1)


def _ssd_dec_state(xt, bm, cm, ea, h0):
    n, nh, pdim, ns = h0.shape
    hpg = nh // SSM_GROUPS
    return pl.pallas_call(
        _ssd_dec_state_kernel,
        grid=(nh // 2,),
        in_specs=[pl.BlockSpec((2 * pdim, n), lambda j: (j, 0)),
                  pl.BlockSpec((n, ns), lambda j: (0, (2 * j) // hpg)),
                  pl.BlockSpec((n, ns), lambda j: (0, (2 * j) // hpg)),
                  pl.BlockSpec((2, n, LANES), lambda j: (j, 0, 0)),
                  pl.BlockSpec((n, 2, pdim, ns), lambda j: (0, j, 0, 0))],
        out_specs=[pl.BlockSpec((n, 2 * pdim), lambda j: (0, j)),
                   pl.BlockSpec((n, 2, pdim, ns), lambda j: (0, j, 0, 0))],
        out_shape=[jax.ShapeDtypeStruct((n, nh * pdim), F32), jax.ShapeDtypeStruct(h0.shape, F32)],
        compiler_params=_params("parallel"),
        name="ssd_decode_state",
    )(xt, bm, cm, ea, h0)


def _ssd_dec_out_kernel(yo_ref, xs_ref, bm_ref, cm_ref, zs_ref, eae_ref, xdt_ref, dsk_ref, ng_ref, y_ref):
    di = xs_ref.shape[1]
    gw = di // SSM_GROUPS
    n = SSM_STATE
    for g in range(SSM_GROUPS):
        gs = slice(g * gw, (g + 1) * gw)
        cb = jnp.sum(cm_ref[:, g * n:(g + 1) * n] * bm_ref[:, g * n:(g + 1) * n], axis=-1, keepdims=True)
        y = cb * xdt_ref[:, gs] + eae_ref[:, gs] * yo_ref[:, gs] + dsk_ref[:, gs] * xs_ref[:, gs]
        yz = y * zs_ref[:, gs]
        ms = jnp.mean(yz * yz, axis=-1, keepdims=True)
        y_ref[:, gs] = (yz * lax.rsqrt(ms + EPS) * ng_ref[:, gs]).astype(BF16)


def _ssd_dec_out(yo, xs, bm, cm, zs, eae, xdt, p):
    n, di = xs.shape
    gn = bm.shape[1]
    full = lambda shape: pl.BlockSpec(shape, lambda i: (0,) * len(shape))
    return pl.pallas_call(
        _ssd_dec_out_kernel,
        grid=(1,),
        in_specs=[full((n, di)), full((n, di)), full((n, gn)), full((n, gn)), full((n, di)),
                  full((n, di)), full((n, di)), full((1, di)), full((1, di))],
        out_specs=full((n, di)),
        out_shape=jax.ShapeDtypeStruct((n, di), BF16),
        compiler_params=_params("arbitrary"),
        name="ssd_decode_out",
    )(yo, xs, bm, cm, zs, eae, xdt, p["dsk"], p["ng"])


def _pad_lanes(v):
    return jnp.pad(v.astype(F32), (0, LANES - v.shape[0])).reshape(1, LANES)


def _pad_cols(w):
    return jnp.pad(w, ((0, 0), (0, LANES - w.shape[1])))


def _rg_params(w_in, conv_w, conv_b, gate_w, gate_b, lam, w_out):
    w = conv_w.shape[-1]
    return dict(wy=w_in[:, :w].astype(BF16), wx=w_in[:, w:].astype(BF16), cw=conv_w,
                cb=conv_b.reshape(1, w), gwr=gate_w[0].astype(BF16), gwi=gate_w[1].astype(BF16),
                gb=gate_b, lam=lam.reshape(1, w), wo=w_out.astype(BF16))


def _gdn_params(w_in, conv_w, a_log, dt_bias, norm_g, w_out):
    qkv = conv_w.shape[-1]
    nh = a_log.shape[0]
    hv = nh * GDN_DV
    wab = _pad_cols(w_in[:, qkv + hv:]).astype(BF16)
    alr, dtr = _pad_lanes(a_log), _pad_lanes(dt_bias)
    return dict(wqkv=w_in[:, :qkv].astype(BF16), wg=w_in[:, qkv:qkv + hv].astype(BF16), wab=wab,
                wabt=wab.T, cw=conv_w, alr=alr, dtr=dtr, alc=alr.T, dtc=dtr.T,
                ng=norm_g.reshape(1, GDN_DV), wo=w_out.astype(BF16))


def _ssd_params(w_in, conv_w, conv_b, a_log, dt_bias, d_skip, norm_g, w_out):
    cd = conv_w.shape[-1]
    nh = a_log.shape[0]
    di = nh * SSM_HEADDIM
    wdt = _pad_cols(w_in[:, di + cd:]).astype(BF16)
    alr, dtr = _pad_lanes(a_log), _pad_lanes(dt_bias)
    head_of = jnp.arange(di, dtype=jnp.int32) // SSM_HEADDIM
    rows = jnp.arange(LANES, dtype=jnp.int32)[:, None]
    e1 = (rows == head_of[None, :]).astype(BF16)
    e2 = (rows == head_of[None, :] + nh).astype(BF16)
    return dict(wz=w_in[:, :di].astype(BF16), wx=w_in[:, di:di + cd].astype(BF16), wdt=wdt, wdtt=wdt.T,
                cw=conv_w, cb=conv_b.reshape(1, cd), alr=alr, dtr=dtr, alc=alr.T, dtc=dtr.T,
                e1=e1, e2=e2, dsk=jnp.repeat(d_skip, SSM_HEADDIM).reshape(1, di),
                ng=norm_g.reshape(1, di), wo=w_out.astype(BF16))


def kernel(x_prompt, x_sample, state_rglru_conv, state_rglru_h, state_gdn_conv, state_gdn_S, state_ssd_conv, state_ssd_h, c_prompt, c_sample, w_mod, b_mod, w_mlp_up, w_mlp_down, final_norm_g, rg_w_in, rg_conv_w, rg_conv_b, rg_gate_w, rg_gate_b, rg_lambda, rg_w_out, gdn_w_in, gdn_conv_w, gdn_A_log, gdn_dt_bias, gdn_norm_g, gdn_w_out, ssd_w_in, ssd_conv_w, ssd_conv_b, ssd_A_log, ssd_dt_bias, ssd_D, ssd_norm_g, ssd_w_out):
    bp, l, d = x_prompt.shape
    ns = x_sample.shape[0]
    assert x_sample.shape[1] == 1 and l % ROW_TILE == 0 and ns % SUBLANES == 0

    mod = _modulation(jnp.concatenate([c_prompt, c_sample], axis=0), w_mod, b_mod)
    mod = mod.reshape(DEPTH, bp + ns, 6, d)
    mod_p = jnp.transpose(mod[:, :bp], (0, 2, 1, 3))[:, :, :, None, :]
    mod_s = jnp.transpose(mod[:, bp:], (0, 2, 1, 3))
    fg = final_norm_g.reshape(1, d)

    xp = x_prompt
    xs = x_sample.reshape(ns, d)
    tails = lambda t: t[:, SUBLANES - 3:, :]
    to_rows = lambda s: jnp.swapaxes(s, 0, 1)
    out = {k: [] for k in ("p_rg_conv", "p_rg_h", "p_gdn_conv", "p_gdn_S", "p_ssd_conv", "p_ssd_h",
                           "s_rg_conv", "s_rg_h", "s_gdn_conv", "s_gdn_S", "s_ssd_conv", "s_ssd_h")}
    for i in range(DEPTH):
        j = i // N_MIXERS
        kind = i % N_MIXERS
        if kind == 0:
            p = _rg_params(rg_w_in[j], rg_conv_w[j], rg_conv_b[j], rg_gate_w[j], rg_gate_b[j],
                           rg_lambda[j], rg_w_out[j])
            yp, tail, h_last = _rg_prompt(xp, mod_p[i], p)
            out["p_rg_conv"].append(tails(tail))
            out["p_rg_h"].append(h_last[:, 0, :])
            ysm, nb, h_new = _rg_decode(xs, mod_s[i], p, to_rows(state_rglru_conv[j]), state_rglru_h[j])
            out["s_rg_conv"].append(to_rows(nb))
            out["s_rg_h"].append(h_new)
        elif kind == 1:
            p = _gdn_params(gdn_w_in[j], gdn_conv_w[j], gdn_A_log[j], gdn_dt_bias[j], gdn_norm_g[j],
                            gdn_w_out[j])
            q, k, v, gz, gb, gr, tail = _gdn_in(xp, mod_p[i], p)
            yp, s_fin = _gdn_scan(q, k, v, gz, gb, gr, p["ng"])
            out["p_gdn_conv"].append(tails(tail))
            out["p_gdn_S"].append(s_fin)
            qt, kt, v1, gz1, eg, be, nb = _gdn_dec_in(xs, mod_s[i], p, to_rows(state_gdn_conv[j]))
            ysm, s_new = _gdn_dec_state(qt, kt, v1, gz1, eg, be, p["ng"], state_gdn_S[j])
            out["s_gdn_conv"].append(to_rows(nb))
            out["s_gdn_S"].append(s_new)
        else:
            p = _ssd_params(ssd_w_in[j], ssd_conv_w[j], ssd_conv_b[j], ssd_A_log[j], ssd_dt_bias[j],
                            ssd_D[j], ssd_norm_g[j], ssd_w_out[j])
            zs, xc, bm, cm, da, at, tail = _ssd_in(xp, mod_p[i], p)
            yp, h_fin = _ssd_scan(zs, xc, bm, cm, da, at, p)
            out["p_ssd_conv"].append(tails(tail))
            out["p_ssd_h"].append(h_fin.reshape(bp, -1, SSM_HEADDIM, SSM_STATE))
            zs1, xc1, bm1, cm1, xt, ea, xdt, eae, nb = _ssd_dec_in(xs, mod_s[i], p, to_rows(state_ssd_conv[j]))
            yo, h_new = _ssd_dec_state(xt, bm1, cm1, ea, state_ssd_h[j])
            ysm = _ssd_dec_out(yo, xc1, bm1, cm1, zs1, eae, xdt, p)
            out["s_ssd_conv"].append(to_rows(nb))
            out["s_ssd_h"].append(h_new)
        wu = w_mlp_up[i].astype(BF16)
        wd = w_mlp_down[i].astype(BF16)
        final = i == DEPTH - 1
        xp = _outproj_mlp(xp.reshape(bp * l, d), yp.reshape(bp * l, -1), mod_p[i], p["wo"], wu, wd, fg,
                          tm=ROW_TILE, rows_per_mod=l, final=final).reshape(bp, l, d)
        xs = _outproj_mlp(xs, ysm, mod_s[i], p["wo"], wu, wd, fg, tm=ns, rows_per_mod=1, final=final)

    st = {k: jnp.stack(v) for k, v in out.items()}
    return (xp, xs.reshape(ns, 1, d),
            st["p_rg_conv"], st["p_rg_h"], st["p_gdn_conv"], st["p_gdn_S"], st["p_ssd_conv"], st["p_ssd_h"],
            st["s_rg_conv"], st["s_rg_h"], st["s_gdn_conv"], st["s_gdn_S"], st["s_ssd_conv"], st["s_ssd_h"])
```

```python
import functools

import jax
import jax.numpy as jnp
from jax import lax
from jax.experimental import pallas as pl
from jax.experimental.pallas import tpu as pltpu

F32 = jnp.float32
BF16 = jnp.bfloat16

DEPTH = 4
N_MIXERS = 3
EPS = 1e-6
RG_C = 8.0
RG_BLOCK = 256
GDN_DK = 128
GDN_DV = 128
SSM_HEADDIM = 64
SSM_STATE = 128
SSM_GROUPS = 4

SUBLANES = 8
LANES = 128

VMEM_LIMIT = 56 * 1024 * 1024
ROW_TILE = 512
GDN_CHUNK = 64
GDN_STEP = 128
SSD_CHUNK = 128
MLP_HCHUNK = 1024


def _params(*sem):
    return pltpu.CompilerParams(dimension_semantics=sem, vmem_limit_bytes=VMEM_LIMIT)


def _const_spec(shape):
    nd = len(shape)
    return pl.BlockSpec(shape, lambda *_: (0,) * nd, pipeline_mode=pl.Buffered(1))


def _bdot(a, b):
    return jnp.dot(a.astype(BF16), b.astype(BF16), preferred_element_type=F32)


def _bdot_nt(a, b):
    return lax.dot_general(a.astype(BF16), b.astype(BF16), (((1,), (1,)), ((), ())),
                           preferred_element_type=F32)


def _split(a):
    hi = a.astype(BF16)
    lo = (a - hi.astype(F32)).astype(BF16)
    return hi, lo


def _sdot_l(a, b_exact):
    hi, lo = _split(a)
    return (jnp.dot(hi, b_exact, preferred_element_type=F32)
            + jnp.dot(lo, b_exact, preferred_element_type=F32))


def _sdot_r(a_exact, b):
    hi, lo = _split(b)
    return (jnp.dot(a_exact, hi, preferred_element_type=F32)
            + jnp.dot(a_exact, lo, preferred_element_type=F32))


def _modnorm(x, sc, sh):
    ms = jnp.mean(x * x, axis=-1, keepdims=True)
    return x * lax.rsqrt(ms + EPS) * (1.0 + sc) + sh


def _silu(x):
    return x * jax.nn.sigmoid(x)


def _conv_seq(xpre, tail, w_ref, bias):
    xcat = jnp.concatenate([tail, xpre], axis=0)
    acc = xpre * w_ref[3:4, :]
    for k in (1, 2, 3):
        shifted = pltpu.roll(xcat, k, 0)[SUBLANES:]
        acc = acc + shifted * w_ref[3 - k:4 - k, :]
    if bias is not None:
        acc = acc + bias
    return acc


def _block_tri(n, chunk, lower):
    r = lax.broadcasted_iota(jnp.int32, (n, n), 0)
    c = lax.broadcasted_iota(jnp.int32, (n, n), 1)
    same = (r // chunk) == (c // chunk)
    tri = (c <= r) if lower else (r <= c)
    return jnp.where(same & tri, 1.0, 0.0).astype(BF16)


def _mod_kernel(c_ref, w_ref, b_ref, o_ref):
    c = c_ref[...]
    o_ref[...] = _bdot(_silu(c), w_ref[...]) + b_ref[...]


def _modulation(c_all, w_mod, b_mod):
    nb, d = c_all.shape
    n6 = w_mod.shape[-1]
    tn = 1024
    return pl.pallas_call(
        _mod_kernel,
        grid=(DEPTH, n6 // tn),
        in_specs=[pl.BlockSpec((nb, d), lambda l, n: (0, 0)),
                  pl.BlockSpec((None, d, tn), lambda l, n: (l, 0, n)),
                  pl.BlockSpec((None, 1, tn), lambda l, n: (l, 0, n))],
        out_specs=pl.BlockSpec((None, nb, tn), lambda l, n: (l, 0, n)),
        out_shape=jax.ShapeDtypeStruct((DEPTH, nb, n6), F32),
        compiler_params=_params("parallel", "parallel"),
        name="modulation",
    )(c_all, w_mod, b_mod.reshape(DEPTH, 1, n6))


def _outproj_mlp_kernel(x_ref, y_ref, mod_ref, wo_ref, wu_ref, wd_ref, fg_ref, o_ref, *, final):
    x1 = x_ref[...] + mod_ref[2] * jnp.dot(y_ref[...], wo_ref[...], preferred_element_type=F32)
    u = _modnorm(x1, mod_ref[4], mod_ref[3]).astype(BF16)
    hidden = wu_ref.shape[1]
    acc = None
    for c in range(hidden // MLP_HCHUNK):
        cs = slice(c * MLP_HCHUNK, (c + 1) * MLP_HCHUNK)
        h = jnp.maximum(jnp.dot(u, wu_ref[:, cs], preferred_element_type=F32), 0.0)
        p = jnp.dot((h * h).astype(BF16), wd_ref[cs, :], preferred_element_type=F32)
        acc = p if acc is None else acc + p
    x2 = x1 + mod_ref[5] * acc
    if final:
        ms = jnp.mean(x2 * x2, axis=-1, keepdims=True)
        x2 = x2 * lax.rsqrt(ms + EPS) * fg_ref[...]
    o_ref[...] = x2


def _outproj_mlp(x, y, mod, w_out, w_up, w_down, final_g, *, tm, rows_per_mod, final):
    n, d = x.shape
    k = y.shape[1]
    hid = w_up.shape[1]
    if rows_per_mod == 1:
        mod_spec = pl.BlockSpec((6, tm, d), lambda i: (0, i, 0))
    else:
        per = rows_per_mod // tm
        mod_spec = pl.BlockSpec((6, None, 1, d), lambda i: (0, i // per, 0, 0))
    return pl.pallas_call(
        functools.partial(_outproj_mlp_kernel, final=final),
        grid=(n // tm,),
        in_specs=[pl.BlockSpec((tm, d), lambda i: (i, 0)),
                  pl.BlockSpec((tm, k), lambda i: (i, 0)),
                  mod_spec,
                  _const_spec((k, d)), _const_spec((d, hid)), _const_spec((hid, d)),
                  _const_spec((1, d))],
        out_specs=pl.BlockSpec((tm, d), lambda i: (i, 0)),
        out_shape=jax.ShapeDtypeStruct((n, d), F32),
        compiler_params=_params("parallel"),
        name="outproj_mlp",
    )(x, y, mod, w_out, w_up, w_down, final_g)


def _rg_gates(xbr, gwr_ref, gwi_ref, gb_ref, logsig, a_s, b_s):
    xb = xbr.astype(BF16)
    nblk = xbr.shape[1] // RG_BLOCK
    for n in range(nblk):
        sl = slice(n * RG_BLOCK, (n + 1) * RG_BLOCK)
        gr = jnp.dot(xb[:, sl], gwr_ref[n], preferred_element_type=F32) + gb_ref[0:1, sl]
        gi = jnp.dot(xb[:, sl], gwi_ref[n], preferred_element_type=F32) + gb_ref[1:2, sl]
        log_a = (RG_C * logsig[:, sl]) * jax.nn.sigmoid(gr)
        a_s[:, sl] = jnp.exp(log_a)
        th = jnp.tanh(log_a)
        b_s[:, sl] = jnp.sqrt(-2.0 * th / (1.0 - th)) * jax.nn.sigmoid(gi) * xbr[:, sl]


def _rg_prompt_kernel(x_ref, mod_ref, wy_ref, wx_ref, cw_ref, cb_ref, gwr_ref, gwi_ref, gb_ref,
                      lam_ref, y_ref, tail_ref, h_ref, a_s, b_s):
    t, w = a_s.shape

    @pl.when(pl.program_id(1) == 0)
    def _():
        tail_ref[...] = jnp.zeros_like(tail_ref)
        h_ref[...] = jnp.zeros_like(h_ref)

    u = _modnorm(x_ref[...], mod_ref[1], mod_ref[0]).astype(BF16)
    xpre = jnp.dot(u, wx_ref[...], preferred_element_type=F32)
    xbr = _conv_seq(xpre, tail_ref[...], cw_ref, cb_ref[...])
    tail_ref[...] = xpre[t - SUBLANES:, :]
    _rg_gates(xbr, gwr_ref, gwi_ref, gb_ref, jax.nn.log_sigmoid(lam_ref[...]), a_s, b_s)

    row = lax.broadcasted_iota(jnp.int32, (SUBLANES, w), 0)

    def group(gi, h):
        r0 = pl.multiple_of(gi * SUBLANES, SUBLANES)
        a = a_s[pl.ds(r0, SUBLANES), :]
        b = b_s[pl.ds(r0, SUBLANES), :]
        for s in (1, 2, 4):
            a_sh = jnp.where(row >= s, pltpu.roll(a, s, 0), 1.0)
            b_sh = jnp.where(row >= s, pltpu.roll(b, s, 0), 0.0)
            b = a * b_sh + b
            a = a * a_sh
        hg = a * h + b
        b_s[pl.ds(r0, SUBLANES), :] = hg
        return hg[SUBLANES - 1:SUBLANES, :]

    h_ref[...] = lax.fori_loop(0, t // SUBLANES, group, h_ref[...], unroll=2)
    y_br = jax.nn.gelu(jnp.dot(u, wy_ref[...], preferred_element_type=F32))
    y_ref[...] = (b_s[...] * y_br).astype(BF16)


def _rg_prompt(x, mod, p):
    b, l, d = x.shape
    w = p["wx"].shape[1]
    t = ROW_TILE
    nblk = w // RG_BLOCK
    return pl.pallas_call(
        _rg_prompt_kernel,
        grid=(b, l // t),
        in_specs=[pl.BlockSpec((None, t, d), lambda i, c: (i, c, 0)),
                  pl.BlockSpec((6, None, 1, d), lambda i, c: (0, i, 0, 0)),
                  _const_spec((d, w)), _const_spec((d, w)),
                  _const_spec((4, w)), _const_spec((1, w)),
                  _const_spec((nblk, RG_BLOCK, RG_BLOCK)), _const_spec((nblk, RG_BLOCK, RG_BLOCK)),
                  _const_spec((2, w)), _const_spec((1, w))],
        out_specs=[pl.BlockSpec((None, t, w), lambda i, c: (i, c, 0)),
                   pl.BlockSpec((None, SUBLANES, w), lambda i, c: (i, 0, 0)),
                   pl.BlockSpec((None, 1, w), lambda i, c: (i, 0, 0))],
        out_shape=[jax.ShapeDtypeStruct((b, l, w), BF16),
                   jax.ShapeDtypeStruct((b, SUBLANES, w), F32),
                   jax.ShapeDtypeStruct((b, 1, w), F32)],
        scratch_shapes=[pltpu.VMEM((t, w), F32), pltpu.VMEM((t, w), F32)],
        compiler_params=_params("parallel", "arbitrary"),
        name="rglru_prompt",
    )(x, mod, p["wy"], p["wx"], p["cw"], p["cb"], p["gwr"], p["gwi"], p["gb"], p["lam"])


def _rg_decode_kernel(x_ref, mod_ref, wy_ref, wx_ref, cw_ref, cb_ref, gwr_ref, gwi_ref, gb_ref,
                      lam_ref, cs_ref, h0_ref, y_ref, nb_ref, h_ref, a_s, b_s):
    u = _modnorm(x_ref[...], mod_ref[1], mod_ref[0]).astype(BF16)
    xpre = jnp.dot(u, wx_ref[...], preferred_element_type=F32)
    xbr = (cs_ref[0] * cw_ref[0:1, :] + cs_ref[1] * cw_ref[1:2, :] + cs_ref[2] * cw_ref[2:3, :]
           + xpre * cw_ref[3:4, :] + cb_ref[...])
    nb_ref[0] = cs_ref[1]
    nb_ref[1] = cs_ref[2]
    nb_ref[2] = xpre
    _rg_gates(xbr, gwr_ref, gwi_ref, gb_ref, jax.nn.log_sigmoid(lam_ref[...]), a_s, b_s)
    h = a_s[...] * h0_ref[...] + b_s[...]
    h_ref[...] = h
    y_br = jax.nn.gelu(jnp.dot(u, wy_ref[...], preferred_element_type=F32))
    y_ref[...] = (h * y_br).astype(BF16)


def _rg_decode(x, mod, p, conv_state, h0):
    n, d = x.shape
    w = p["wx"].shape[1]
    nblk = w // RG_BLOCK
    full = lambda shape: pl.BlockSpec(shape, lambda i: (0,) * len(shape))
    return pl.pallas_call(
        _rg_decode_kernel,
        grid=(1,),
        in_specs=[full((n, d)), full((6, n, d)), full((d, w)), full((d, w)), full((4, w)), full((1, w)),
                  full((nblk, RG_BLOCK, RG_BLOCK)), full((nblk, RG_BLOCK, RG_BLOCK)), full((2, w)),
                  full((1, w)), full((3, n, w)), full((n, w))],
        out_specs=[full((n, w)), full((3, n, w)), full((n, w))],
        out_shape=[jax.ShapeDtypeStruct((n, w), BF16),
                   jax.ShapeDtypeStruct((3, n, w), F32),
                   jax.ShapeDtypeStruct((n, w), F32)],
        scratch_shapes=[pltpu.VMEM((n, w), F32), pltpu.VMEM((n, w), F32)],
        compiler_params=_params("arbitrary"),
        name="rglru_decode",
    )(x, mod, p["wy"], p["wx"], p["cw"], p["cb"], p["gwr"], p["gwi"], p["gb"], p["lam"], conv_state, h0)


def _gdn_qkv(u, wqkv_ref, conv_fn, q_ref, k_ref, v_ref):
    hk = q_ref.shape[-1]
    nh = hk // GDN_DK
    for j, (dst, scale) in enumerate(((q_ref, GDN_DK ** -0.5), (k_ref, 1.0), (v_ref, None))):
        pre = jnp.dot(u, wqkv_ref[:, j * hk:(j + 1) * hk], preferred_element_type=F32)
        act = _silu(conv_fn(pre, j))
        if scale is None:
            dst[...] = act
        else:
            for h in range(nh):
                hs = slice(h * GDN_DK, (h + 1) * GDN_DK)
                xh = act[:, hs]
                ss = jnp.sum(xh * xh, axis=-1, keepdims=True)
                dst[:, hs] = xh * (lax.rsqrt(ss + EPS) * scale)


def _gdn_in_kernel(x_ref, mod_ref, wqkv_ref, wg_ref, wab_ref, wabt_ref, cw_ref, alr_ref, dtr_ref,
                   alc_ref, dtc_ref, q_ref, k_ref, v_ref, gz_ref, gb_ref, gr_ref, tail_ref):
    t = x_ref.shape[0]
    hk = q_ref.shape[-1]
    nh = hk // GDN_DK

    @pl.when(pl.program_id(1) == 0)
    def _():
        tail_ref[...] = jnp.zeros_like(tail_ref)

    u = _modnorm(x_ref[...], mod_ref[1], mod_ref[0]).astype(BF16)

    def conv_fn(pre, j):
        cs = slice(j * hk, (j + 1) * hk)
        out = _conv_seq(pre, tail_ref[:, cs], cw_ref.at[:, cs], None)
        tail_ref[:, cs] = pre[t - SUBLANES:, :]
        return out

    _gdn_qkv(u, wqkv_ref, conv_fn, q_ref, k_ref, v_ref)
    gz_ref[...] = _silu(jnp.dot(u, wg_ref[...], preferred_element_type=F32))

    ab = jnp.dot(u, wab_ref[...], preferred_element_type=F32)
    abt = lax.dot_general(wabt_ref[...], u, (((1,), (1,)), ((), ())),
                          preferred_element_type=F32)
    g_col = -jnp.exp(alr_ref[...]) * jax.nn.softplus(ab + dtr_ref[...])
    g_row = -jnp.exp(alc_ref[...]) * jax.nn.softplus(abt + dtc_ref[...])
    gc = _sdot_r(_block_tri(t, GDN_CHUNK, True), g_col)
    gr = _sdot_l(g_row, _block_tri(t, GDN_CHUNK, False))
    lane = lax.broadcasted_iota(jnp.int32, ab.shape, 1)
    gb_ref[...] = jnp.where(lane < nh, gc, jax.nn.sigmoid(ab))
    gr_ref[...] = gr[:SUBLANES, :]


def _gdn_in(x, mod, p):
    b, l, d = x.shape
    hk = p["wqkv"].shape[1] // 3
    t = ROW_TILE
    tok = lambda width: pl.BlockSpec((None, t, width), lambda i, c: (i, c, 0))
    return pl.pallas_call(
        _gdn_in_kernel,
        grid=(b, l // t),
        in_specs=[tok(d), pl.BlockSpec((6, None, 1, d), lambda i, c: (0, i, 0, 0)),
                  _const_spec((d, 3 * hk)), _const_spec((d, hk)), _const_spec((d, LANES)),
                  _const_spec((LANES, d)), _const_spec((4, 3 * hk)),
                  _const_spec((1, LANES)), _const_spec((1, LANES)),
                  _const_spec((LANES, 1)), _const_spec((LANES, 1))],
        out_specs=[tok(hk), tok(hk), tok(hk), tok(hk), tok(LANES),
                   pl.BlockSpec((None, SUBLANES, t), lambda i, c: (i, 0, c)),
                   pl.BlockSpec((None, SUBLANES, 3 * hk), lambda i, c: (i, 0, 0))],
        out_shape=[jax.ShapeDtypeStruct((b, l, hk), F32)] * 4
        + [jax.ShapeDtypeStruct((b, l, LANES), F32),
           jax.ShapeDtypeStruct((b, SUBLANES, l), F32),
           jax.ShapeDtypeStruct((b, SUBLANES, 3 * hk), F32)],
        compiler_params=_params("parallel", "arbitrary"),
        name="gdn_in",
    )(x, mod, p["wqkv"], p["wg"], p["wab"], p["wabt"], p["cw"], p["alr"], p["dtr"], p["alc"], p["dtc"])


def _gdn_scan_kernel(q_ref, k_ref, v_ref, gz_ref, gb_ref, gr_ref, ng_ref, y_ref, s_ref):
    c = GDN_CHUNK
    nh = s_ref.shape[0]

    @pl.when(pl.program_id(1) == 0)
    def _():
        s_ref[...] = jnp.zeros_like(s_ref)

    r = lax.broadcasted_iota(jnp.int32, (c, c), 0)
    cc = lax.broadcasted_iota(jnp.int32, (c, c), 1)
    causal = r >= cc
    strict = r > cc
    eye = jnp.where(r == cc, 1.0, 0.0)
    merge = []
    s = 1
    while s < c:
        merge.append((r // (2 * s) == cc // (2 * s)) & ((r // s) % 2 == 1) & ((cc // s) % 2 == 0))
        s *= 2

    nchunk = q_ref.shape[0] // c
    pairs = [(j, h) for j in range(nchunk) for h in range(nh)]
    rows = lambda j: slice(j * c, (j + 1) * c)
    cols = lambda h: slice(h * GDN_DK, (h + 1) * GDN_DK)
    g_col = lambda j, h: gb_ref[rows(j), h:h + 1]
    beta_of = lambda j, h: gb_ref[rows(j), nh + h:nh + h + 1]
    g_row = lambda j, h: gr_ref[h:h + 1, rows(j)]

    kks = [_bdot_nt(jnp.concatenate([k_ref[rows(j), cols(h)] * beta_of(j, h), q_ref[rows(j), cols(h)]],
                                    axis=0), k_ref[rows(j), cols(h)]) for j, h in pairs]
    a_s, qk_s = [], []
    for (j, h), kk in zip(pairs, kks):
        decay = jnp.where(causal, jnp.exp(jnp.where(causal, g_col(j, h) - g_row(j, h), 0.0)), 0.0)
        a_s.append(jnp.where(strict, kk[:c] * decay, 0.0))
        qk_s.append((kk[c:] * decay).astype(BF16))
    xs = [eye - jnp.where(merge[0], a, 0.0) for a in a_s]
    for m in merge[1:]:
        ts = [_bdot(jnp.where(m, a, 0.0), x) for a, x in zip(a_s, xs)]
        xs = [x - _bdot(x, t) for x, t in zip(xs, ts)]
    uws = []
    for (j, h), x in zip(pairs, xs):
        kb = k_ref[rows(j), cols(h)] * beta_of(j, h)
        rhs = jnp.concatenate([v_ref[rows(j), cols(h)] * beta_of(j, h), kb * jnp.exp(g_col(j, h))], axis=1)
        uws.append(_bdot(x, rhs))

    for j in range(nchunk):
        states = [s_ref[h] for h in range(nh)]
        ws_qs = [_bdot(jnp.concatenate([uws[j * nh + h][:, GDN_DV:],
                                        q_ref[rows(j), cols(h)] * jnp.exp(g_col(j, h))], axis=0), states[h])
                 for h in range(nh)]
        v_news = [uws[j * nh + h][:, :GDN_DV] - ws_qs[h][:c] for h in range(nh)]
        os_ = [ws_qs[h][c:] + jnp.dot(qk_s[j * nh + h], v_news[h].astype(BF16), preferred_element_type=F32)
               for h in range(nh)]
        for h in range(nh):
            g_last = g_row(j, h)[:, c - 1:c]
            k_dec = k_ref[rows(j), cols(h)] * jnp.exp(g_last - g_col(j, h))
            s_ref[h] = jnp.exp(g_last) * states[h] + lax.dot_general(
                k_dec.astype(BF16), v_news[h].astype(BF16), (((0,), (0,)), ((), ())),
                preferred_element_type=F32)
        for h in range(nh):
            o = os_[h]
            ms = jnp.mean(o * o, axis=-1, keepdims=True)
            y_ref[rows(j), cols(h)] = (o * lax.rsqrt(ms + EPS) * ng_ref[...]
                                       * gz_ref[rows(j), cols(h)]).astype(BF16)


def _gdn_scan(q, k, v, gz, gb, gr, norm_g):
    b, l, hk = q.shape
    nh = hk // GDN_DK
    t = GDN_STEP
    tok = lambda width: pl.BlockSpec((None, t, width), lambda i, c: (i, c, 0))
    return pl.pallas_call(
        _gdn_scan_kernel,
        grid=(b, l // t),
        in_specs=[tok(hk), tok(hk), tok(hk), tok(hk), tok(LANES),
                  pl.BlockSpec((None, SUBLANES, t), lambda i, c: (i, 0, c)),
                  _const_spec((1, GDN_DV))],
        out_specs=[tok(hk), pl.BlockSpec((None, nh, GDN_DK, GDN_DV), lambda i, c: (i, 0, 0, 0))],
        out_shape=[jax.ShapeDtypeStruct((b, l, hk), BF16),
                   jax.ShapeDtypeStruct((b, nh, GDN_DK, GDN_DV), F32)],
        compiler_params=_params("parallel", "arbitrary"),
        name="gdn_scan",
    )(q, k, v, gz, gb, gr, norm_g)


def _gdn_dec_in_kernel(x_ref, mod_ref, wqkv_ref, wg_ref, wab_ref, cw_ref, alr_ref, dtr_ref, cs_ref,
                       qt_ref, kt_ref, v_ref, gz_ref, eg_ref, be_ref, nb_ref, q_s, k_s):
    hk = v_ref.shape[-1]
    nh = hk // GDN_DK
    u = _modnorm(x_ref[...], mod_ref[1], mod_ref[0]).astype(BF16)

    def conv_fn(pre, j):
        cs = slice(j * hk, (j + 1) * hk)
        nb_ref[0, :, cs] = cs_ref[1, :, cs]
        nb_ref[1, :, cs] = cs_ref[2, :, cs]
        nb_ref[2, :, cs] = pre
        return (cs_ref[0, :, cs] * cw_ref[0:1, cs] + cs_ref[1, :, cs] * cw_ref[1:2, cs]
                + cs_ref[2, :, cs] * cw_ref[2:3, cs] + pre * cw_ref[3:4, cs])

    _gdn_qkv(u, wqkv_ref, conv_fn, q_s, k_s, v_ref)
    gz_ref[...] = _silu(jnp.dot(u, wg_ref[...], preferred_element_type=F32))
    ab = jnp.dot(u, wab_ref[...], preferred_element_type=F32)
    e_g = jnp.exp(-jnp.exp(alr_ref[...]) * jax.nn.softplus(ab + dtr_ref[...]))
    beta = jax.nn.sigmoid(ab)
    n = ab.shape[0]
    for h in range(nh):
        hs = slice(h * GDN_DK, (h + 1) * GDN_DK)
        qt_ref[h] = q_s[:, hs].T
        kt_ref[h] = k_s[:, hs].T
        eg_ref[h] = jnp.broadcast_to(e_g[:, h:h + 1], (n, LANES))
        be_ref[h] = jnp.broadcast_to(beta[:, nh + h:nh + h + 1], (n, LANES))


def _gdn_dec_in(x, mod, p, conv_state):
    n, d = x.shape
    hk = p["wqkv"].shape[1] // 3
    nh = hk // GDN_DK
    full = lambda shape: pl.BlockSpec(shape, lambda i: (0,) * len(shape))
    return pl.pallas_call(
        _gdn_dec_in_kernel,
        grid=(1,),
        in_specs=[full((n, d)), full((6, n, d)), full((d, 3 * hk)), full((d, hk)), full((d, LANES)),
                  full((4, 3 * hk)), full((1, LANES)), full((1, LANES)), full((3, n, 3 * hk))],
        out_specs=[full((nh, GDN_DK, n)), full((nh, GDN_DK, n)), full((n, hk)), full((n, hk)),
                   full((nh, n, LANES)), full((nh, n, LANES)), full((3, n, 3 * hk))],
        out_shape=[jax.ShapeDtypeStruct((nh, GDN_DK, n), F32)] * 2
        + [jax.ShapeDtypeStruct((n, hk), F32)] * 2
        + [jax.ShapeDtypeStruct((nh, n, LANES), F32)] * 2
        + [jax.ShapeDtypeStruct((3, n, 3 * hk), F32)],
        scratch_shapes=[pltpu.VMEM((n, hk), F32), pltpu.VMEM((n, hk), F32)],
        compiler_params=_params("arbitrary"),
        name="gdn_decode_in",
    )(x, mod, p["wqkv"], p["wg"], p["wab"], p["cw"], p["alr"], p["dtr"], conv_state)


def _gdn_dec_state_kernel(qt_ref, kt_ref, v_ref, gz_ref, eg_ref, be_ref, ng_ref, s0_ref,
                          y_ref, s_ref, o_s):
    n = s0_ref.shape[0]
    for b in range(n):
        kc = kt_ref[:, b:b + 1]
        qc = qt_ref[:, b:b + 1]
        s0 = s0_ref[b]
        k_s0 = jnp.sum(kc * s0, axis=0, keepdims=True)
        q_s0 = jnp.sum(qc * s0, axis=0, keepdims=True)
        qk = jnp.sum(qc * kc, axis=0, keepdims=True)
        e_g = eg_ref[b:b + 1, :]
        v_new = be_ref[b:b + 1, :] * (v_ref[b:b + 1, :] - e_g * k_s0)
        o_s[b:b + 1, :] = e_g * q_s0 + qk * v_new
        s_ref[b] = e_g * s0 + kc * v_new
    o = o_s[...]
    ms = jnp.mean(o * o, axis=-1, keepdims=True)
    y_ref[...] = (o * lax.rsqrt(ms + EPS) * ng_ref[...] * gz_ref[...]).astype(BF16)


def _gdn_dec_state(qt, kt, v, gz, eg, be, norm_g, s0):
    n, nh, dk, dv = s0.shape
    head = lambda a, bb: pl.BlockSpec((None, a, bb), lambda h: (h, 0, 0))
    col = pl.BlockSpec((n, dv), lambda h: (0, h))
    sspec = pl.BlockSpec((n, None, dk, dv), lambda h: (0, h, 0, 0))
    return pl.pallas_call(
        _gdn_dec_state_kernel,
        grid=(nh,),
        in_specs=[head(dk, n), head(dk, n), col, col, head(n, LANES), head(n, LANES),
                  pl.BlockSpec((1, dv), lambda h: (0, 0)), sspec],
        out_specs=[col, sspec],
        out_shape=[jax.ShapeDtypeStruct((n, nh * dv), BF16), jax.ShapeDtypeStruct(s0.shape, F32)],
        scratch_shapes=[pltpu.VMEM((n, dv), F32)],
        compiler_params=_params("parallel"),
        name="gdn_decode_state",
    )(qt, kt, v, gz, eg, be, norm_g, s0)


def _ssd_xbc(u, wx_ref, conv_fn, xs_ref, bm_ref, cm_ref):
    di = xs_ref.shape[-1]
    gn = bm_ref.shape[-1]
    blk = 1024
    for j in range(di // blk):
        cs = slice(j * blk, (j + 1) * blk)
        pre = jnp.dot(u, wx_ref[:, cs], preferred_element_type=F32)
        xs_ref[:, cs] = _silu(conv_fn(pre, cs))
    for dst, off in ((bm_ref, di), (cm_ref, di + gn)):
        cs = slice(off, off + gn)
        pre = jnp.dot(u, wx_ref[:, cs], preferred_element_type=F32)
        dst[...] = _silu(conv_fn(pre, cs))


def _ssd_in_kernel(x_ref, mod_ref, wz_ref, wx_ref, wdt_ref, wdtt_ref, cw_ref, cb_ref, dtr_ref, alr_ref,
                   dtc_ref, alc_ref, zs_ref, xs_ref, bm_ref, cm_ref, da_ref, at_ref, tail_ref):
    t = x_ref.shape[0]
    nh = at_ref.shape[0]

    @pl.when(pl.program_id(1) == 0)
    def _():
        tail_ref[...] = jnp.zeros_like(tail_ref)

    u = _modnorm(x_ref[...], mod_ref[1], mod_ref[0]).astype(BF16)
    di = zs_ref.shape[-1]
    for j in range(di // 1024):
        cs = slice(j * 1024, (j + 1) * 1024)
        zs_ref[:, cs] = _silu(jnp.dot(u, wz_ref[:, cs], preferred_element_type=F32))

    def conv_fn(pre, cs):
        out = _conv_seq(pre, tail_ref[:, cs], cw_ref.at[:, cs], cb_ref[:, cs])
        tail_ref[:, cs] = pre[t - SUBLANES:, :]
        return out

    _ssd_xbc(u, wx_ref, conv_fn, xs_ref, bm_ref, cm_ref)

    dt_c = jax.nn.softplus(jnp.dot(u, wdt_ref[...], preferred_element_type=F32) + dtr_ref[...])
    dt_r = jax.nn.softplus(lax.dot_general(wdtt_ref[...], u, (((1,), (1,)), ((), ())),
                                           preferred_element_type=F32) + dtc_ref[...])
    acs_c = _sdot_r(_block_tri(t, SSD_CHUNK, True), dt_c * -jnp.exp(alr_ref[...]))
    acs_r = _sdot_l(dt_r * -jnp.exp(alc_ref[...]), _block_tri(t, SSD_CHUNK, False))
    lane = lax.broadcasted_iota(jnp.int32, dt_c.shape, 1)
    da_ref[...] = jnp.where(lane < nh, dt_c, pltpu.roll(acs_c, nh, 1))
    at_ref[...] = acs_r[:nh, :]


def _ssd_in(x, mod, p):
    b, l, d = x.shape
    di = p["wz"].shape[1]
    cd = p["wx"].shape[1]
    gn = (cd - di) // 2
    nh = di // SSM_HEADDIM
    t = ROW_TILE
    tok = lambda width: pl.BlockSpec((None, t, width), lambda i, c: (i, c, 0))
    return pl.pallas_call(
        _ssd_in_kernel,
        grid=(b, l // t),
        in_specs=[tok(d), pl.BlockSpec((6, None, 1, d), lambda i, c: (0, i, 0, 0)),
                  _const_spec((d, di)), _const_spec((d, cd)), _const_spec((d, LANES)),
                  _const_spec((LANES, d)), _const_spec((4, cd)), _const_spec((1, cd)),
                  _const_spec((1, LANES)), _const_spec((1, LANES)),
                  _const_spec((LANES, 1)), _const_spec((LANES, 1))],
        out_specs=[tok(di), tok(di), tok(gn), tok(gn), tok(LANES),
                   pl.BlockSpec((None, nh, t), lambda i, c: (i, 0, c)),
                   pl.BlockSpec((None, SUBLANES, cd), lambda i, c: (i, 0, 0))],
        out_shape=[jax.ShapeDtypeStruct((b, l, di), F32), jax.ShapeDtypeStruct((b, l, di), F32),
                   jax.ShapeDtypeStruct((b, l, gn), F32), jax.ShapeDtypeStruct((b, l, gn), F32),
                   jax.ShapeDtypeStruct((b, l, LANES), F32),
                   jax.ShapeDtypeStruct((b, nh, l), F32),
                   jax.ShapeDtypeStruct((b, SUBLANES, cd), F32)],
        compiler_params=_params("parallel", "arbitrary"),
        name="ssd_in",
    )(x, mod, p["wz"], p["wx"], p["wdt"], p["wdtt"], p["cw"], p["cb"], p["dtr"], p["alr"], p["dtc"], p["alc"])


def _ssd_scan_kernel(zs_ref, xs_ref, bm_ref, cm_ref, da_ref, at_ref, e1_ref, e2_ref, dsk_ref, ng_ref,
                     y_ref, hout_ref, ht_s, y_s):
    c = xs_ref.shape[0]
    di = xs_ref.shape[1]
    n = SSM_STATE
    pdim = SSM_HEADDIM
    gw = di // SSM_GROUPS
    hpg = gw // pdim
    nh_total = di // pdim
    ci = pl.program_id(1)

    @pl.when(ci == 0)
    def _():
        ht_s[...] = jnp.zeros_like(ht_s)

    da = da_ref[...]
    lane_a = lax.broadcasted_iota(jnp.int32, da.shape, 1)
    is_acs = (lane_a >= nh_total) & (lane_a < 2 * nh_total)
    acs = jnp.where(is_acs, da, 0.0)
    acs_last = acs[c - 1:c, :]
    dt_e = _sdot_l(jnp.where(lane_a < nh_total, da, 0.0), e1_ref[...])
    eacs_e = _sdot_l(jnp.exp(acs), e2_ref[...])
    ds_e = _sdot_l(jnp.exp(acs_last - acs), e2_ref[...])
    xs = xs_ref[...]
    xdt = xs * dt_e
    xd = xdt * ds_e

    r = lax.broadcasted_iota(jnp.int32, (c, c), 0)
    cc = lax.broadcasted_iota(jnp.int32, (c, c), 1)
    causal = r >= cc
    lane = lax.broadcasted_iota(jnp.int32, (c, 2 * pdim), 1)

    for g in range(SSM_GROUPS):
        gs = slice(g * gw, (g + 1) * gw)
        bg = bm_ref[:, g * n:(g + 1) * n]
        cg = cm_ref[:, g * n:(g + 1) * n]
        cb = _bdot_nt(cg, bg)
        ht = ht_s[g]
        y_off = _bdot(cg, ht) * eacs_e[:, gs]
        ht_s[g] = eacs_e[c - 1:c, gs] * ht + _bdot(bg.T, xd[:, gs])
        for pr in range(hpg // 2):
            ms = []
            for hh in (2 * pr, 2 * pr + 1):
                h = g * hpg + hh
                seg = da[:, nh_total + h:nh_total + h + 1] - at_ref[h:h + 1, :]
                lm = jnp.where(causal, jnp.exp(jnp.where(causal, seg, 0.0)), 0.0)
                ms.append((cb * lm).astype(BF16))
            ps = slice(g * gw + pr * 2 * pdim, g * gw + (pr + 1) * 2 * pdim)
            xp = xdt[:, ps].astype(BF16)
            zero = jnp.zeros_like(xp)
            rhs = jnp.concatenate([jnp.where(lane < pdim, xp, zero), jnp.where(lane >= pdim, xp, zero)], axis=0)
            y_d = jnp.dot(jnp.concatenate(ms, axis=1), rhs, preferred_element_type=F32)
            os_ = slice(pr * 2 * pdim, (pr + 1) * 2 * pdim)
            y_s[:, os_] = y_d + y_off[:, os_] + dsk_ref[:, ps] * xs[:, ps]
        yz = y_s[...] * zs_ref[:, gs]
        ms_ = jnp.mean(yz * yz, axis=-1, keepdims=True)
        y_ref[:, gs] = (yz * lax.rsqrt(ms_ + EPS) * ng_ref[:, gs]).astype(BF16)

    @pl.when(ci == pl.num_programs(1) - 1)
    def _():
        for g in range(SSM_GROUPS):
            hout_ref[g * gw:(g + 1) * gw, :] = ht_s[g].T


def _ssd_scan(zs, xs, bm, cm, da, at, p):
    b, l, di = xs.shape
    gn = bm.shape[-1]
    nh = di // SSM_HEADDIM
    gw = di // SSM_GROUPS
    t = SSD_CHUNK
    tok = lambda width: pl.BlockSpec((None, t, width), lambda i, c: (i, c, 0))
    return pl.pallas_call(
        _ssd_scan_kernel,
        grid=(b, l // t),
        in_specs=[tok(di), tok(di), tok(gn), tok(gn), tok(LANES),
                  pl.BlockSpec((None, nh, t), lambda i, c: (i, 0, c)),
                  _const_spec((LANES, di)), _const_spec((LANES, di)),
                  _const_spec((1, di)), _const_spec((1, di))],
        out_specs=[tok(di), pl.BlockSpec((None, di, SSM_STATE), lambda i, c: (i, 0, 0))],
        out_shape=[jax.ShapeDtypeStruct((b, l, di), BF16),
                   jax.ShapeDtypeStruct((b, di, SSM_STATE), F32)],
        scratch_shapes=[pltpu.VMEM((SSM_GROUPS, SSM_STATE, gw), F32), pltpu.VMEM((t, gw), F32)],
        compiler_params=_params("parallel", "arbitrary"),
        name="ssd_scan",
    )(zs, xs, bm, cm, da, at, p["e1"], p["e2"], p["dsk"], p["ng"])


def _ssd_dec_in_kernel(x_ref, mod_ref, wz_ref, wx_ref, wdt_ref, cw_ref, cb_ref, dtr_ref, alr_ref,
                       e1_ref, cs_ref, zs_ref, xs_ref, bm_ref, cm_ref, xt_ref, ea_ref, xdt_ref, eae_ref,
                       nb_ref):
    n = x_ref.shape[0]
    di = zs_ref.shape[-1]
    nh = di // SSM_HEADDIM
    u = _modnorm(x_ref[...], mod_ref[1], mod_ref[0]).astype(BF16)
    for j in range(di // 1024):
        cs = slice(j * 1024, (j + 1) * 1024)
        zs_ref[:, cs] = _silu(jnp.dot(u, wz_ref[:, cs], preferred_element_type=F32))

    def conv_fn(pre, cs):
        nb_ref[0, :, cs] = cs_ref[1, :, cs]
        nb_ref[1, :, cs] = cs_ref[2, :, cs]
        nb_ref[2, :, cs] = pre
        return (cs_ref[0, :, cs] * cw_ref[0:1, cs] + cs_ref[1, :, cs] * cw_ref[1:2, cs]
                + cs_ref[2, :, cs] * cw_ref[2:3, cs] + pre * cw_ref[3:4, cs] + cb_ref[:, cs])

    _ssd_xbc(u, wx_ref, conv_fn, xs_ref, bm_ref, cm_ref)
    dt = jax.nn.softplus(jnp.dot(u, wdt_ref[...], preferred_element_type=F32) + dtr_ref[...])
    e_a = jnp.exp(dt * -jnp.exp(alr_ref[...]))
    xdt_ref[...] = xs_ref[...] * _sdot_l(dt, e1_ref[...])
    eae_ref[...] = _sdot_l(e_a, e1_ref[...])
    for j in range(di // LANES):
        xt_ref[j * LANES:(j + 1) * LANES, :] = xdt_ref[:, j * LANES:(j + 1) * LANES].T
    for h in range(nh):
        ea_ref[h] = jnp.broadcast_to(e_a[:, h:h + 1], (n, LANES))


def _ssd_dec_in(x, mod, p, conv_state):
    n, d = x.shape
    di = p["wz"].shape[1]
    cd = p["wx"].shape[1]
    gn = (cd - di) // 2
    nh = di // SSM_HEADDIM
    full = lambda shape: pl.BlockSpec(shape, lambda i: (0,) * len(shape))
    return pl.pallas_call(
        _ssd_dec_in_kernel,
        grid=(1,),
        in_specs=[full((n, d)), full((6, n, d)), full((d, di)), full((d, cd)), full((d, LANES)),
                  full((4, cd)), full((1, cd)), full((1, LANES)), full((1, LANES)),
                  full((LANES, di)), full((3, n, cd))],
        out_specs=[full((n, di)), full((n, di)), full((n, gn)), full((n, gn)), full((di, n)),
                   full((nh, n, LANES)), full((n, di)), full((n, di)), full((3, n, cd))],
        out_shape=[jax.ShapeDtypeStruct((n, di), F32), jax.ShapeDtypeStruct((n, di), F32),
                   jax.ShapeDtypeStruct((n, gn), F32), jax.ShapeDtypeStruct((n, gn), F32),
                   jax.ShapeDtypeStruct((di, n), F32),
                   jax.ShapeDtypeStruct((nh, n, LANES), F32),
                   jax.ShapeDtypeStruct((n, di), F32), jax.ShapeDtypeStruct((n, di), F32),
                   jax.ShapeDtypeStruct((3, n, cd), F32)],
        compiler_params=_params("arbitrary"),
        name="ssd_decode_in",
    )(x, mod, p["wz"], p["wx"], p["wdt"], p["cw"], p["cb"], p["dtr"], p["alr"], p["e1"], conv_state)


def _ssd_dec_state_kernel(xt_ref, bm_ref, cm_ref, ea_ref, h0_ref, yo_ref, h_ref):
    n = h0_ref.shape[0]
    pdim = h0_ref.shape[2]
    for b in range(n):
        bb = bm_ref[b:b + 1, :]
        c8 = jnp.broadcast_to(cm_ref[b:b + 1, :], (SUBLANES, bb.shape[1]))
        outs = []
        for hh in range(2):
            h0 = h0_ref[b, hh]
            outs.append(_bdot_nt(c8, h0)[0:1, :])
            xc = xt_ref[hh * pdim:(hh + 1) * pdim, b:b + 1]
            h_ref[b, hh] = ea_ref[hh, b:b + 1, :] * h0 + xc * bb
        yo_ref[b:b + 1, :] = jnp.concatenate(outs, axis=1)


def _ssd_dec_state(xt, bm, cm, ea, h0):
    n, nh, pdim, ns = h0.shape
    hpg = nh // SSM_GROUPS
    return pl.pallas_call(
        _ssd_dec_state_kernel,
        grid=(nh // 2,),
        in_specs=[pl.BlockSpec((2 * pdim, n), lambda j: (j, 0)),
                  pl.BlockSpec((n, ns), lambda j: (0, (2 * j) // hpg)),
                  pl.BlockSpec((n, ns), lambda j: (0, (2 * j) // hpg)),
                  pl.BlockSpec((2, n, LANES), lambda j: (j, 0, 0)),
                  pl.BlockSpec((n, 2, pdim, ns), lambda j: (0, j, 0, 0))],
        out_specs=[pl.BlockSpec((n, 2 * pdim), lambda j: (0, j)),
                   pl.BlockSpec((n, 2, pdim, ns), lambda j: (0, j, 0, 0))],
        out_shape=[jax.ShapeDtypeStruct((n, nh * pdim), F32), jax.ShapeDtypeStruct(h0.shape, F32)],
        compiler_params=_params("parallel"),
        name="ssd_decode_state",
    )(xt, bm, cm, ea, h0)


def _ssd_dec_out_kernel(yo_ref, xs_ref, bm_ref, cm_ref, zs_ref, eae_ref, xdt_ref, dsk_ref, ng_ref, y_ref):
    di = xs_ref.shape[1]
    gw = di // SSM_GROUPS
    n = SSM_STATE
    for g in range(SSM_GROUPS):
        gs = slice(g * gw, (g + 1) * gw)
        cb = jnp.sum(cm_ref[:, g * n:(g + 1) * n] * bm_ref[:, g * n:(g + 1) * n], axis=-1, keepdims=True)
        y = cb * xdt_ref[:, gs] + eae_ref[:, gs] * yo_ref[:, gs] + dsk_ref[:, gs] * xs_ref[:, gs]
        yz = y * zs_ref[:, gs]
        ms = jnp.mean(yz * yz, axis=-1, keepdims=True)
        y_ref[:, gs] = (yz * lax.rsqrt(ms + EPS) * ng_ref[:, gs]).astype(BF16)


def _ssd_dec_out(yo, xs, bm, cm, zs, eae, xdt, p):
    n, di = xs.shape
    gn = bm.shape[1]
    full = lambda shape: pl.BlockSpec(shape, lambda i: (0,) * len(shape))
    return pl.pallas_call(
        _ssd_dec_out_kernel,
        grid=(1,),
        in_specs=[full((n, di)), full((n, di)), full((n, gn)), full((n, gn)), full((n, di)),
                  full((n, di)), full((n, di)), full((1, di)), full((1, di))],
        out_specs=full((n, di)),
        out_shape=jax.ShapeDtypeStruct((n, di), BF16),
        compiler_params=_params("arbitrary"),
        name="ssd_decode_out",
    )(yo, xs, bm, cm, zs, eae, xdt, p["dsk"], p["ng"])


def _pad_lanes(v):
    return jnp.pad(v.astype(F32), (0, LANES - v.shape[0])).reshape(1, LANES)


def _pad_cols(w):
    return jnp.pad(w, ((0, 0), (0, LANES - w.shape[1])))


def _rg_params(w_in, conv_w, conv_b, gate_w, gate_b, lam, w_out):
    w = conv_w.shape[-1]
    return dict(wy=w_in[:, :w].astype(BF16), wx=w_in[:, w:].astype(BF16), cw=conv_w,
                cb=conv_b.reshape(1, w), gwr=gate_w[0].astype(BF16), gwi=gate_w[1].astype(BF16),
                gb=gate_b, lam=lam.reshape(1, w), wo=w_out.astype(BF16))


def _gdn_params(w_in, conv_w, a_log, dt_bias, norm_g, w_out):
    qkv = conv_w.shape[-1]
    nh = a_log.shape[0]
    hv = nh * GDN_DV
    wab = _pad_cols(w_in[:, qkv + hv:]).astype(BF16)
    alr, dtr = _pad_lanes(a_log), _pad_lanes(dt_bias)
    return dict(wqkv=w_in[:, :qkv].astype(BF16), wg=w_in[:, qkv:qkv + hv].astype(BF16), wab=wab,
                wabt=wab.T, cw=conv_w, alr=alr, dtr=dtr, alc=alr.T, dtc=dtr.T,
                ng=norm_g.reshape(1, GDN_DV), wo=w_out.astype(BF16))


def _ssd_params(w_in, conv_w, conv_b, a_log, dt_bias, d_skip, norm_g, w_out):
    cd = conv_w.shape[-1]
    nh = a_log.shape[0]
    di = nh * SSM_HEADDIM
    wdt = _pad_cols(w_in[:, di + cd:]).astype(BF16)
    alr, dtr = _pad_lanes(a_log), _pad_lanes(dt_bias)
    head_of = jnp.arange(di, dtype=jnp.int32) // SSM_HEADDIM
    rows = jnp.arange(LANES, dtype=jnp.int32)[:, None]
    e1 = (rows == head_of[None, :]).astype(BF16)
    e2 = (rows == head_of[None, :] + nh).astype(BF16)
    return dict(wz=w_in[:, :di].astype(BF16), wx=w_in[:, di:di + cd].astype(BF16), wdt=wdt, wdtt=wdt.T,
                cw=conv_w, cb=conv_b.reshape(1, cd), alr=alr, dtr=dtr, alc=alr.T, dtc=dtr.T,
                e1=e1, e2=e2, dsk=jnp.repeat(d_skip, SSM_HEADDIM).reshape(1, di),
                ng=norm_g.reshape(1, di), wo=w_out.astype(BF16))


def kernel(x_prompt, x_sample, state_rglru_conv, state_rglru_h, state_gdn_conv, state_gdn_S, state_ssd_conv, state_ssd_h, c_prompt, c_sample, w_mod, b_mod, w_mlp_up, w_mlp_down, final_norm_g, rg_w_in, rg_conv_w, rg_conv_b, rg_gate_w, rg_gate_b, rg_lambda, rg_w_out, gdn_w_in, gdn_conv_w, gdn_A_log, gdn_dt_bias, gdn_norm_g, gdn_w_out, ssd_w_in, ssd_conv_w, ssd_conv_b, ssd_A_log, ssd_dt_bias, ssd_D, ssd_norm_g, ssd_w_out):
    bp, l, d = x_prompt.shape
    ns = x_sample.shape[0]
    assert x_sample.shape[1] == 1 and l % ROW_TILE == 0 and ns % SUBLANES == 0

    mod = _modulation(jnp.concatenate([c_prompt, c_sample], axis=0), w_mod, b_mod)
    mod = mod.reshape(DEPTH, bp + ns, 6, d)
    mod_p = jnp.transpose(mod[:, :bp], (0, 2, 1, 3))[:, :, :, None, :]
    mod_s = jnp.transpose(mod[:, bp:], (0, 2, 1, 3))
    fg = final_norm_g.reshape(1, d)

    xp = x_prompt
    xs = x_sample.reshape(ns, d)
    tails = lambda t: t[:, SUBLANES - 3:, :]
    to_rows = lambda s: jnp.swapaxes(s, 0, 1)
    out = {k: [] for k in ("p_rg_conv", "p_rg_h", "p_gdn_conv", "p_gdn_S", "p_ssd_conv", "p_ssd_h",
                           "s_rg_conv", "s_rg_h", "s_gdn_conv", "s_gdn_S", "s_ssd_conv", "s_ssd_h")}
    for i in range(DEPTH):
        j = i // N_MIXERS
        kind = i % N_MIXERS
        if kind == 0:
            p = _rg_params(rg_w_in[j], rg_conv_w[j], rg_conv_b[j], rg_gate_w[j], rg_gate_b[j],
                           rg_lambda[j], rg_w_out[j])
            yp, tail, h_last = _rg_prompt(xp, mod_p[i], p)
            out["p_rg_conv"].append(tails(tail))
            out["p_rg_h"].append(h_last[:, 0, :])
            ysm, nb, h_new = _rg_decode(xs, mod_s[i], p, to_rows(state_rglru_conv[j]), state_rglru_h[j])
            out["s_rg_conv"].append(to_rows(nb))
            out["s_rg_h"].append(h_new)
        elif kind == 1:
            p = _gdn_params(gdn_w_in[j], gdn_conv_w[j], gdn_A_log[j], gdn_dt_bias[j], gdn_norm_g[j],
                            gdn_w_out[j])
            q, k, v, gz, gb, gr, tail = _gdn_in(xp, mod_p[i], p)
            yp, s_fin = _gdn_scan(q, k, v, gz, gb, gr, p["ng"])
            out["p_gdn_conv"].append(tails(tail))
            out["p_gdn_S"].append(s_fin)
            qt, kt, v1, gz1, eg, be, nb = _gdn_dec_in(xs, mod_s[i], p, to_rows(state_gdn_conv[j]))
            ysm, s_new = _gdn_dec_state(qt, kt, v1, gz1, eg, be, p["ng"], state_gdn_S[j])
            out["s_gdn_conv"].append(to_rows(nb))
            out["s_gdn_S"].append(s_new)
        else:
            p = _ssd_params(ssd_w_in[j], ssd_conv_w[j], ssd_conv_b[j], ssd_A_log[j], ssd_dt_bias[j],
                            ssd_D[j], ssd_norm_g[j], ssd_w_out[j])
            zs, xc, bm, cm, da, at, tail = _ssd_in(xp, mod_p[i], p)
            yp, h_fin = _ssd_scan(zs, xc, bm, cm, da, at, p)
            out["p_ssd_conv"].append(tails(tail))
            out["p_ssd_h"].append(h_fin.reshape(bp, -1, SSM_HEADDIM, SSM_STATE))
            zs1, xc1, bm1, cm1, xt, ea, xdt, eae, nb = _ssd_dec_in(xs, mod_s[i], p, to_rows(state_ssd_conv[j]))
            yo, h_new = _ssd_dec_state(xt, bm1, cm1, ea, state_ssd_h[j])
            ysm = _ssd_dec_out(yo, xc1, bm1, cm1, zs1, eae, xdt, p)
            out["s_ssd_conv"].append(to_rows(nb))
            out["s_ssd_h"].append(h_new)
        wu = w_mlp_up[i].astype(BF16)
        wd = w_mlp_down[i].astype(BF16)
        final = i == DEPTH - 1
        xp = _outproj_mlp(xp.reshape(bp * l, d), yp.reshape(bp * l, -1), mod_p[i], p["wo"], wu, wd, fg,
                          tm=ROW_TILE, rows_per_mod=l, final=final).reshape(bp, l, d)
        xs = _outproj_mlp(xs, ysm, mod_s[i], p["wo"], wu, wd, fg, tm=ns, rows_per_mod=1, final=final)

    st = {k: jnp.stack(v) for k, v in out.items()}
    return (xp, xs.reshape(ns, 1, d),
            st["p_rg_conv"], st["p_rg_h"], st["p_gdn_conv"], st["p_gdn_S"], st["p_ssd_conv"], st["p_ssd_h"],
            st["s_rg_conv"], st["s_rg_h"], st["s_gdn_conv"], st["s_gdn_S"], st["s_ssd_conv"], st["s_ssd_h"])
```

```python
import functools

import jax
import jax.numpy as jnp
from jax import lax
from jax.experimental import pallas as pl
from jax.experimental.pallas import tpu as pltpu

F32 = jnp.float32
BF16 = jnp.bfloat16

DEPTH = 4
N_MIXERS = 3
EPS = 1e-6
RG_C = 8.0
RG_BLOCK = 256
GDN_DK = 128
GDN_DV = 128
SSM_HEADDIM = 64
SSM_STATE = 128
SSM_GROUPS = 4

SUBLANES = 8
LANES = 128

VMEM_LIMIT = 56 * 1024 * 1024
ROW_TILE = 512
GDN_CHUNK = 64
GDN_STEP = 128
GDN_SEQS = 1
SSD_CHUNK = 128
MLP_HCHUNK = 1024
WBLK = 1024


def _params(*sem):
    return pltpu.CompilerParams(dimension_semantics=sem, vmem_limit_bytes=VMEM_LIMIT)


def _w(arr, blk, idx):
    return arr, pl.BlockSpec(blk, lambda *_: idx, pipeline_mode=pl.Buffered(1))


def _unzip(pairs):
    return [a for a, _ in pairs], [s for _, s in pairs]


def _bdot(a, b):
    return jnp.dot(a.astype(BF16), b.astype(BF16), preferred_element_type=F32)


def _bdot_nt(a, b):
    return lax.dot_general(a.astype(BF16), b.astype(BF16), (((1,), (1,)), ((), ())),
                           preferred_element_type=F32)


def _split(a):
    hi = a.astype(BF16)
    lo = (a - hi.astype(F32)).astype(BF16)
    return hi, lo


def _sdot_l(a, b_exact):
    hi, lo = _split(a)
    return (jnp.dot(hi, b_exact, preferred_element_type=F32)
            + jnp.dot(lo, b_exact, preferred_element_type=F32))


def _sdot_r(a_exact, b):
    hi, lo = _split(b)
    return (jnp.dot(a_exact, hi, preferred_element_type=F32)
            + jnp.dot(a_exact, lo, preferred_element_type=F32))


def _modnorm(x, sc, sh):
    ms = jnp.mean(x * x, axis=-1, keepdims=True)
    return x * lax.rsqrt(ms + EPS) * (1.0 + sc) + sh


def _sigmoid(x):
    return 0.5 * jnp.tanh(0.5 * x) + 0.5


def _silu(x):
    hx = 0.5 * x
    return hx * jnp.tanh(hx) + hx


def _conv_seq(xpre, tail, w_ref, bias):
    xcat = jnp.concatenate([tail, xpre], axis=0)
    acc = xpre * w_ref[3:4, :]
    for k in (1, 2, 3):
        shifted = pltpu.roll(xcat, k, 0)[SUBLANES:]
        acc = acc + shifted * w_ref[3 - k:4 - k, :]
    if bias is not None:
        acc = acc + bias
    return acc


def _pipelined(dots, consume, bufs):
    bufs[0][...] = dots[0]()
    for j in range(len(dots)):
        if j + 1 < len(dots):
            bufs[(j + 1) % 2][...] = dots[j + 1]()
        consume(j, bufs[j % 2][...])


def _block_tri(n, chunk, lower):
    r = lax.broadcasted_iota(jnp.int32, (n, n), 0)
    c = lax.broadcasted_iota(jnp.int32, (n, n), 1)
    same = (r // chunk) == (c // chunk)
    tri = (c <= r) if lower else (r <= c)
    return jnp.where(same & tri, 1.0, 0.0).astype(BF16)


def _mod_kernel(c_ref, w_ref, b_ref, o_ref):
    o_ref[...] = _bdot(_silu(c_ref[...]), w_ref[...]) + b_ref[...]


def _modulation(c_all, w_mod, b_mod):
    nb, d = c_all.shape
    n6 = w_mod.shape[-1]
    tn = 1024
    return pl.pallas_call(
        _mod_kernel,
        grid=(DEPTH, n6 // tn),
        in_specs=[pl.BlockSpec((nb, d), lambda l, n: (0, 0)),
                  pl.BlockSpec((None, d, tn), lambda l, n: (l, 0, n)),
                  pl.BlockSpec((None, 1, tn), lambda l, n: (l, 0, n))],
        out_specs=pl.BlockSpec((None, nb, tn), lambda l, n: (l, 0, n)),
        out_shape=jax.ShapeDtypeStruct((DEPTH, nb, n6), F32),
        compiler_params=_params("parallel", "parallel"),
        name="modulation",
    )(c_all, w_mod, b_mod.reshape(DEPTH, 1, n6))


def _outproj_mlp_kernel(x_ref, y_ref, mod_ref, wo_ref, wu_ref, wd_ref, fg_ref, o_ref, *, final):
    x1 = x_ref[...] + mod_ref[2] * jnp.dot(y_ref[...], wo_ref[...], preferred_element_type=F32)
    u = _modnorm(x1, mod_ref[4], mod_ref[3]).astype(BF16)
    hidden = wu_ref.shape[1]
    acc = None
    for c in range(hidden // MLP_HCHUNK):
        cs = slice(c * MLP_HCHUNK, (c + 1) * MLP_HCHUNK)
        h = jnp.maximum(jnp.dot(u, wu_ref[:, cs], preferred_element_type=F32), 0.0)
        p = jnp.dot((h * h).astype(BF16), wd_ref[cs, :], preferred_element_type=F32)
        acc = p if acc is None else acc + p
    x2 = x1 + mod_ref[5] * acc
    if final:
        ms = jnp.mean(x2 * x2, axis=-1, keepdims=True)
        x2 = x2 * lax.rsqrt(ms + EPS) * fg_ref[...]
    o_ref[...] = x2


def _outproj_mlp(x, y, mod, weights, *, tm, rows_per_mod, final):
    n, d = x.shape
    k = y.shape[1]
    if rows_per_mod == 1:
        mod_spec = pl.BlockSpec((6, tm, d), lambda i: (0, i, 0))
    else:
        per = rows_per_mod // tm
        mod_spec = pl.BlockSpec((6, None, 1, d), lambda i: (0, i // per, 0, 0))
    w_arrs, w_specs = _unzip(weights)
    return pl.pallas_call(
        functools.partial(_outproj_mlp_kernel, final=final),
        grid=(n // tm,),
        in_specs=[pl.BlockSpec((tm, d), lambda i: (i, 0)), pl.BlockSpec((tm, k), lambda i: (i, 0)),
                  mod_spec] + w_specs,
        out_specs=pl.BlockSpec((tm, d), lambda i: (i, 0)),
        out_shape=jax.ShapeDtypeStruct((n, d), F32),
        compiler_params=_params("parallel"),
        name="outproj_mlp",
    )(x, y, mod, *w_arrs)


def _rg_gates(xbr, gwr_ref, gwi_ref, gb_ref, logsig, a_s, b_s):
    xb = xbr.astype(BF16)
    nblk = xbr.shape[1] // RG_BLOCK
    grs = [jnp.dot(xb[:, n * RG_BLOCK:(n + 1) * RG_BLOCK], gwr_ref[n], preferred_element_type=F32)
           for n in range(nblk)]
    gis = [jnp.dot(xb[:, n * RG_BLOCK:(n + 1) * RG_BLOCK], gwi_ref[n], preferred_element_type=F32)
           for n in range(nblk)]
    for n in range(nblk):
        sl = slice(n * RG_BLOCK, (n + 1) * RG_BLOCK)
        log_a = (RG_C * logsig[:, sl]) * _sigmoid(grs[n] + gb_ref[0:1, sl])
        a_s[:, sl] = jnp.exp(log_a)
        th = jnp.tanh(log_a)
        z = -2.0 * th / (1.0 - th)
        root = jnp.where(z > 0.0, z * lax.rsqrt(z), 0.0)
        b_s[:, sl] = root * _sigmoid(gis[n] + gb_ref[1:2, sl]) * xbr[:, sl]


def _rg_prompt_kernel(x_ref, mod_ref, wy_ref, wx_ref, cw_ref, cb_ref, gwr_ref, gwi_ref, gb_ref,
                      lam_ref, y_ref, tail_ref, h_ref, a_s, b_s, y_s):
    t, w = a_s.shape

    @pl.when(pl.program_id(1) == 0)
    def _():
        tail_ref[...] = jnp.zeros_like(tail_ref)
        h_ref[...] = jnp.zeros_like(h_ref)

    u = _modnorm(x_ref[...], mod_ref[1], mod_ref[0]).astype(BF16)
    xpre = jnp.dot(u, wx_ref[...], preferred_element_type=F32)
    ypre = jnp.dot(u, wy_ref[...], preferred_element_type=F32)
    xbr = _conv_seq(xpre, tail_ref[...], cw_ref, cb_ref[...])
    tail_ref[...] = xpre[t - SUBLANES:, :]
    y_s[...] = jax.nn.gelu(ypre)
    _rg_gates(xbr, gwr_ref, gwi_ref, gb_ref, jax.nn.log_sigmoid(lam_ref[...]), a_s, b_s)

    row = lax.broadcasted_iota(jnp.int32, (SUBLANES, w), 0)

    def group(gi, h):
        r0 = pl.multiple_of(gi * SUBLANES, SUBLANES)
        a = a_s[pl.ds(r0, SUBLANES), :]
        b = b_s[pl.ds(r0, SUBLANES), :]
        for s in (1, 2, 4):
            a_sh = jnp.where(row >= s, pltpu.roll(a, s, 0), 1.0)
            b_sh = jnp.where(row >= s, pltpu.roll(b, s, 0), 0.0)
            b = a * b_sh + b
            a = a * a_sh
        hg = a * h + b
        b_s[pl.ds(r0, SUBLANES), :] = hg
        return hg[SUBLANES - 1:SUBLANES, :]

    h_ref[...] = lax.fori_loop(0, t // SUBLANES, group, h_ref[...], unroll=2)
    y_ref[...] = (b_s[...] * y_s[...]).astype(BF16)


def _rg_weights(p):
    return [p["wy"], p["wx"], p["cw"], p["cb"], p["gwr"], p["gwi"], p["gb"], p["lam"]]


def _rg_prompt(x, mod, p):
    b, l, d = x.shape
    w = p["width"]
    t = ROW_TILE
    w_arrs, w_specs = _unzip(_rg_weights(p))
    return pl.pallas_call(
        _rg_prompt_kernel,
        grid=(b, l // t),
        in_specs=[pl.BlockSpec((None, t, d), lambda i, c: (i, c, 0)),
                  pl.BlockSpec((6, None, 1, d), lambda i, c: (0, i, 0, 0))] + w_specs,
        out_specs=[pl.BlockSpec((None, t, w), lambda i, c: (i, c, 0)),
                   pl.BlockSpec((None, SUBLANES, w), lambda i, c: (i, 0, 0)),
                   pl.BlockSpec((None, 1, w), lambda i, c: (i, 0, 0))],
        out_shape=[jax.ShapeDtypeStruct((b, l, w), BF16),
                   jax.ShapeDtypeStruct((b, SUBLANES, w), F32),
                   jax.ShapeDtypeStruct((b, 1, w), F32)],
        scratch_shapes=[pltpu.VMEM((t, w), F32), pltpu.VMEM((t, w), F32), pltpu.VMEM((t, w), F32)],
        compiler_params=_params("parallel", "arbitrary"),
        name="rglru_prompt",
    )(x, mod, *w_arrs)


def _rg_decode_kernel(x_ref, mod_ref, wy_ref, wx_ref, cw_ref, cb_ref, gwr_ref, gwi_ref, gb_ref,
                      lam_ref, cs_ref, h0_ref, y_ref, nb_ref, h_ref, a_s, b_s):
    u = _modnorm(x_ref[...], mod_ref[1], mod_ref[0]).astype(BF16)
    xpre = jnp.dot(u, wx_ref[...], preferred_element_type=F32)
    xbr = (cs_ref[0] * cw_ref[0:1, :] + cs_ref[1] * cw_ref[1:2, :] + cs_ref[2] * cw_ref[2:3, :]
           + xpre * cw_ref[3:4, :] + cb_ref[...])
    nb_ref[0] = cs_ref[1]
    nb_ref[1] = cs_ref[2]
    nb_ref[2] = xpre
    _rg_gates(xbr, gwr_ref, gwi_ref, gb_ref, jax.nn.log_sigmoid(lam_ref[...]), a_s, b_s)
    h = a_s[...] * h0_ref[...] + b_s[...]
    h_ref[...] = h
    y_br = jax.nn.gelu(jnp.dot(u, wy_ref[...], preferred_element_type=F32))
    y_ref[...] = (h * y_br).astype(BF16)


def _full(shape):
    return pl.BlockSpec(shape, lambda *_: (0,) * len(shape))


def _rg_decode(x, mod, p, conv_state, h0, j):
    n, d = x.shape
    w = p["width"]
    w_arrs, w_specs = _unzip(_rg_weights(p))
    return pl.pallas_call(
        _rg_decode_kernel,
        grid=(1,),
        in_specs=[_full((n, d)), _full((6, n, d))] + w_specs
        + [_full((3, n, w)), pl.BlockSpec((None, n, w), lambda i: (j, 0, 0))],
        out_specs=[_full((n, w)), _full((3, n, w)), _full((n, w))],
        out_shape=[jax.ShapeDtypeStruct((n, w), BF16),
                   jax.ShapeDtypeStruct((3, n, w), F32),
                   jax.ShapeDtypeStruct((n, w), F32)],
        scratch_shapes=[pltpu.VMEM((n, w), F32), pltpu.VMEM((n, w), F32)],
        compiler_params=_params("arbitrary"),
        name="rglru_decode",
    )(x, mod, *w_arrs, conv_state, h0)


def _gdn_qkvg(u, wqkv_ref, wg_ref, conv_fn, q_ref, k_ref, v_ref, gz_ref, bufs):
    hk = q_ref.shape[-1]
    nh = hk // GDN_DK
    dsts = ((q_ref, GDN_DK ** -0.5), (k_ref, 1.0), (v_ref, None))

    def consume(j, pre):
        if j == len(dsts):
            gz_ref[...] = _silu(pre)
            return
        dst, scale = dsts[j]
        act = _silu(conv_fn(pre, j))
        if scale is None:
            dst[...] = act
        else:
            for h in range(nh):
                hs = slice(h * GDN_DK, (h + 1) * GDN_DK)
                xh = act[:, hs]
                ss = jnp.sum(xh * xh, axis=-1, keepdims=True)
                dst[:, hs] = xh * (lax.rsqrt(ss + EPS) * scale)

    dots = [functools.partial(lambda j: jnp.dot(u, wqkv_ref[:, j * hk:(j + 1) * hk],
                                                preferred_element_type=F32), j) for j in range(len(dsts))]
    dots.append(lambda: jnp.dot(u, wg_ref[...], preferred_element_type=F32))
    _pipelined(dots, consume, bufs)


def _gdn_in_kernel(x_ref, mod_ref, wqkv_ref, wg_ref, wab_ref, wabt_ref, cw_ref, alr_ref, dtr_ref,
                   alc_ref, dtc_ref, q_ref, k_ref, v_ref, gz_ref, gb_ref, gr_ref, tail_ref, p0_s, p1_s):
    t = x_ref.shape[0]
    hk = q_ref.shape[-1]
    nh = hk // GDN_DK

    @pl.when(pl.program_id(1) == 0)
    def _():
        tail_ref[...] = jnp.zeros_like(tail_ref)

    u = _modnorm(x_ref[...], mod_ref[1], mod_ref[0]).astype(BF16)

    def conv_fn(pre, j):
        cs = slice(j * hk, (j + 1) * hk)
        out = _conv_seq(pre, tail_ref[:, cs], cw_ref.at[:, cs], None)
        tail_ref[:, cs] = pre[t - SUBLANES:, :]
        return out

    _gdn_qkvg(u, wqkv_ref, wg_ref, conv_fn, q_ref, k_ref, v_ref, gz_ref, (p0_s, p1_s))

    ab = jnp.dot(u, wab_ref[...], preferred_element_type=F32)
    abt = lax.dot_general(wabt_ref[...], u, (((1,), (1,)), ((), ())),
                          preferred_element_type=F32)
    g_col = -jnp.exp(alr_ref[...]) * jax.nn.softplus(ab + dtr_ref[...])
    g_row = -jnp.exp(alc_ref[...]) * jax.nn.softplus(abt + dtc_ref[...])
    gc = _sdot_r(_block_tri(t, GDN_CHUNK, True), g_col)
    gr = _sdot_l(g_row, _block_tri(t, GDN_CHUNK, False))
    lane = lax.broadcasted_iota(jnp.int32, ab.shape, 1)
    gb_ref[...] = jnp.where(lane < nh, gc, _sigmoid(ab))
    gr_ref[...] = gr[:SUBLANES, :]


def _gdn_in(x, mod, p):
    b, l, d = x.shape
    hk = p["hk"]
    t = ROW_TILE
    tok = lambda width: pl.BlockSpec((None, t, width), lambda i, c: (i, c, 0))
    w_arrs, w_specs = _unzip([p["wqkv"], p["wg"], p["wab"], p["wabt"], p["cw"], p["alr"], p["dtr"],
                              p["alc"], p["dtc"]])
    return pl.pallas_call(
        _gdn_in_kernel,
        grid=(b, l // t),
        in_specs=[tok(d), pl.BlockSpec((6, None, 1, d), lambda i, c: (0, i, 0, 0))] + w_specs,
        out_specs=[tok(hk), tok(hk), tok(hk), tok(hk), tok(LANES),
                   pl.BlockSpec((None, SUBLANES, t), lambda i, c: (i, 0, c)),
                   pl.BlockSpec((None, SUBLANES, 3 * hk), lambda i, c: (i, 0, 0))],
        out_shape=[jax.ShapeDtypeStruct((b, l, hk), F32)] * 4
        + [jax.ShapeDtypeStruct((b, l, LANES), F32),
           jax.ShapeDtypeStruct((b, SUBLANES, l), F32),
           jax.ShapeDtypeStruct((b, SUBLANES, 3 * hk), F32)],
        scratch_shapes=[pltpu.VMEM((t, hk), F32), pltpu.VMEM((t, hk), F32)],
        compiler_params=_params("parallel", "arbitrary"),
        name="gdn_in",
    )(x, mod, *w_arrs)


def _gdn_scan_kernel(q_ref, k_ref, v_ref, gz_ref, gb_ref, gr_ref, ng_ref, y_ref, s_ref):
    c = GDN_CHUNK
    nseq, nh = s_ref.shape[0], s_ref.shape[1]

    @pl.when(pl.program_id(1) == 0)
    def _():
        s_ref[...] = jnp.zeros_like(s_ref)

    r = lax.broadcasted_iota(jnp.int32, (c, c), 0)
    cc = lax.broadcasted_iota(jnp.int32, (c, c), 1)
    causal = r >= cc
    strict = r > cc
    eye = jnp.where(r == cc, 1.0, 0.0)
    merge = []
    blk = 1
    while blk < c:
        merge.append((r // (2 * blk) == cc // (2 * blk)) & ((r // blk) % 2 == 1) & ((cc // blk) % 2 == 0))
        blk *= 2

    nchunk = q_ref.shape[1] // c
    trip = [(b, j, h) for b in range(nseq) for j in range(nchunk) for h in range(nh)]
    rows = lambda j: slice(j * c, (j + 1) * c)
    cols = lambda h: slice(h * GDN_DK, (h + 1) * GDN_DK)
    g_col = lambda b, j, h: gb_ref[b, rows(j), h:h + 1]
    beta_of = lambda b, j, h: gb_ref[b, rows(j), nh + h:nh + h + 1]
    g_row = lambda b, j, h: gr_ref[b, h:h + 1, rows(j)]
    blk_of = lambda ref, b, j, h: ref[b, rows(j), cols(h)]

    kks = [_bdot_nt(jnp.concatenate([blk_of(k_ref, *t) * beta_of(*t), blk_of(q_ref, *t)], axis=0),
                    blk_of(k_ref, *t)) for t in trip]
    a_s, qk_s = [], []
    for t, kk in zip(trip, kks):
        decay = jnp.where(causal, jnp.exp(jnp.where(causal, g_col(*t) - g_row(*t), 0.0)), 0.0)
        a_s.append(jnp.where(strict, kk[:c] * decay, 0.0))
        qk_s.append((kk[c:] * decay).astype(BF16))
    xs = [eye - jnp.where(merge[0], a, 0.0) for a in a_s]
    for m in merge[1:]:
        ts = [_bdot(jnp.where(m, a, 0.0), x) for a, x in zip(a_s, xs)]
        xs = [x - _bdot(x, t_) for x, t_ in zip(xs, ts)]
    uws = {}
    for t, x in zip(trip, xs):
        kb = blk_of(k_ref, *t) * beta_of(*t)
        rhs = jnp.concatenate([blk_of(v_ref, *t) * beta_of(*t), kb * jnp.exp(g_col(*t))], axis=1)
        uws[t] = _bdot(x, rhs)
    qks = dict(zip(trip, qk_s))

    for j in range(nchunk):
        cur = [(b, j, h) for b in range(nseq) for h in range(nh)]
        states = {t: s_ref[t[0], t[2]] for t in cur}
        ws_qs = {t: _bdot(jnp.concatenate([uws[t][:, GDN_DV:], blk_of(q_ref, *t) * jnp.exp(g_col(*t))],
                                          axis=0), states[t]) for t in cur}
        v_news = {t: uws[t][:, :GDN_DV] - ws_qs[t][:c] for t in cur}
        outs = {t: ws_qs[t][c:] + jnp.dot(qks[t], v_news[t].astype(BF16), preferred_element_type=F32)
                for t in cur}
        for t in cur:
            g_last = g_row(*t)[:, c - 1:c]
            k_dec = blk_of(k_ref, *t) * jnp.exp(g_last - g_col(*t))
            s_ref[t[0], t[2]] = jnp.exp(g_last) * states[t] + lax.dot_general(
                k_dec.astype(BF16), v_news[t].astype(BF16), (((0,), (0,)), ((), ())),
                preferred_element_type=F32)
        for t in cur:
            o = outs[t]
            ms = jnp.mean(o * o, axis=-1, keepdims=True)
            y_ref[t[0], rows(j), cols(t[2])] = (o * lax.rsqrt(ms + EPS) * ng_ref[...]
                                                * blk_of(gz_ref, *t)).astype(BF16)


def _gdn_scan(q, k, v, gz, gb, gr, norm_g):
    b, l, hk = q.shape
    nh = hk // GDN_DK
    t = GDN_STEP
    ns = GDN_SEQS
    tok = lambda width: pl.BlockSpec((ns, t, width), lambda i, c: (i, c, 0))
    ng_arr, ng_spec = norm_g
    return pl.pallas_call(
        _gdn_scan_kernel,
        grid=(b // ns, l // t),
        in_specs=[tok(hk), tok(hk), tok(hk), tok(hk), tok(LANES),
                  pl.BlockSpec((ns, SUBLANES, t), lambda i, c: (i, 0, c)), ng_spec],
        out_specs=[tok(hk), pl.BlockSpec((ns, nh, GDN_DK, GDN_DV), lambda i, c: (i, 0, 0, 0))],
        out_shape=[jax.ShapeDtypeStruct((b, l, hk), BF16),
                   jax.ShapeDtypeStruct((b, nh, GDN_DK, GDN_DV), F32)],
        compiler_params=_params("parallel", "arbitrary"),
        name="gdn_scan",
    )(q, k, v, gz, gb, gr, ng_arr)


def _gdn_dec_in_kernel(x_ref, mod_ref, wqkv_ref, wg_ref, wab_ref, cw_ref, alr_ref, dtr_ref, cs_ref,
                       qt_ref, kt_ref, v_ref, gz_ref, eg_ref, be_ref, nb_ref, q_s, k_s, p0_s, p1_s):
    hk = v_ref.shape[-1]
    nh = hk // GDN_DK
    u = _modnorm(x_ref[...], mod_ref[1], mod_ref[0]).astype(BF16)

    def conv_fn(pre, j):
        cs = slice(j * hk, (j + 1) * hk)
        nb_ref[0, :, cs] = cs_ref[1, :, cs]
        nb_ref[1, :, cs] = cs_ref[2, :, cs]
        nb_ref[2, :, cs] = pre
        return (cs_ref[0, :, cs] * cw_ref[0:1, cs] + cs_ref[1, :, cs] * cw_ref[1:2, cs]
                + cs_ref[2, :, cs] * cw_ref[2:3, cs] + pre * cw_ref[3:4, cs])

    _gdn_qkvg(u, wqkv_ref, wg_ref, conv_fn, q_s, k_s, v_ref, gz_ref, (p0_s, p1_s))
    ab = jnp.dot(u, wab_ref[...], preferred_element_type=F32)
    e_g = jnp.exp(-jnp.exp(alr_ref[...]) * jax.nn.softplus(ab + dtr_ref[...]))
    beta = _sigmoid(ab)
    n = ab.shape[0]
    for h in range(nh):
        hs = slice(h * GDN_DK, (h + 1) * GDN_DK)
        qt_ref[h] = q_s[:, hs].T
        kt_ref[h] = k_s[:, hs].T
        eg_ref[h] = jnp.broadcast_to(e_g[:, h:h + 1], (n, LANES))
        be_ref[h] = jnp.broadcast_to(beta[:, nh + h:nh + h + 1], (n, LANES))


def _gdn_dec_in(x, mod, p, conv_state):
    n, d = x.shape
    hk = p["hk"]
    nh = hk // GDN_DK
    w_arrs, w_specs = _unzip([p["wqkv"], p["wg"], p["wab"], p["cw"], p["alr"], p["dtr"]])
    return pl.pallas_call(
        _gdn_dec_in_kernel,
        grid=(1,),
        in_specs=[_full((n, d)), _full((6, n, d))] + w_specs + [_full((3, n, 3 * hk))],
        out_specs=[_full((nh, GDN_DK, n)), _full((nh, GDN_DK, n)), _full((n, hk)), _full((n, hk)),
                   _full((nh, n, LANES)), _full((nh, n, LANES)), _full((3, n, 3 * hk))],
        out_shape=[jax.ShapeDtypeStruct((nh, GDN_DK, n), F32)] * 2
        + [jax.ShapeDtypeStruct((n, hk), F32)] * 2
        + [jax.ShapeDtypeStruct((nh, n, LANES), F32)] * 2
        + [jax.ShapeDtypeStruct((3, n, 3 * hk), F32)],
        scratch_shapes=[pltpu.VMEM((n, hk), F32)] * 4,
        compiler_params=_params("arbitrary"),
        name="gdn_decode_in",
    )(x, mod, *w_arrs, conv_state)


def _gdn_dec_state_kernel(qt_ref, kt_ref, v_ref, gz_ref, eg_ref, be_ref, ng_ref, s0_ref,
                          y_ref, s_ref, o_s):
    n = s0_ref.shape[0]
    for b in range(n):
        kc = kt_ref[:, b:b + 1]
        qc = qt_ref[:, b:b + 1]
        s0 = s0_ref[b]
        k_s0 = jnp.sum(kc * s0, axis=0, keepdims=True)
        q_s0 = jnp.sum(qc * s0, axis=0, keepdims=True)
        qk = jnp.sum(qc * kc, axis=0, keepdims=True)
        e_g = eg_ref[b:b + 1, :]
        v_new = be_ref[b:b + 1, :] * (v_ref[b:b + 1, :] - e_g * k_s0)
        o_s[b:b + 1, :] = e_g * q_s0 + qk * v_new
        s_ref[b] = e_g * s0 + kc * v_new
    o = o_s[...]
    ms = jnp.mean(o * o, axis=-1, keepdims=True)
    y_ref[...] = (o * lax.rsqrt(ms + EPS) * ng_ref[...] * gz_ref[...]).astype(BF16)


def _gdn_dec_state(qt, kt, v, gz, eg, be, norm_g, s0, j):
    _, n, nh, dk, dv = s0.shape
    head = lambda a, bb: pl.BlockSpec((None, a, bb), lambda h: (h, 0, 0))
    col = pl.BlockSpec((n, dv), lambda h: (0, h))
    ng_arr, ng_spec = norm_g
    return pl.pallas_call(
        _gdn_dec_state_kernel,
        grid=(nh,),
        in_specs=[head(dk, n), head(dk, n), col, col, head(n, LANES), head(n, LANES), ng_spec,
                  pl.BlockSpec((None, n, None, dk, dv), lambda h: (j, 0, h, 0, 0))],
        out_specs=[col, pl.BlockSpec((n, None, dk, dv), lambda h: (0, h, 0, 0))],
        out_shape=[jax.ShapeDtypeStruct((n, nh * dv), BF16), jax.ShapeDtypeStruct((n, nh, dk, dv), F32)],
        scratch_shapes=[pltpu.VMEM((n, dv), F32)],
        compiler_params=_params("parallel"),
        name="gdn_decode_state",
    )(qt, kt, v, gz, eg, be, ng_arr, s0)


def _ssd_proj(u, wz_ref, wx_refs, conv_fn, zs_ref, xs_ref, bm_ref, cm_ref, bufs):
    gn = bm_ref.shape[-1]
    nz = zs_ref.shape[-1] // WBLK

    def consume(j, pre):
        if j < nz:
            zs_ref[:, j * WBLK:(j + 1) * WBLK] = _silu(pre)
            return
        jx = j - nz
        cs = slice(jx * WBLK, (jx + 1) * WBLK)
        act = _silu(conv_fn(pre, cs))
        if jx + 1 < len(wx_refs):
            xs_ref[:, cs] = act
        else:
            bm_ref[...] = act[:, :gn]
            cm_ref[...] = act[:, gn:]

    dots = [functools.partial(lambda j: jnp.dot(u, wz_ref[:, j * WBLK:(j + 1) * WBLK],
                                                preferred_element_type=F32), j) for j in range(nz)]
    dots += [functools.partial(lambda r: jnp.dot(u, r[...], preferred_element_type=F32), r) for r in wx_refs]
    _pipelined(dots, consume, bufs)


def _ssd_in_kernel(x_ref, mod_ref, wz_ref, wx0_ref, wx1_ref, wx2_ref, wdt_ref, wdtt_ref, cw_ref, cb_ref,
                   dtr_ref, alr_ref, dtc_ref, alc_ref,
                   zs_ref, xs_ref, bm_ref, cm_ref, da_ref, at_ref, dtt_ref, tail_ref, p0_s, p1_s):
    t = x_ref.shape[0]
    nh = at_ref.shape[0]

    @pl.when(pl.program_id(1) == 0)
    def _():
        tail_ref[...] = jnp.zeros_like(tail_ref)

    u = _modnorm(x_ref[...], mod_ref[1], mod_ref[0]).astype(BF16)

    def conv_fn(pre, cs):
        out = _conv_seq(pre, tail_ref[:, cs], cw_ref.at[:, cs], cb_ref[:, cs])
        tail_ref[:, cs] = pre[t - SUBLANES:, :]
        return out

    _ssd_proj(u, wz_ref, (wx0_ref, wx1_ref, wx2_ref), conv_fn, zs_ref, xs_ref, bm_ref, cm_ref, (p0_s, p1_s))

    dt_c = jax.nn.softplus(jnp.dot(u, wdt_ref[...], preferred_element_type=F32) + dtr_ref[...])
    dt_r = jax.nn.softplus(lax.dot_general(wdtt_ref[...], u, (((1,), (1,)), ((), ())),
                                           preferred_element_type=F32) + dtc_ref[...])
    acs_c = _sdot_r(_block_tri(t, SSD_CHUNK, True), dt_c * -jnp.exp(alr_ref[...]))
    acs_r = _sdot_l(dt_r * -jnp.exp(alc_ref[...]), _block_tri(t, SSD_CHUNK, False))
    lane = lax.broadcasted_iota(jnp.int32, dt_c.shape, 1)
    da_ref[...] = jnp.where(lane < nh, dt_c, pltpu.roll(acs_c, nh, 1))
    at_ref[...] = acs_r[:nh, :]
    dtt_ref[...] = dt_r[:nh, :]


def _ssd_in(x, mod, p):
    b, l, d = x.shape
    di, cd, nh = p["di"], p["cd"], p["nh"]
    gn = (cd - di) // 2
    t = ROW_TILE
    tok = lambda width: pl.BlockSpec((None, t, width), lambda i, c: (i, c, 0))
    w_arrs, w_specs = _unzip([p["wz"], *p["wx"], p["wdt"], p["wdtt"], p["cw"], p["cb"], p["dtr"],
                              p["alr"], p["dtc"], p["alc"]])
    rows_spec = pl.BlockSpec((None, nh, t), lambda i, c: (i, 0, c))
    return pl.pallas_call(
        _ssd_in_kernel,
        grid=(b, l // t),
        in_specs=[tok(d), pl.BlockSpec((6, None, 1, d), lambda i, c: (0, i, 0, 0))] + w_specs,
        out_specs=[tok(di), tok(di), tok(gn), tok(gn), tok(LANES), rows_spec, rows_spec,
                   pl.BlockSpec((None, SUBLANES, cd), lambda i, c: (i, 0, 0))],
        out_shape=[jax.ShapeDtypeStruct((b, l, di), F32), jax.ShapeDtypeStruct((b, l, di), F32),
                   jax.ShapeDtypeStruct((b, l, gn), F32), jax.ShapeDtypeStruct((b, l, gn), F32),
                   jax.ShapeDtypeStruct((b, l, LANES), F32),
                   jax.ShapeDtypeStruct((b, nh, l), F32), jax.ShapeDtypeStruct((b, nh, l), F32),
                   jax.ShapeDtypeStruct((b, SUBLANES, cd), F32)],
        scratch_shapes=[pltpu.VMEM((t, WBLK), F32), pltpu.VMEM((t, WBLK), F32)],
        compiler_params=_params("parallel", "arbitrary"),
        name="ssd_in",
    )(x, mod, *w_arrs)


def _ssd_scan_kernel(zs_ref, xs_ref, bm_ref, cm_ref, da_ref, at_ref, dtt_ref, e2_ref, dsk_ref, ng_ref,
                     y_ref, hout_ref, ht_s):
    c = xs_ref.shape[0]
    di = xs_ref.shape[1]
    n = SSM_STATE
    pdim = SSM_HEADDIM
    gw = di // SSM_GROUPS
    hpg = gw // pdim
    nh = di // pdim
    ci = pl.program_id(1)

    @pl.when(ci == 0)
    def _():
        ht_s[...] = jnp.zeros_like(ht_s)

    da = da_ref[...]
    lane_a = lax.broadcasted_iota(jnp.int32, da.shape, 1)
    is_acs = (lane_a >= nh) & (lane_a < 2 * nh)
    acs = jnp.where(is_acs, da, 0.0)
    dt_on_acs = jnp.where(is_acs, pltpu.roll(da, nh, 1), 0.0)
    eacs_e = _sdot_l(jnp.where(is_acs, jnp.exp(acs), 0.0), e2_ref[...])
    dsdt_e = _sdot_l(jnp.exp(acs[c - 1:c, :] - acs) * dt_on_acs, e2_ref[...])
    xs = xs_ref[...]
    xd = xs * dsdt_e

    r = lax.broadcasted_iota(jnp.int32, (c, c), 0)
    cc = lax.broadcasted_iota(jnp.int32, (c, c), 1)
    causal = r >= cc
    lane = lax.broadcasted_iota(jnp.int32, (c, 2 * pdim), 1)

    groups = range(SSM_GROUPS)
    gsl = lambda g: slice(g * gw, (g + 1) * gw)
    nsl = lambda g: slice(g * n, (g + 1) * n)
    cbs = [_bdot_nt(cm_ref[:, nsl(g)], bm_ref[:, nsl(g)]) for g in groups]
    hts = [ht_s[g] for g in groups]
    y_offs = [_bdot(cm_ref[:, nsl(g)], hts[g]) * eacs_e[:, gsl(g)] for g in groups]
    for g in groups:
        ht_s[g] = eacs_e[c - 1:c, gsl(g)] * hts[g] + _bdot(bm_ref[:, nsl(g)].T, xd[:, gsl(g)])
    for g in groups:
        y_parts = []
        for pr in range(hpg // 2):
            ms = []
            for hh in (2 * pr, 2 * pr + 1):
                h = g * hpg + hh
                seg = da[:, nh + h:nh + h + 1] - at_ref[h:h + 1, :]
                lm = jnp.where(causal, jnp.exp(jnp.where(causal, seg, 0.0)), 0.0)
                ms.append((cbs[g] * lm * dtt_ref[h:h + 1, :]).astype(BF16))
            ps = slice(g * gw + pr * 2 * pdim, g * gw + (pr + 1) * 2 * pdim)
            xp = xs[:, ps].astype(BF16)
            zero = jnp.zeros_like(xp)
            rhs = jnp.concatenate([jnp.where(lane < pdim, xp, zero), jnp.where(lane >= pdim, xp, zero)], axis=0)
            y_d = jnp.dot(jnp.concatenate(ms, axis=1), rhs, preferred_element_type=F32)
            os_ = slice(pr * 2 * pdim, (pr + 1) * 2 * pdim)
            y_parts.append(y_d + y_offs[g][:, os_] + dsk_ref[:, ps] * xs[:, ps])
        yz = jnp.concatenate(y_parts, axis=1) * zs_ref[:, gsl(g)]
        ms_ = jnp.mean(yz * yz, axis=-1, keepdims=True)
        y_ref[:, gsl(g)] = (yz * lax.rsqrt(ms_ + EPS) * ng_ref[:, gsl(g)]).astype(BF16)

    @pl.when(ci == pl.num_programs(1) - 1)
    def _():
        for g in groups:
            hout_ref[gsl(g), :] = ht_s[g].T


def _ssd_scan(zs, xs, bm, cm, da, at, dtt, p):
    b, l, di = xs.shape
    gn = bm.shape[-1]
    nh = p["nh"]
    gw = di // SSM_GROUPS
    t = SSD_CHUNK
    tok = lambda width: pl.BlockSpec((None, t, width), lambda i, c: (i, c, 0))
    rows_spec = pl.BlockSpec((None, nh, t), lambda i, c: (i, 0, c))
    w_arrs, w_specs = _unzip([p["e2"], p["dsk"], p["ng"]])
    return pl.pallas_call(
        _ssd_scan_kernel,
        grid=(b, l // t),
        in_specs=[tok(di), tok(di), tok(gn), tok(gn), tok(LANES), rows_spec, rows_spec] + w_specs,
        out_specs=[tok(di), pl.BlockSpec((None, di, SSM_STATE), lambda i, c: (i, 0, 0))],
        out_shape=[jax.ShapeDtypeStruct((b, l, di), BF16),
                   jax.ShapeDtypeStruct((b, di, SSM_STATE), F32)],
        scratch_shapes=[pltpu.VMEM((SSM_GROUPS, SSM_STATE, gw), F32)],
        compiler_params=_params("parallel", "arbitrary"),
        name="ssd_scan",
    )(zs, xs, bm, cm, da, at, dtt, *w_arrs)


def _ssd_dec_in_kernel(x_ref, mod_ref, wz_ref, wx0_ref, wx1_ref, wx2_ref, wdt_ref, cw_ref, cb_ref,
                       dtr_ref, alr_ref, e1_ref, cs_ref,
                       zs_ref, xs_ref, bm_ref, cm_ref, xt_ref, ea_ref, xdt_ref, eae_ref, nb_ref, p0_s, p1_s):
    n = x_ref.shape[0]
    di = zs_ref.shape[-1]
    nh = di // SSM_HEADDIM
    u = _modnorm(x_ref[...], mod_ref[1], mod_ref[0]).astype(BF16)

    def conv_fn(pre, cs):
        nb_ref[0, :, cs] = cs_ref[1, :, cs]
        nb_ref[1, :, cs] = cs_ref[2, :, cs]
        nb_ref[2, :, cs] = pre
        return (cs_ref[0, :, cs] * cw_ref[0:1, cs] + cs_ref[1, :, cs] * cw_ref[1:2, cs]
                + cs_ref[2, :, cs] * cw_ref[2:3, cs] + pre * cw_ref[3:4, cs] + cb_ref[:, cs])

    _ssd_proj(u, wz_ref, (wx0_ref, wx1_ref, wx2_ref), conv_fn, zs_ref, xs_ref, bm_ref, cm_ref, (p0_s, p1_s))
    dt = jax.nn.softplus(jnp.dot(u, wdt_ref[...], preferred_element_type=F32) + dtr_ref[...])
    e_a = jnp.exp(dt * -jnp.exp(alr_ref[...]))
    xdt_ref[...] = xs_ref[...] * _sdot_l(dt, e1_ref[...])
    eae_ref[...] = _sdot_l(e_a, e1_ref[...])
    for j in range(di // LANES):
        xt_ref[j * LANES:(j + 1) * LANES, :] = xdt_ref[:, j * LANES:(j + 1) * LANES].T
    for h in range(nh):
        ea_ref[h] = jnp.broadcast_to(e_a[:, h:h + 1], (n, LANES))


def _ssd_dec_in(x, mod, p, conv_state):
    n, d = x.shape
    di, cd, nh = p["di"], p["cd"], p["nh"]
    gn = (cd - di) // 2
    w_arrs, w_specs = _unzip([p["wz"], *p["wx"], p["wdt"], p["cw"], p["cb"], p["dtr"], p["alr"], p["e1"]])
    return pl.pallas_call(
        _ssd_dec_in_kernel,
        grid=(1,),
        in_specs=[_full((n, d)), _full((6, n, d))] + w_specs + [_full((3, n, cd))],
        out_specs=[_full((n, di)), _full((n, di)), _full((n, gn)), _full((n, gn)), _full((di, n)),
                   _full((nh, n, LANES)), _full((n, di)), _full((n, di)), _full((3, n, cd))],
        out_shape=[jax.ShapeDtypeStruct((n, di), F32), jax.ShapeDtypeStruct((n, di), F32),
                   jax.ShapeDtypeStruct((n, gn), F32), jax.ShapeDtypeStruct((n, gn), F32),
                   jax.ShapeDtypeStruct((di, n), F32),
                   jax.ShapeDtypeStruct((nh, n, LANES), F32),
                   jax.ShapeDtypeStruct((n, di), F32), jax.ShapeDtypeStruct((n, di), F32),
                   jax.ShapeDtypeStruct((3, n, cd), F32)],
        scratch_shapes=[pltpu.VMEM((n, WBLK), F32), pltpu.VMEM((n, WBLK), F32)],
        compiler_params=_params("arbitrary"),
        name="ssd_decode_in",
    )(x, mod, *w_arrs, conv_state)


def _ssd_dec_state_kernel(xt_ref, bm_ref, cm_ref, ea_ref, h0_ref, yo_ref, h_ref):
    n = h0_ref.shape[0]
    pdim = h0_ref.shape[2]
    for b in range(n):
        bb = bm_ref[b:b + 1, :]
        c8 = jnp.broadcast_to(cm_ref[b:b + 1, :], (SUBLANES, bb.shape[1]))
        outs = []
        for hh in range(2):
            h0 = h0_ref[b, hh]
            outs.append(_bdot_nt(c8, h0)[0:1, :])
            xc = xt_ref[hh * pdim:(hh + 1) * pdim, b:b + 1]
            h_ref[b, hh] = ea_ref[hh, b:b + 1, :] * h0 + xc * bb
        yo_ref[b:b + 1, :] = jnp.concatenate(outs, axis=1)


def _ssd_dec_state(xt, bm, cm, ea, h0, j):
    _, n, nh, pdim, ns = h0.shape
    hpg = nh // SSM_GROUPS
    return pl.pallas_call(
        _ssd_dec_state_kernel,
        grid=(nh // 2,),
        in_specs=[pl.BlockSpec((2 * pdim, n), lambda i: (i, 0)),
                  pl.BlockSpec((n, ns), lambda i: (0, (2 * i) // hpg)),
                  pl.BlockSpec((n, ns), lambda i: (0, (2 * i) // hpg)),
                  pl.BlockSpec((2, n, LANES), lambda i: (i, 0, 0)),
                  pl.BlockSpec((None, n, 2, pdim, ns), lambda i: (j, 0, i, 0, 0))],
        out_specs=[pl.BlockSpec((n, 2 * pdim), lambda i: (0, i)),
                   pl.BlockSpec((n, 2, pdim, ns), lambda i: (0, i, 0, 0))],
        out_shape=[jax.ShapeDtypeStruct((n, nh * pdim), F32), jax.ShapeDtypeStruct((n, nh, pdim, ns), F32)],
        compiler_params=_params("parallel"),
        name="ssd_decode_state",
    )(xt, bm, cm, ea, h0)


def _ssd_dec_out_kernel(yo_ref, xs_ref, bm_ref, cm_ref, zs_ref, eae_ref, xdt_ref, dsk_ref, ng_ref, y_ref):
    di = xs_ref.shape[1]
    gw = di // SSM_GROUPS
    n = SSM_STATE
    for g in range(SSM_GROUPS):
        gs = slice(g * gw, (g + 1) * gw)
        cb = jnp.sum(cm_ref[:, g * n:(g + 1) * n] * bm_ref[:, g * n:(g + 1) * n], axis=-1, keepdims=True)
        y = cb * xdt_ref[:, gs] + eae_ref[:, gs] * yo_ref[:, gs] + dsk_ref[:, gs] * xs_ref[:, gs]
        yz = y * zs_ref[:, gs]
        ms = jnp.mean(yz * yz, axis=-1, keepdims=True)
        y_ref[:, gs] = (yz * lax.rsqrt(ms + EPS) * ng_ref[:, gs]).astype(BF16)


def _ssd_dec_out(yo, xs, bm, cm, zs, eae, xdt, p):
    n, di = xs.shape
    gn = bm.shape[1]
    w_arrs, w_specs = _unzip([p["dsk"], p["ng"]])
    return pl.pallas_call(
        _ssd_dec_out_kernel,
        grid=(1,),
        in_specs=[_full((n, di)), _full((n, di)), _full((n, gn)), _full((n, gn)), _full((n, di)),
                  _full((n, di)), _full((n, di))] + w_specs,
        out_specs=_full((n, di)),
        out_shape=jax.ShapeDtypeStruct((n, di), BF16),
        compiler_params=_params("arbitrary"),
        name="ssd_decode_out",
    )(yo, xs, bm, cm, zs, eae, xdt, *w_arrs)


def _pad_lanes(v):
    return jnp.pad(v.astype(F32), (0, LANES - v.shape[0])).reshape(1, LANES)


def _pad_cols(w):
    return jnp.pad(w, ((0, 0), (0, LANES - w.shape[1])))


def _small(a):
    return _w(a, a.shape, (0,) * a.ndim)


def _rg_params(j, w_in, conv_w, conv_b, gate_w, gate_b, lam, w_out):
    nl, d, w2 = w_in.shape
    w = w2 // 2
    nblk = w // RG_BLOCK
    gate = (None, None, nblk, RG_BLOCK, RG_BLOCK)
    return dict(width=w,
                wy=_w(w_in, (None, d, w), (j, 0, 0)), wx=_w(w_in, (None, d, w), (j, 0, 1)),
                cw=_w(conv_w, (None, 4, w), (j, 0, 0)), cb=_w(conv_b.reshape(nl, 1, w), (None, 1, w), (j, 0, 0)),
                gwr=_w(gate_w, gate, (j, 0, 0, 0, 0)), gwi=_w(gate_w, gate, (j, 1, 0, 0, 0)),
                gb=_w(gate_b, (None, 2, w), (j, 0, 0)), lam=_w(lam.reshape(nl, 1, w), (None, 1, w), (j, 0, 0)),
                wo=_w(w_out, (None, w, d), (j, 0, 0)))


def _gdn_params(j, w_in, conv_w, a_log, dt_bias, norm_g, w_out):
    nl, d, _ = w_in.shape
    qkv = conv_w.shape[-1]
    hk = qkv // 3
    wab = _pad_cols(w_in[j, :, qkv + hk:])
    alr, dtr = _pad_lanes(a_log[j]), _pad_lanes(dt_bias[j])
    return dict(hk=hk,
                wqkv=_w(w_in, (None, d, qkv), (j, 0, 0)), wg=_w(w_in, (None, d, hk), (j, 0, qkv // hk)),
                wab=_small(wab), wabt=_small(wab.T), cw=_w(conv_w, (None, 4, qkv), (j, 0, 0)),
                alr=_small(alr), dtr=_small(dtr), alc=_small(alr.T), dtc=_small(dtr.T),
                ng=_w(norm_g.reshape(nl, 1, GDN_DV), (None, 1, GDN_DV), (j, 0, 0)),
                wo=_w(w_out, (None, hk, d), (j, 0, 0)))


def _ssd_params(j, w_in, conv_w, conv_b, a_log, dt_bias, d_skip, norm_g, w_out):
    nl, d, _ = w_in.shape
    cd = conv_w.shape[-1]
    nh = a_log.shape[-1]
    di = nh * SSM_HEADDIM
    assert di % WBLK == 0 and (cd - di) == WBLK
    wdt = _pad_cols(w_in[j, :, di + cd:])
    alr, dtr = _pad_lanes(a_log[j]), _pad_lanes(dt_bias[j])
    head_of = jnp.arange(di, dtype=jnp.int32) // SSM_HEADDIM
    rows = jnp.arange(LANES, dtype=jnp.int32)[:, None]
    e1 = (rows == head_of[None, :]).astype(BF16)
    e2 = (rows == head_of[None, :] + nh).astype(BF16)
    return dict(di=di, cd=cd, nh=nh,
                wz=_w(w_in, (None, d, di), (j, 0, 0)),
                wx=[_w(w_in, (None, d, WBLK), (j, 0, di // WBLK + i)) for i in range(cd // WBLK)],
                wdt=_small(wdt), wdtt=_small(wdt.T), cw=_w(conv_w, (None, 4, cd), (j, 0, 0)),
                cb=_w(conv_b.reshape(nl, 1, cd), (None, 1, cd), (j, 0, 0)),
                alr=_small(alr), dtr=_small(dtr), alc=_small(alr.T), dtc=_small(dtr.T),
                e1=_small(e1), e2=_small(e2),
                dsk=_small(jnp.repeat(d_skip[j], SSM_HEADDIM).reshape(1, di)),
                ng=_w(norm_g.reshape(nl, 1, di), (None, 1, di), (j, 0, 0)),
                wo=_w(w_out, (None, di, d), (j, 0, 0)))


def kernel(x_prompt, x_sample, state_rglru_conv, state_rglru_h, state_gdn_conv, state_gdn_S, state_ssd_conv, state_ssd_h, c_prompt, c_sample, w_mod, b_mod, w_mlp_up, w_mlp_down, final_norm_g, rg_w_in, rg_conv_w, rg_conv_b, rg_gate_w, rg_gate_b, rg_lambda, rg_w_out, gdn_w_in, gdn_conv_w, gdn_A_log, gdn_dt_bias, gdn_norm_g, gdn_w_out, ssd_w_in, ssd_conv_w, ssd_conv_b, ssd_A_log, ssd_dt_bias, ssd_D, ssd_norm_g, ssd_w_out):
    bp, l, d = x_prompt.shape
    ns = x_sample.shape[0]
    hid = w_mlp_up.shape[-1]
    assert x_sample.shape[1] == 1 and l % ROW_TILE == 0 and ns % SUBLANES == 0 and bp % GDN_SEQS == 0

    mod = _modulation(jnp.concatenate([c_prompt, c_sample], axis=0), w_mod, b_mod)
    mod = mod.reshape(DEPTH, bp + ns, 6, d)
    mod_p = jnp.transpose(mod[:, :bp], (0, 2, 1, 3))[:, :, :, None, :]
    mod_s = jnp.transpose(mod[:, bp:], (0, 2, 1, 3))
    fg = _small(final_norm_g.reshape(1, d))

    w_up, w_down = w_mlp_up.astype(BF16), w_mlp_down.astype(BF16)
    rg_in, rg_gate, rg_out = rg_w_in.astype(BF16), rg_gate_w.astype(BF16), rg_w_out.astype(BF16)
    gdn_in, gdn_out = gdn_w_in.astype(BF16), gdn_w_out.astype(BF16)
    ssd_in, ssd_out = ssd_w_in.astype(BF16), ssd_w_out.astype(BF16)

    xp = x_prompt
    xs = x_sample.reshape(ns, d)
    tails = lambda t: t[:, SUBLANES - 3:, :]
    to_rows = lambda s: jnp.swapaxes(s, 0, 1)
    out = {k: [] for k in ("p_rg_conv", "p_rg_h", "p_gdn_conv", "p_gdn_S", "p_ssd_conv", "p_ssd_h",
                           "s_rg_conv", "s_rg_h", "s_gdn_conv", "s_gdn_S", "s_ssd_conv", "s_ssd_h")}
    for i in range(DEPTH):
        j = i // N_MIXERS
        kind = i % N_MIXERS
        if kind == 0:
            p = _rg_params(j, rg_in, rg_conv_w, rg_conv_b, rg_gate, rg_gate_b, rg_lambda, rg_out)
            yp, tail, h_last = _rg_prompt(xp, mod_p[i], p)
            out["p_rg_conv"].append(tails(tail))
            out["p_rg_h"].append(h_last[:, 0, :])
            ysm, nb, h_new = _rg_decode(xs, mod_s[i], p, to_rows(state_rglru_conv[j]), state_rglru_h, j)
            out["s_rg_conv"].append(to_rows(nb))
            out["s_rg_h"].append(h_new)
        elif kind == 1:
            p = _gdn_params(j, gdn_in, gdn_conv_w, gdn_A_log, gdn_dt_bias, gdn_norm_g, gdn_out)
            q, k, v, gz, gb, gr, tail = _gdn_in(xp, mod_p[i], p)
            yp, s_fin = _gdn_scan(q, k, v, gz, gb, gr, p["ng"])
            out["p_gdn_conv"].append(tails(tail))
            out["p_gdn_S"].append(s_fin)
            qt, kt, v1, gz1, eg, be, nb = _gdn_dec_in(xs, mod_s[i], p, to_rows(state_gdn_conv[j]))
            ysm, s_new = _gdn_dec_state(qt, kt, v1, gz1, eg, be, p["ng"], state_gdn_S, j)
            out["s_gdn_conv"].append(to_rows(nb))
            out["s_gdn_S"].append(s_new)
        else:
            p = _ssd_params(j, ssd_in, ssd_conv_w, ssd_conv_b, ssd_A_log, ssd_dt_bias, ssd_D, ssd_norm_g,
                            ssd_out)
            zs, xc, bm, cm, da, at, dtt, tail = _ssd_in(xp, mod_p[i], p)
            yp, h_fin = _ssd_scan(zs, xc, bm, cm, da, at, dtt, p)
            out["p_ssd_conv"].append(tails(tail))
            out["p_ssd_h"].append(h_fin.reshape(bp, -1, SSM_HEADDIM, SSM_STATE))
            zs1, xc1, bm1, cm1, xt, ea, xdt, eae, nb = _ssd_dec_in(xs, mod_s[i], p, to_rows(state_ssd_conv[j]))
            yo, h_new = _ssd_dec_state(xt, bm1, cm1, ea, state_ssd_h, j)
            ysm = _ssd_dec_out(yo, xc1, bm1, cm1, zs1, eae, xdt, p)
            out["s_ssd_conv"].append(to_rows(nb))
            out["s_ssd_h"].append(h_new)
        final = i == DEPTH - 1
        weights = [p["wo"], _w(w_up, (None, d, hid), (i, 0, 0)), _w(w_down, (None, hid, d), (i, 0, 0)), fg]
        xp = _outproj_mlp(xp.reshape(bp * l, d), yp.reshape(bp * l, -1), mod_p[i], weights,
                          tm=ROW_TILE, rows_per_mod=l, final=final).reshape(bp, l, d)
        xs = _outproj_mlp(xs, ysm, mod_s[i], weights, tm=ns, rows_per_mod=1, final=final)

    st = {k: jnp.stack(v) for k, v in out.items()}
    return (xp, xs.reshape(ns, 1, d),
            st["p_rg_conv"], st["p_rg_h"], st["p_gdn_conv"], st["p_gdn_S"], st["p_ssd_conv"], st["p_ssd_h"],
            st["s_rg_conv"], st["s_rg_h"], st["s_gdn_conv"], st["s_gdn_S"], st["s_ssd_conv"], st["s_ssd_h"])
```

```python
import functools

import jax
import jax.numpy as jnp
from jax import lax
from jax.experimental import pallas as pl
from jax.experimental.pallas import tpu as pltpu

F32 = jnp.float32
BF16 = jnp.bfloat16

DEPTH = 4
N_MIXERS = 3
EPS = 1e-6
RG_C = 8.0
RG_BLOCK = 256
GDN_DK = 128
GDN_DV = 128
SSM_HEADDIM = 64
SSM_STATE = 128
SSM_GROUPS = 4

SUBLANES = 8
LANES = 128

VMEM_LIMIT = 56 * 1024 * 1024
ROW_TILE = 512
GDN_CHUNK = 64
GDN_STEP = 128
GDN_SEQS = 1
SSD_CHUNK = 128
MLP_HCHUNK = 1024
WBLK = 1024


def _params(*sem):
    return pltpu.CompilerParams(dimension_semantics=sem, vmem_limit_bytes=VMEM_LIMIT)


def _w(arr, blk, idx):
    return arr, pl.BlockSpec(blk, lambda *_: idx, pipeline_mode=pl.Buffered(1))


def _unzip(pairs):
    return [a for a, _ in pairs], [s for _, s in pairs]


def _bdot(a, b):
    return jnp.dot(a.astype(BF16), b.astype(BF16), preferred_element_type=F32)


def _bdot_nt(a, b):
    return lax.dot_general(a.astype(BF16), b.astype(BF16), (((1,), (1,)), ((), ())),
                           preferred_element_type=F32)


def _split(a):
    hi = a.astype(BF16)
    lo = (a - hi.astype(F32)).astype(BF16)
    return hi, lo


def _sdot_l(a, b_exact):
    hi, lo = _split(a)
    return (jnp.dot(hi, b_exact, preferred_element_type=F32)
            + jnp.dot(lo, b_exact, preferred_element_type=F32))


def _sdot_r(a_exact, b):
    hi, lo = _split(b)
    return (jnp.dot(a_exact, hi, preferred_element_type=F32)
            + jnp.dot(a_exact, lo, preferred_element_type=F32))


def _modnorm(x, sc, sh):
    ms = jnp.mean(x * x, axis=-1, keepdims=True)
    return x * lax.rsqrt(ms + EPS) * (1.0 + sc) + sh


def _sigmoid(x):
    return 0.5 * jnp.tanh(0.5 * x) + 0.5


def _silu(x):
    hx = 0.5 * x
    return hx * jnp.tanh(hx) + hx


def _conv_seq(xpre, tail, w_ref, bias):
    xcat = jnp.concatenate([tail, xpre], axis=0)
    acc = xpre * w_ref[3:4, :]
    for k in (1, 2, 3):
        shifted = pltpu.roll(xcat, k, 0)[SUBLANES:]
        acc = acc + shifted * w_ref[3 - k:4 - k, :]
    if bias is not None:
        acc = acc + bias
    return acc


def _conv_step(pre, cs_ref, nb_ref, w_ref, cs, bias):
    nb_ref[:, 0, cs] = cs_ref[:, 1, cs]
    nb_ref[:, 1, cs] = cs_ref[:, 2, cs]
    nb_ref[:, 2, cs] = pre
    out = (cs_ref[:, 0, cs] * w_ref[0:1, cs] + cs_ref[:, 1, cs] * w_ref[1:2, cs]
           + cs_ref[:, 2, cs] * w_ref[2:3, cs] + pre * w_ref[3:4, cs])
    return out if bias is None else out + bias


DRAIN_SLOTS = 3
DRAIN_AHEAD = 2


def _zero_operand():
    return jnp.zeros((1,), jnp.int32), pl.BlockSpec(memory_space=pltpu.SMEM)


def _pipelined(dots, consume, buf, zero):
    n = len(dots)
    for j in range(min(DRAIN_AHEAD, n)):
        buf[zero + j % DRAIN_SLOTS] = dots[j]()
    for j in range(n):
        if j + DRAIN_AHEAD < n:
            buf[zero + (j + DRAIN_AHEAD) % DRAIN_SLOTS] = dots[j + DRAIN_AHEAD]()
        consume(j, buf[zero + j % DRAIN_SLOTS])


def _block_tri(n, chunk, lower):
    r = lax.broadcasted_iota(jnp.int32, (n, n), 0)
    c = lax.broadcasted_iota(jnp.int32, (n, n), 1)
    same = (r // chunk) == (c // chunk)
    tri = (c <= r) if lower else (r <= c)
    return jnp.where(same & tri, 1.0, 0.0).astype(BF16)


def _mod_kernel(c_ref, w_ref, b_ref, o_ref):
    o_ref[...] = _bdot(_silu(c_ref[...]), w_ref[...]) + b_ref[...]


def _modulation(c_all, w_mod, b_mod):
    nb, d = c_all.shape
    n6 = w_mod.shape[-1]
    tn = 1024
    return pl.pallas_call(
        _mod_kernel,
        grid=(DEPTH, n6 // tn),
        in_specs=[pl.BlockSpec((nb, d), lambda l, n: (0, 0)),
                  pl.BlockSpec((None, d, tn), lambda l, n: (l, 0, n)),
                  pl.BlockSpec((None, 1, tn), lambda l, n: (l, 0, n))],
        out_specs=pl.BlockSpec((None, nb, tn), lambda l, n: (l, 0, n)),
        out_shape=jax.ShapeDtypeStruct((DEPTH, nb, n6), F32),
        compiler_params=_params("parallel", "parallel"),
        name="modulation",
    )(c_all, w_mod, b_mod.reshape(DEPTH, 1, n6))


def _time_rows(blk_ref):
    ncb, t, _ = blk_ref.shape
    ng = t // SUBLANES
    return jnp.concatenate(
        [jnp.concatenate([blk_ref[cb, pl.ds(s, ng, stride=SUBLANES), :] for cb in range(ncb)], axis=1)
         for s in range(SUBLANES)], axis=0)


def _outproj_mlp_kernel(x_ref, y_ref, mod_ref, wo_ref, wu_ref, wd_ref, fg_ref, o_ref, *, final):
    y = y_ref[...] if len(y_ref.shape) == 2 else _time_rows(y_ref).astype(BF16)
    x1 = x_ref[...] + mod_ref[2] * jnp.dot(y, wo_ref[...], preferred_element_type=F32)
    u = _modnorm(x1, mod_ref[4], mod_ref[3]).astype(BF16)
    hidden = wu_ref.shape[1]
    acc = None
    for c in range(hidden // MLP_HCHUNK):
        cs = slice(c * MLP_HCHUNK, (c + 1) * MLP_HCHUNK)
        h = jnp.maximum(jnp.dot(u, wu_ref[:, cs], preferred_element_type=F32), 0.0)
        p = jnp.dot((h * h).astype(BF16), wd_ref[cs, :], preferred_element_type=F32)
        acc = p if acc is None else acc + p
    x2 = x1 + mod_ref[5] * acc
    if final:
        ms = jnp.mean(x2 * x2, axis=-1, keepdims=True)
        x2 = x2 * lax.rsqrt(ms + EPS) * fg_ref[...]
    o_ref[...] = x2


def _outproj_mlp(x, y, mod, weights, *, tm, rows_per_mod, final):
    n, d = x.shape
    if rows_per_mod == 1:
        mod_spec = pl.BlockSpec((6, tm, d), lambda i: (0, i, 0))
    else:
        per = rows_per_mod // tm
        mod_spec = pl.BlockSpec((6, None, 1, d), lambda i: (0, i // per, 0, 0))
    if y.ndim == 2:
        y_spec = pl.BlockSpec((tm, y.shape[1]), lambda i: (i, 0))
    else:
        assert tm == ROW_TILE and rows_per_mod == y.shape[2]
        y_spec = pl.BlockSpec((None, y.shape[1], tm, LANES), lambda i: (i // per, 0, i % per, 0))
    w_arrs, w_specs = _unzip(weights)
    return pl.pallas_call(
        functools.partial(_outproj_mlp_kernel, final=final),
        grid=(n // tm,),
        in_specs=[pl.BlockSpec((tm, d), lambda i: (i, 0)), y_spec, mod_spec] + w_specs,
        out_specs=pl.BlockSpec((tm, d), lambda i: (i, 0)),
        out_shape=jax.ShapeDtypeStruct((n, d), F32),
        compiler_params=_params("parallel"),
        name="outproj_mlp",
    )(x, y, mod, *w_arrs)


def _rg_gate_block(n, xbr, gwr_ref, gwi_ref, gb_ref, logsig, a_s, b_s):
    sl = slice(n * RG_BLOCK, (n + 1) * RG_BLOCK)
    xb = xbr.astype(BF16)
    gr = jnp.dot(xb, gwr_ref[n], preferred_element_type=F32)
    gi = jnp.dot(xb, gwi_ref[n], preferred_element_type=F32)
    log_a = (RG_C * logsig[:, sl]) * _sigmoid(gr + gb_ref[0:1, sl])
    a_s[:, sl] = jnp.exp(log_a)
    th = jnp.tanh(log_a)
    z = -2.0 * th / (1.0 - th)
    root = jnp.where(z > 0.0, z * lax.rsqrt(z), 0.0)
    b_s[:, sl] = root * _sigmoid(gi + gb_ref[1:2, sl]) * xbr


def _rg_gates(xbr, gwr_ref, gwi_ref, gb_ref, logsig, a_s, b_s):
    for n in range(xbr.shape[1] // RG_BLOCK):
        _rg_gate_block(n, xbr[:, n * RG_BLOCK:(n + 1) * RG_BLOCK], gwr_ref, gwi_ref, gb_ref, logsig, a_s, b_s)


def _rg_prompt_kernel(z_ref, x_ref, mod_ref, wy_ref, wx_ref, cw_ref, cb_ref, gwr_ref, gwi_ref, gb_ref,
                      lam_ref, y_ref, tail_ref, h_ref, a_s, b_s, y_s, p_s):
    t, w = a_s.shape
    ng = t // SUBLANES
    ncb = w // LANES

    @pl.when(pl.program_id(1) == 0)
    def _():
        tail_ref[...] = jnp.zeros_like(tail_ref)
        h_ref[...] = jnp.zeros_like(h_ref)

    u_t = _modnorm(x_ref[...], mod_ref[1], mod_ref[0]).astype(BF16)
    r_i = lax.broadcasted_iota(jnp.int32, (t, t), 0)
    t_i = lax.broadcasted_iota(jnp.int32, (t, t), 1)
    perm = jnp.where(t_i == (r_i % SUBLANES) * ng + r_i // SUBLANES, 1.0, 0.0).astype(BF16)
    u = jnp.dot(perm, u_t, preferred_element_type=F32).astype(BF16)
    sub = lax.broadcasted_iota(jnp.int32, (SUBLANES, RG_BLOCK), 0)
    ntail = tail_ref.shape[0] // SUBLANES
    logsig = jax.nn.log_sigmoid(lam_ref[...])
    nblk = w // RG_BLOCK
    order = [(kind, n) for n in range(nblk) for kind in ("x", "y")]

    def consume(i, pre):
        kind, n = order[i]
        cs = slice(n * RG_BLOCK, (n + 1) * RG_BLOCK)
        if kind == "y":
            y_s[:, cs] = jax.nn.gelu(pre)
            return
        acc = pre * cw_ref[3:4, cs] + cb_ref[:, cs]
        for k in (1, 2, 3):
            top = [jnp.where(sub == 0,
                             pltpu.roll(tail_ref[(ntail - k + i) * SUBLANES:(ntail - k + i + 1) * SUBLANES, cs], 1, 0),
                             pltpu.roll(pre[(ng - k + i) * SUBLANES:(ng - k + i + 1) * SUBLANES, :], 1, 0))
                   for i in range(k)]
            shifted = jnp.concatenate(top + [pre[:(ng - k) * SUBLANES]], axis=0)
            acc = acc + shifted * cw_ref[3 - k:4 - k, cs]
        tail_ref[:, cs] = pre[(ng - ntail) * SUBLANES:, :]
        _rg_gate_block(n, acc, gwr_ref, gwi_ref, gb_ref, logsig, a_s, b_s)

    def dot_of(kind, n):
        w_ref = wx_ref if kind == "x" else wy_ref
        return jnp.dot(u, w_ref[:, n * RG_BLOCK:(n + 1) * RG_BLOCK], preferred_element_type=F32)

    _pipelined([functools.partial(dot_of, *blk) for blk in order], consume, p_s, z_ref[0])

    def compose(gi, carry):
        ca, cbb = carry
        r0 = pl.multiple_of(gi * SUBLANES, SUBLANES)
        a = a_s[pl.ds(r0, SUBLANES), :]
        ca = a * ca
        cbb = a * cbb + b_s[pl.ds(r0, SUBLANES), :]
        a_s[pl.ds(r0, SUBLANES), :] = ca
        b_s[pl.ds(r0, SUBLANES), :] = cbb
        return ca, cbb

    a_end, b_end = lax.fori_loop(0, ng, compose, (jnp.ones((SUBLANES, w), F32), jnp.zeros((SUBLANES, w), F32)),
                                 unroll=4)
    h_in = jnp.zeros((SUBLANES, w), F32)
    for _ in range(SUBLANES):
        h_in = jnp.where(lax.broadcasted_iota(jnp.int32, (SUBLANES, w), 0) == 0, h_ref[...],
                         pltpu.roll(a_end * h_in + b_end, 1, 0))
    h_ref[...] = (a_end * h_in + b_end)[SUBLANES - 1:, :]
    hy = (a_s[...] * jnp.concatenate([h_in] * ng, axis=0) + b_s[...]) * y_s[...]
    for cb in range(ncb):
        y_ref[cb] = hy[:, cb * LANES:(cb + 1) * LANES]


def _rg_weights(p):
    return [p["wy"], p["wx"], p["cw"], p["cb"], p["gwr"], p["gwi"], p["gb"], p["lam"]]


def _rg_prompt(x, mod, p):
    b, l, d = x.shape
    w = p["width"]
    t = ROW_TILE
    w_arrs, w_specs = _unzip(_rg_weights(p))
    ncb = w // LANES
    ntail = (4 - 1) * SUBLANES
    zero, zero_spec = _zero_operand()
    return pl.pallas_call(
        _rg_prompt_kernel,
        grid=(b, l // t),
        in_specs=[zero_spec, pl.BlockSpec((None, t, d), lambda i, c: (i, c, 0)),
                  pl.BlockSpec((6, None, 1, d), lambda i, c: (0, i, 0, 0))] + w_specs,
        out_specs=[pl.BlockSpec((None, ncb, t, LANES), lambda i, c: (i, 0, c, 0)),
                   pl.BlockSpec((None, ntail, w), lambda i, c: (i, 0, 0)),
                   pl.BlockSpec((None, 1, w), lambda i, c: (i, 0, 0))],
        out_shape=[jax.ShapeDtypeStruct((b, ncb, l, LANES), F32),
                   jax.ShapeDtypeStruct((b, ntail, w), F32),
                   jax.ShapeDtypeStruct((b, 1, w), F32)],
        scratch_shapes=[pltpu.VMEM((t, w), F32)] * 3 + [pltpu.VMEM((DRAIN_SLOTS, t, RG_BLOCK), F32)],
        compiler_params=_params("parallel", "arbitrary"),
        name="rglru_prompt",
    )(zero, x, mod, *w_arrs)


def _rg_decode_kernel(x_ref, mod_ref, wy_ref, wx_ref, cw_ref, cb_ref, gwr_ref, gwi_ref, gb_ref,
                      lam_ref, cs_ref, h0_ref, y_ref, nb_ref, h_ref, a_s, b_s):
    u = _modnorm(x_ref[...], mod_ref[1], mod_ref[0]).astype(BF16)
    xpre = jnp.dot(u, wx_ref[...], preferred_element_type=F32)
    xbr = _conv_step(xpre, cs_ref, nb_ref, cw_ref, slice(None), cb_ref[...])
    _rg_gates(xbr, gwr_ref, gwi_ref, gb_ref, jax.nn.log_sigmoid(lam_ref[...]), a_s, b_s)
    h = a_s[...] * h0_ref[...] + b_s[...]
    h_ref[...] = h
    y_br = jax.nn.gelu(jnp.dot(u, wy_ref[...], preferred_element_type=F32))
    y_ref[...] = (h * y_br).astype(BF16)


def _full(shape):
    return pl.BlockSpec(shape, lambda *_: (0,) * len(shape))


def _rg_decode(x, mod, p, conv_state, h0, j):
    n, d = x.shape
    w = p["width"]
    w_arrs, w_specs = _unzip(_rg_weights(p))
    return pl.pallas_call(
        _rg_decode_kernel,
        grid=(1,),
        in_specs=[_full((n, d)), _full((6, n, d))] + w_specs
        + [_state_spec(conv_state, j), _state_spec(h0, j)],
        out_specs=[_full((n, w)), _full((n, 3, w)), _full((n, w))],
        out_shape=[jax.ShapeDtypeStruct((n, w), BF16),
                   jax.ShapeDtypeStruct((n, 3, w), F32),
                   jax.ShapeDtypeStruct((n, w), F32)],
        scratch_shapes=[pltpu.VMEM((n, w), F32), pltpu.VMEM((n, w), F32)],
        compiler_params=_params("arbitrary"),
        name="rglru_decode",
    )(x, mod, *w_arrs, conv_state, h0)


def _gdn_qkvg(u, wqkv_ref, wg_ref, conv_fn, q_ref, k_ref, v_ref, gz_ref, buf, zero):
    hk = q_ref.shape[-1]
    nh = hk // GDN_DK
    dsts = ((q_ref, GDN_DK ** -0.5), (k_ref, 1.0), (v_ref, None))
    order = (0, None, 1, 2)

    def consume(i, pre):
        j = order[i]
        if j is None:
            gz_ref[...] = _silu(pre)
            return
        dst, scale = dsts[j]
        act = _silu(conv_fn(pre, j))
        if scale is None:
            dst[...] = act
        else:
            for h in range(nh):
                hs = slice(h * GDN_DK, (h + 1) * GDN_DK)
                xh = act[:, hs]
                ss = jnp.sum(xh * xh, axis=-1, keepdims=True)
                dst[:, hs] = xh * (lax.rsqrt(ss + EPS) * scale)

    def dot_of(j):
        w = wg_ref[...] if j is None else wqkv_ref[:, j * hk:(j + 1) * hk]
        return jnp.dot(u, w, preferred_element_type=F32)

    _pipelined([functools.partial(dot_of, j) for j in order], consume, buf, zero)


def _gdn_in_kernel(z_ref, x_ref, mod_ref, wqkv_ref, wg_ref, wab_ref, wabt_ref, cw_ref, alr_ref, dtr_ref,
                   alc_ref, dtc_ref, q_ref, k_ref, v_ref, gz_ref, gb_ref, gr_ref, tail_ref, p_s):
    t = x_ref.shape[0]
    hk = q_ref.shape[-1]
    nh = hk // GDN_DK

    @pl.when(pl.program_id(1) == 0)
    def _():
        tail_ref[...] = jnp.zeros_like(tail_ref)

    u = _modnorm(x_ref[...], mod_ref[1], mod_ref[0]).astype(BF16)

    def conv_fn(pre, j):
        cs = slice(j * hk, (j + 1) * hk)
        out = _conv_seq(pre, tail_ref[:, cs], cw_ref.at[:, cs], None)
        tail_ref[:, cs] = pre[t - SUBLANES:, :]
        return out

    _gdn_qkvg(u, wqkv_ref, wg_ref, conv_fn, q_ref, k_ref, v_ref, gz_ref, p_s, z_ref[0])

    ab = jnp.dot(u, wab_ref[...], preferred_element_type=F32)
    abt = lax.dot_general(wabt_ref[...], u, (((1,), (1,)), ((), ())),
                          preferred_element_type=F32)
    g_col = -jnp.exp(alr_ref[...]) * jax.nn.softplus(ab + dtr_ref[...])
    g_row = -jnp.exp(alc_ref[...]) * jax.nn.softplus(abt + dtc_ref[...])
    gc = _sdot_r(_block_tri(t, GDN_CHUNK, True), g_col)
    gr = _sdot_l(g_row, _block_tri(t, GDN_CHUNK, False))
    lane = lax.broadcasted_iota(jnp.int32, ab.shape, 1)
    gb_ref[...] = jnp.where(lane < nh, gc, _sigmoid(ab))
    gr_ref[...] = gr[:SUBLANES, :]


def _gdn_in(x, mod, p):
    b, l, d = x.shape
    hk = p["hk"]
    t = ROW_TILE
    tok = lambda width: pl.BlockSpec((None, t, width), lambda i, c: (i, c, 0))
    w_arrs, w_specs = _unzip([p["wqkv"], p["wg"], p["wab"], p["wabt"], p["cw"], p["alr"], p["dtr"],
                              p["alc"], p["dtc"]])
    zero, zero_spec = _zero_operand()
    return pl.pallas_call(
        _gdn_in_kernel,
        grid=(b, l // t),
        in_specs=[zero_spec, tok(d), pl.BlockSpec((6, None, 1, d), lambda i, c: (0, i, 0, 0))] + w_specs,
        out_specs=[tok(hk), tok(hk), tok(hk), tok(hk), tok(LANES),
                   pl.BlockSpec((None, SUBLANES, t), lambda i, c: (i, 0, c)),
                   pl.BlockSpec((None, SUBLANES, 3 * hk), lambda i, c: (i, 0, 0))],
        out_shape=[jax.ShapeDtypeStruct((b, l, hk), F32)] * 4
        + [jax.ShapeDtypeStruct((b, l, LANES), F32),
           jax.ShapeDtypeStruct((b, SUBLANES, l), F32),
           jax.ShapeDtypeStruct((b, SUBLANES, 3 * hk), F32)],
        scratch_shapes=[pltpu.VMEM((DRAIN_SLOTS, t, hk), F32)],
        compiler_params=_params("parallel", "arbitrary"),
        name="gdn_in",
    )(zero, x, mod, *w_arrs)


def _gdn_scan_kernel(q_ref, k_ref, v_ref, gz_ref, gb_ref, gr_ref, ng_ref, y_ref, s_ref):
    c = GDN_CHUNK
    nseq, nh = s_ref.shape[0], s_ref.shape[1]

    @pl.when(pl.program_id(1) == 0)
    def _():
        s_ref[...] = jnp.zeros_like(s_ref)

    r = lax.broadcasted_iota(jnp.int32, (c, c), 0)
    cc = lax.broadcasted_iota(jnp.int32, (c, c), 1)
    causal = r >= cc
    strict = r > cc
    eye = jnp.where(r == cc, 1.0, 0.0)
    merge = []
    blk = 1
    while blk < c:
        merge.append((r // (2 * blk) == cc // (2 * blk)) & ((r // blk) % 2 == 1) & ((cc // blk) % 2 == 0))
        blk *= 2

    nchunk = q_ref.shape[1] // c
    trip = [(b, j, h) for b in range(nseq) for j in range(nchunk) for h in range(nh)]
    rows = lambda j: slice(j * c, (j + 1) * c)
    cols = lambda h: slice(h * GDN_DK, (h + 1) * GDN_DK)
    g_col = lambda b, j, h: gb_ref[b, rows(j), h:h + 1]
    beta_of = lambda b, j, h: gb_ref[b, rows(j), nh + h:nh + h + 1]
    g_row = lambda b, j, h: gr_ref[b, h:h + 1, rows(j)]
    blk_of = lambda ref, b, j, h: ref[b, rows(j), cols(h)]

    kks = [_bdot_nt(jnp.concatenate([blk_of(k_ref, *t) * beta_of(*t), blk_of(q_ref, *t)], axis=0),
                    blk_of(k_ref, *t)) for t in trip]
    a_s, qk_s = [], []
    for t, kk in zip(trip, kks):
        decay = jnp.where(causal, jnp.exp(jnp.where(causal, g_col(*t) - g_row(*t), 0.0)), 0.0)
        a_s.append(jnp.where(strict, kk[:c] * decay, 0.0))
        qk_s.append((kk[c:] * decay).astype(BF16))
    xs = [eye - jnp.where(merge[0], a, 0.0) for a in a_s]
    for m in merge[1:]:
        ts = [_bdot(jnp.where(m, a, 0.0), x) for a, x in zip(a_s, xs)]
        xs = [x - _bdot(x, t_) for x, t_ in zip(xs, ts)]
    uws = {}
    for t, x in zip(trip, xs):
        kb = blk_of(k_ref, *t) * beta_of(*t)
        rhs = jnp.concatenate([blk_of(v_ref, *t) * beta_of(*t), kb * jnp.exp(g_col(*t))], axis=1)
        uws[t] = _bdot(x, rhs)
    qks = dict(zip(trip, qk_s))

    for j in range(nchunk):
        cur = [(b, j, h) for b in range(nseq) for h in range(nh)]
        states = {t: s_ref[t[0], t[2]] for t in cur}
        ws_qs = {t: _bdot(jnp.concatenate([uws[t][:, GDN_DV:], blk_of(q_ref, *t) * jnp.exp(g_col(*t))],
                                          axis=0), states[t]) for t in cur}
        v_news = {t: uws[t][:, :GDN_DV] - ws_qs[t][:c] for t in cur}
        outs = {t: ws_qs[t][c:] + jnp.dot(qks[t], v_news[t].astype(BF16), preferred_element_type=F32)
                for t in cur}
        for t in cur:
            g_last = g_row(*t)[:, c - 1:c]
            k_dec = blk_of(k_ref, *t) * jnp.exp(g_last - g_col(*t))
            s_ref[t[0], t[2]] = jnp.exp(g_last) * states[t] + lax.dot_general(
                k_dec.astype(BF16), v_news[t].astype(BF16), (((0,), (0,)), ((), ())),
                preferred_element_type=F32)
        for t in cur:
            o = outs[t]
            ms = jnp.mean(o * o, axis=-1, keepdims=True)
            y_ref[t[0], rows(j), cols(t[2])] = (o * lax.rsqrt(ms + EPS) * ng_ref[...]
                                                * blk_of(gz_ref, *t)).astype(BF16)


def _gdn_scan(q, k, v, gz, gb, gr, norm_g):
    b, l, hk = q.shape
    nh = hk // GDN_DK
    t = GDN_STEP
    ns = GDN_SEQS
    tok = lambda width: pl.BlockSpec((ns, t, width), lambda i, c: (i, c, 0))
    ng_arr, ng_spec = norm_g
    return pl.pallas_call(
        _gdn_scan_kernel,
        grid=(b // ns, l // t),
        in_specs=[tok(hk), tok(hk), tok(hk), tok(hk), tok(LANES),
                  pl.BlockSpec((ns, SUBLANES, t), lambda i, c: (i, 0, c)), ng_spec],
        out_specs=[tok(hk), pl.BlockSpec((ns, nh, GDN_DK, GDN_DV), lambda i, c: (i, 0, 0, 0))],
        out_shape=[jax.ShapeDtypeStruct((b, l, hk), BF16),
                   jax.ShapeDtypeStruct((b, nh, GDN_DK, GDN_DV), F32)],
        compiler_params=_params("parallel", "arbitrary"),
        name="gdn_scan",
    )(q, k, v, gz, gb, gr, ng_arr)


def _gdn_dec_in_kernel(z_ref, x_ref, mod_ref, wqkv_ref, wg_ref, wab_ref, cw_ref, alr_ref, dtr_ref, cs_ref,
                       qt_ref, kt_ref, v_ref, gz_ref, eg_ref, be_ref, nb_ref, q_s, k_s, p_s):
    hk = v_ref.shape[-1]
    nh = hk // GDN_DK
    u = _modnorm(x_ref[...], mod_ref[1], mod_ref[0]).astype(BF16)

    def conv_fn(pre, j):
        return _conv_step(pre, cs_ref, nb_ref, cw_ref, slice(j * hk, (j + 1) * hk), None)

    _gdn_qkvg(u, wqkv_ref, wg_ref, conv_fn, q_s, k_s, v_ref, gz_ref, p_s, z_ref[0])
    ab = jnp.dot(u, wab_ref[...], preferred_element_type=F32)
    e_g = jnp.exp(-jnp.exp(alr_ref[...]) * jax.nn.softplus(ab + dtr_ref[...]))
    beta = _sigmoid(ab)
    n = ab.shape[0]
    for h in range(nh):
        hs = slice(h * GDN_DK, (h + 1) * GDN_DK)
        qt_ref[h] = q_s[:, hs].T
        kt_ref[h] = k_s[:, hs].T
        eg_ref[h] = jnp.broadcast_to(e_g[:, h:h + 1], (n, LANES))
        be_ref[h] = jnp.broadcast_to(beta[:, nh + h:nh + h + 1], (n, LANES))


def _state_spec(state, j):
    return pl.BlockSpec((None,) + state.shape[1:], lambda *_: (j,) + (0,) * (state.ndim - 1))


def _gdn_dec_in(x, mod, p, conv_state, j):
    n, d = x.shape
    hk = p["hk"]
    nh = hk // GDN_DK
    w_arrs, w_specs = _unzip([p["wqkv"], p["wg"], p["wab"], p["cw"], p["alr"], p["dtr"]])
    zero, zero_spec = _zero_operand()
    return pl.pallas_call(
        _gdn_dec_in_kernel,
        grid=(1,),
        in_specs=[zero_spec, _full((n, d)), _full((6, n, d))] + w_specs + [_state_spec(conv_state, j)],
        out_specs=[_full((nh, GDN_DK, n)), _full((nh, GDN_DK, n)), _full((n, hk)), _full((n, hk)),
                   _full((nh, n, LANES)), _full((nh, n, LANES)), _full((n, 3, 3 * hk))],
        out_shape=[jax.ShapeDtypeStruct((nh, GDN_DK, n), F32)] * 2
        + [jax.ShapeDtypeStruct((n, hk), F32)] * 2
        + [jax.ShapeDtypeStruct((nh, n, LANES), F32)] * 2
        + [jax.ShapeDtypeStruct((n, 3, 3 * hk), F32)],
        scratch_shapes=[pltpu.VMEM((n, hk), F32)] * 2 + [pltpu.VMEM((DRAIN_SLOTS, n, hk), F32)],
        compiler_params=_params("arbitrary"),
        name="gdn_decode_in",
    )(zero, x, mod, *w_arrs, conv_state)


def _gdn_dec_state_kernel(qt_ref, kt_ref, v_ref, gz_ref, eg_ref, be_ref, ng_ref, s0_ref,
                          y_ref, s_ref, o_s):
    n = s0_ref.shape[0]
    for b in range(n):
        kc = kt_ref[:, b:b + 1]
        qc = qt_ref[:, b:b + 1]
        s0 = s0_ref[b]
        k_s0 = jnp.sum(kc * s0, axis=0, keepdims=True)
        q_s0 = jnp.sum(qc * s0, axis=0, keepdims=True)
        qk = jnp.sum(qc * kc, axis=0, keepdims=True)
        e_g = eg_ref[b:b + 1, :]
        v_new = be_ref[b:b + 1, :] * (v_ref[b:b + 1, :] - e_g * k_s0)
        o_s[b:b + 1, :] = e_g * q_s0 + qk * v_new
        s_ref[b] = e_g * s0 + kc * v_new
    o = o_s[...]
    ms = jnp.mean(o * o, axis=-1, keepdims=True)
    y_ref[...] = (o * lax.rsqrt(ms + EPS) * ng_ref[...] * gz_ref[...]).astype(BF16)


def _gdn_dec_state(qt, kt, v, gz, eg, be, norm_g, s0, j):
    _, n, nh, dk, dv = s0.shape
    head = lambda a, bb: pl.BlockSpec((None, a, bb), lambda h: (h, 0, 0))
    col = pl.BlockSpec((n, dv), lambda h: (0, h))
    ng_arr, ng_spec = norm_g
    return pl.pallas_call(
        _gdn_dec_state_kernel,
        grid=(nh,),
        in_specs=[head(dk, n), head(dk, n), col, col, head(n, LANES), head(n, LANES), ng_spec,
                  pl.BlockSpec((None, n, None, dk, dv), lambda h: (j, 0, h, 0, 0))],
        out_specs=[col, pl.BlockSpec((n, None, dk, dv), lambda h: (0, h, 0, 0))],
        out_shape=[jax.ShapeDtypeStruct((n, nh * dv), BF16), jax.ShapeDtypeStruct((n, nh, dk, dv), F32)],
        scratch_shapes=[pltpu.VMEM((n, dv), F32)],
        compiler_params=_params("parallel"),
        name="gdn_decode_state",
    )(qt, kt, v, gz, eg, be, ng_arr, s0)


def _ssd_proj(u, wz_ref, wx_refs, conv_fn, zs_ref, xs_ref, bm_ref, cm_ref, buf, zero):
    gn = bm_ref.shape[-1]
    nz = zs_ref.shape[-1] // WBLK
    nx = len(wx_refs)
    order = [blk for pair in zip([("x", j) for j in range(nx)], [("z", j) for j in range(nz)] + [None] * nx)
             for blk in pair if blk is not None]

    def consume(i, pre):
        kind, j = order[i]
        cs = slice(j * WBLK, (j + 1) * WBLK)
        if kind == "z":
            zs_ref[:, cs] = _silu(pre)
            return
        act = _silu(conv_fn(pre, cs))
        if j + 1 < nx:
            xs_ref[:, cs] = act
        else:
            bm_ref[...] = act[:, :gn]
            cm_ref[...] = act[:, gn:]

    def dot_of(kind, j):
        w = wz_ref[:, j * WBLK:(j + 1) * WBLK] if kind == "z" else wx_refs[j][...]
        return jnp.dot(u, w, preferred_element_type=F32)

    _pipelined([functools.partial(dot_of, *blk) for blk in order], consume, buf, zero)


def _ssd_in_kernel(z_ref, x_ref, mod_ref, wz_ref, wx0_ref, wx1_ref, wx2_ref, wdt_ref, wdtt_ref, cw_ref,
                   cb_ref, dtr_ref, alr_ref, dtc_ref, alc_ref,
                   zs_ref, xs_ref, bm_ref, cm_ref, da_ref, at_ref, dtt_ref, tail_ref, p_s):
    t = x_ref.shape[0]
    nh = at_ref.shape[0]

    @pl.when(pl.program_id(1) == 0)
    def _():
        tail_ref[...] = jnp.zeros_like(tail_ref)

    u = _modnorm(x_ref[...], mod_ref[1], mod_ref[0]).astype(BF16)

    def conv_fn(pre, cs):
        out = _conv_seq(pre, tail_ref[:, cs], cw_ref.at[:, cs], cb_ref[:, cs])
        tail_ref[:, cs] = pre[t - SUBLANES:, :]
        return out

    _ssd_proj(u, wz_ref, (wx0_ref, wx1_ref, wx2_ref), conv_fn, zs_ref, xs_ref, bm_ref, cm_ref, p_s, z_ref[0])

    dt_c =jax.nn.softplus(jnp.dot(u, wdt_ref[...], preferred_element_type=F32) + dtr_ref[...])
    dt_r = jax.nn.softplus(lax.dot_general(wdtt_ref[...], u, (((1,), (1,)), ((), ())),
                                           preferred_element_type=F32) + dtc_ref[...])
    acs_c = _sdot_r(_block_tri(t, SSD_CHUNK, True), dt_c * -jnp.exp(alr_ref[...]))
    acs_r = _sdot_l(dt_r * -jnp.exp(alc_ref[...]), _block_tri(t, SSD_CHUNK, False))
    lane = lax.broadcasted_iota(jnp.int32, dt_c.shape, 1)
    da_ref[...] = jnp.where(lane < nh, dt_c, pltpu.roll(acs_c, nh, 1))
    at_ref[...] = acs_r[:nh, :]
    dtt_ref[...] = dt_r[:nh, :]


def _ssd_in(x, mod, p):
    b, l, d = x.shape
    di, cd, nh = p["di"], p["cd"], p["nh"]
    gn = (cd - di) // 2
    t = ROW_TILE
    tok = lambda width: pl.BlockSpec((None, t, width), lambda i, c: (i, c, 0))
    w_arrs, w_specs = _unzip([p["wz"], *p["wx"], p["wdt"], p["wdtt"], p["cw"], p["cb"], p["dtr"],
                              p["alr"], p["dtc"], p["alc"]])
    rows_spec = pl.BlockSpec((None, nh, t), lambda i, c: (i, 0, c))
    zero, zero_spec = _zero_operand()
    return pl.pallas_call(
        _ssd_in_kernel,
        grid=(b, l // t),
        in_specs=[zero_spec, tok(d), pl.BlockSpec((6, None, 1, d), lambda i, c: (0, i, 0, 0))] + w_specs,
        out_specs=[tok(di), tok(di), tok(gn), tok(gn), tok(LANES), rows_spec, rows_spec,
                   pl.BlockSpec((None, SUBLANES, cd), lambda i, c: (i, 0, 0))],
        out_shape=[jax.ShapeDtypeStruct((b, l, di), F32), jax.ShapeDtypeStruct((b, l, di), F32),
                   jax.ShapeDtypeStruct((b, l, gn), F32), jax.ShapeDtypeStruct((b, l, gn), F32),
                   jax.ShapeDtypeStruct((b, l, LANES), F32),
                   jax.ShapeDtypeStruct((b, nh, l), F32), jax.ShapeDtypeStruct((b, nh, l), F32),
                   jax.ShapeDtypeStruct((b, SUBLANES, cd), F32)],
        scratch_shapes=[pltpu.VMEM((DRAIN_SLOTS, t, WBLK), F32)],
        compiler_params=_params("parallel", "arbitrary"),
        name="ssd_in",
    )(zero, x, mod, *w_arrs)


def _ssd_scan_kernel(zs_ref, xs_ref, bm_ref, cm_ref, da_ref, at_ref, dtt_ref, e2_ref, dsk_ref, ng_ref,
                     y_ref, hout_ref, ht_s):
    c = xs_ref.shape[0]
    di = xs_ref.shape[1]
    n = SSM_STATE
    pdim = SSM_HEADDIM
    gw = di // SSM_GROUPS
    hpg = gw // pdim
    nh = di // pdim
    ci = pl.program_id(1)

    @pl.when(ci == 0)
    def _():
        ht_s[...] = jnp.zeros_like(ht_s)

    da = da_ref[...]
    lane_a = lax.broadcasted_iota(jnp.int32, da.shape, 1)
    is_acs = (lane_a >= nh) & (lane_a < 2 * nh)
    acs = jnp.where(is_acs, da, 0.0)
    dt_on_acs = jnp.where(is_acs, pltpu.roll(da, nh, 1), 0.0)
    eacs_e = _sdot_l(jnp.where(is_acs, jnp.exp(acs), 0.0), e2_ref[...])
    dsdt_e = _sdot_l(jnp.exp(acs[c - 1:c, :] - acs) * dt_on_acs, e2_ref[...])
    xs = xs_ref[...]
    xd = xs * dsdt_e

    r = lax.broadcasted_iota(jnp.int32, (c, c), 0)
    cc = lax.broadcasted_iota(jnp.int32, (c, c), 1)
    causal = r >= cc
    lane = lax.broadcasted_iota(jnp.int32, (c, 2 * pdim), 1)

    groups = range(SSM_GROUPS)
    gsl = lambda g: slice(g * gw, (g + 1) * gw)
    nsl = lambda g: slice(g * n, (g + 1) * n)
    cbs = [_bdot_nt(cm_ref[:, nsl(g)], bm_ref[:, nsl(g)]) for g in groups]
    hts = [ht_s[g] for g in groups]
    y_offs = [_bdot(cm_ref[:, nsl(g)], hts[g]) * eacs_e[:, gsl(g)] for g in groups]
    for g in groups:
        ht_s[g] = eacs_e[c - 1:c, gsl(g)] * hts[g] + _bdot(bm_ref[:, nsl(g)].T, xd[:, gsl(g)])
    for g in groups:
        y_parts = []
        for pr in range(hpg // 2):
            ms = []
            for hh in (2 * pr, 2 * pr + 1):
                h = g * hpg + hh
                seg = da[:, nh + h:nh + h + 1] - at_ref[h:h + 1, :]
                lm = jnp.where(causal, jnp.exp(jnp.where(causal, seg, 0.0)), 0.0)
                ms.append((cbs[g] * lm * dtt_ref[h:h + 1, :]).astype(BF16))
            ps = slice(g * gw + pr * 2 * pdim, g * gw + (pr + 1) * 2 * pdim)
            xp = xs[:, ps].astype(BF16)
            zero = jnp.zeros_like(xp)
            rhs = jnp.concatenate([jnp.where(lane < pdim, xp, zero), jnp.where(lane >= pdim, xp, zero)], axis=0)
            y_d = jnp.dot(jnp.concatenate(ms, axis=1), rhs, preferred_element_type=F32)
            os_ = slice(pr * 2 * pdim, (pr + 1) * 2 * pdim)
            y_parts.append(y_d + y_offs[g][:, os_] + dsk_ref[:, ps] * xs[:, ps])
        yz = jnp.concatenate(y_parts, axis=1) * zs_ref[:, gsl(g)]
        ms_ = jnp.mean(yz * yz, axis=-1, keepdims=True)
        y_ref[:, gsl(g)] = (yz * lax.rsqrt(ms_ + EPS) * ng_ref[:, gsl(g)]).astype(BF16)

    @pl.when(ci == pl.num_programs(1) - 1)
    def _():
        for g in groups:
            hout_ref[gsl(g), :] = ht_s[g].T


def _ssd_scan(zs, xs, bm, cm, da, at, dtt, p):
    b, l, di = xs.shape
    gn = bm.shape[-1]
    nh = p["nh"]
    gw = di // SSM_GROUPS
    t = SSD_CHUNK
    tok = lambda width: pl.BlockSpec((None, t, width), lambda i, c: (i, c, 0))
    rows_spec = pl.BlockSpec((None, nh, t), lambda i, c: (i, 0, c))
    w_arrs, w_specs = _unzip([p["e2"], p["dsk"], p["ng"]])
    return pl.pallas_call(
        _ssd_scan_kernel,
        grid=(b, l // t),
        in_specs=[tok(di), tok(di), tok(gn), tok(gn), tok(LANES), rows_spec, rows_spec] + w_specs,
        out_specs=[tok(di), pl.BlockSpec((None, di, SSM_STATE), lambda i, c: (i, 0, 0))],
        out_shape=[jax.ShapeDtypeStruct((b, l, di), BF16),
                   jax.ShapeDtypeStruct((b, di, SSM_STATE), F32)],
        scratch_shapes=[pltpu.VMEM((SSM_GROUPS, SSM_STATE, gw), F32)],
        compiler_params=_params("parallel", "arbitrary"),
        name="ssd_scan",
    )(zs, xs, bm, cm, da, at, dtt, *w_arrs)


def _ssd_dec_in_kernel(z_ref, x_ref, mod_ref, wz_ref, wx0_ref, wx1_ref, wx2_ref, wdt_ref, cw_ref, cb_ref,
                       dtr_ref, alr_ref, e1_ref, cs_ref,
                       zs_ref, xs_ref, bm_ref, cm_ref, xt_ref, ea_ref, xdt_ref, eae_ref, nb_ref, p_s):
    n = x_ref.shape[0]
    di = zs_ref.shape[-1]
    nh = di // SSM_HEADDIM
    u = _modnorm(x_ref[...], mod_ref[1], mod_ref[0]).astype(BF16)

    def conv_fn(pre, cs):
        return _conv_step(pre, cs_ref, nb_ref, cw_ref, cs, cb_ref[:, cs])

    _ssd_proj(u, wz_ref, (wx0_ref, wx1_ref, wx2_ref), conv_fn, zs_ref, xs_ref, bm_ref, cm_ref, p_s, z_ref[0])
    dt = jax.nn.softplus(jnp.dot(u, wdt_ref[...], preferred_element_type=F32) + dtr_ref[...])
    e_a = jnp.exp(dt * -jnp.exp(alr_ref[...]))
    xdt_ref[...] = xs_ref[...] * _sdot_l(dt, e1_ref[...])
    eae_ref[...] = _sdot_l(e_a, e1_ref[...])
    for j in range(di // LANES):
        xt_ref[j * LANES:(j + 1) * LANES, :] = xdt_ref[:, j * LANES:(j + 1) * LANES].T
    for h in range(nh):
        ea_ref[h] = jnp.broadcast_to(e_a[:, h:h + 1], (n, LANES))


def _ssd_dec_in(x, mod, p, conv_state, j):
    n, d = x.shape
    di, cd, nh = p["di"], p["cd"], p["nh"]
    gn = (cd - di) // 2
    w_arrs, w_specs = _unzip([p["wz"], *p["wx"], p["wdt"], p["cw"], p["cb"], p["dtr"], p["alr"], p["e1"]])
    zero, zero_spec = _zero_operand()
    return pl.pallas_call(
        _ssd_dec_in_kernel,
        grid=(1,),
        in_specs=[zero_spec, _full((n, d)), _full((6, n, d))] + w_specs + [_state_spec(conv_state, j)],
        out_specs=[_full((n, di)), _full((n, di)), _full((n, gn)), _full((n, gn)), _full((di, n)),
                   _full((nh, n, LANES)), _full((n, di)), _full((n, di)), _full((n, 3, cd))],
        out_shape=[jax.ShapeDtypeStruct((n, di), F32), jax.ShapeDtypeStruct((n, di), F32),
                   jax.ShapeDtypeStruct((n, gn), F32), jax.ShapeDtypeStruct((n, gn), F32),
                   jax.ShapeDtypeStruct((di, n), F32),
                   jax.ShapeDtypeStruct((nh, n, LANES), F32),
                   jax.ShapeDtypeStruct((n, di), F32), jax.ShapeDtypeStruct((n, di), F32),
                   jax.ShapeDtypeStruct((n, 3, cd), F32)],
        scratch_shapes=[pltpu.VMEM((DRAIN_SLOTS, n, WBLK), F32)],
        compiler_params=_params("arbitrary"),
        name="ssd_decode_in",
    )(zero, x, mod, *w_arrs, conv_state)


def _ssd_dec_state_kernel(xt_ref, bm_ref, cm_ref, ea_ref, h0_ref, yo_ref, h_ref):
    n = h0_ref.shape[0]
    pdim = h0_ref.shape[2]
    for b in range(n):
        bb = bm_ref[b:b + 1, :]
        c8 = jnp.broadcast_to(cm_ref[b:b + 1, :], (SUBLANES, bb.shape[1]))
        outs = []
        for hh in range(2):
            h0 = h0_ref[b, hh]
            outs.append(_bdot_nt(c8, h0)[0:1, :])
            xc = xt_ref[hh * pdim:(hh + 1) * pdim, b:b + 1]
            h_ref[b, hh] = ea_ref[hh, b:b + 1, :] * h0 + xc * bb
        yo_ref[b:b + 1, :] = jnp.concatenate(outs, axis=1)


def _ssd_dec_state(xt, bm, cm, ea, h0, j):
    _, n, nh, pdim, ns = h0.shape
    hpg = nh // SSM_GROUPS
    return pl.pallas_call(
        _ssd_dec_state_kernel,
        grid=(nh // 2,),
        in_specs=[pl.BlockSpec((2 * pdim, n), lambda i: (i, 0)),
                  pl.BlockSpec((n, ns), lambda i: (0, (2 * i) // hpg)),
                  pl.BlockSpec((n, ns), lambda i: (0, (2 * i) // hpg)),
                  pl.BlockSpec((2, n, LANES), lambda i: (i, 0, 0)),
                  pl.BlockSpec((None, n, 2, pdim, ns), lambda i: (j, 0, i, 0, 0))],
        out_specs=[pl.BlockSpec((n, 2 * pdim), lambda i: (0, i)),
                   pl.BlockSpec((n, 2, pdim, ns), lambda i: (0, i, 0, 0))],
        out_shape=[jax.ShapeDtypeStruct((n, nh * pdim), F32), jax.ShapeDtypeStruct((n, nh, pdim, ns), F32)],
        compiler_params=_params("parallel"),
        name="ssd_decode_state",
    )(xt, bm, cm, ea, h0)


def _ssd_dec_out_kernel(yo_ref, xs_ref, bm_ref, cm_ref, zs_ref, eae_ref, xdt_ref, dsk_ref, ng_ref, y_ref):
    di = xs_ref.shape[1]
    gw = di // SSM_GROUPS
    n = SSM_STATE
    for g in range(SSM_GROUPS):
        gs = slice(g * gw, (g + 1) * gw)
        cb = jnp.sum(cm_ref[:, g * n:(g + 1) * n] * bm_ref[:, g * n:(g + 1) * n], axis=-1, keepdims=True)
        y = cb * xdt_ref[:, gs] + eae_ref[:, gs] * yo_ref[:, gs] + dsk_ref[:, gs] * xs_ref[:, gs]
        yz = y * zs_ref[:, gs]
        ms = jnp.mean(yz * yz, axis=-1, keepdims=True)
        y_ref[:, gs] = (yz * lax.rsqrt(ms + EPS) * ng_ref[:, gs]).astype(BF16)


def _ssd_dec_out(yo, xs, bm, cm, zs, eae, xdt, p):
    n, di = xs.shape
    gn = bm.shape[1]
    w_arrs, w_specs = _unzip([p["dsk"], p["ng"]])
    return pl.pallas_call(
        _ssd_dec_out_kernel,
        grid=(1,),
        in_specs=[_full((n, di)), _full((n, di)), _full((n, gn)), _full((n, gn)), _full((n, di)),
                  _full((n, di)), _full((n, di))] + w_specs,
        out_specs=_full((n, di)),
        out_shape=jax.ShapeDtypeStruct((n, di), BF16),
        compiler_params=_params("arbitrary"),
        name="ssd_decode_out",
    )(yo, xs, bm, cm, zs, eae, xdt, *w_arrs)


def _pad_lanes(v):
    return jnp.pad(v.astype(F32), (0, LANES - v.shape[0])).reshape(1, LANES)


def _pad_cols(w):
    return jnp.pad(w, ((0, 0), (0, LANES - w.shape[1])))


def _small(a):
    return _w(a, a.shape, (0,) * a.ndim)


def _rg_params(j, w_in, conv_w, conv_b, gate_w, gate_b, lam, w_out):
    nl, d, w2 = w_in.shape
    w = w2 // 2
    nblk = w // RG_BLOCK
    gate = (None, None, nblk, RG_BLOCK, RG_BLOCK)
    return dict(width=w,
                wy=_w(w_in, (None, d, w), (j, 0, 0)), wx=_w(w_in, (None, d, w), (j, 0, 1)),
                cw=_w(conv_w, (None, 4, w), (j, 0, 0)), cb=_w(conv_b.reshape(nl, 1, w), (None, 1, w), (j, 0, 0)),
                gwr=_w(gate_w, gate, (j, 0, 0, 0, 0)), gwi=_w(gate_w, gate, (j, 1, 0, 0, 0)),
                gb=_w(gate_b, (None, 2, w), (j, 0, 0)), lam=_w(lam.reshape(nl, 1, w), (None, 1, w), (j, 0, 0)),
                wo=_w(w_out, (None, w, d), (j, 0, 0)))


def _gdn_params(j, w_in, conv_w, a_log, dt_bias, norm_g, w_out):
    nl, d, _ = w_in.shape
    qkv = conv_w.shape[-1]
    hk = qkv // 3
    wab = _pad_cols(w_in[j, :, qkv + hk:])
    alr, dtr = _pad_lanes(a_log[j]), _pad_lanes(dt_bias[j])
    return dict(hk=hk,
                wqkv=_w(w_in, (None, d, qkv), (j, 0, 0)), wg=_w(w_in, (None, d, hk), (j, 0, qkv // hk)),
                wab=_small(wab), wabt=_small(wab.T), cw=_w(conv_w, (None, 4, qkv), (j, 0, 0)),
                alr=_small(alr), dtr=_small(dtr), alc=_small(alr.T), dtc=_small(dtr.T),
                ng=_w(norm_g.reshape(nl, 1, GDN_DV), (None, 1, GDN_DV), (j, 0, 0)),
                wo=_w(w_out, (None, hk, d), (j, 0, 0)))


def _ssd_params(j, w_in, conv_w, conv_b, a_log, dt_bias, d_skip, norm_g, w_out):
    nl, d, _ = w_in.shape
    cd = conv_w.shape[-1]
    nh = a_log.shape[-1]
    di = nh * SSM_HEADDIM
    assert di % WBLK == 0 and (cd - di) == WBLK
    wdt = _pad_cols(w_in[j, :, di + cd:])
    alr, dtr = _pad_lanes(a_log[j]), _pad_lanes(dt_bias[j])
    head_of = jnp.arange(di, dtype=jnp.int32) // SSM_HEADDIM
    rows = jnp.arange(LANES, dtype=jnp.int32)[:, None]
    e1 = (rows == head_of[None, :]).astype(BF16)
    e2 = (rows == head_of[None, :] + nh).astype(BF16)
    return dict(di=di, cd=cd, nh=nh,
                wz=_w(w_in, (None, d, di), (j, 0, 0)),
                wx=[_w(w_in, (None, d, WBLK), (j, 0, di // WBLK + i)) for i in range(cd // WBLK)],
                wdt=_small(wdt), wdtt=_small(wdt.T), cw=_w(conv_w, (None, 4, cd), (j, 0, 0)),
                cb=_w(conv_b.reshape(nl, 1, cd), (None, 1, cd), (j, 0, 0)),
                alr=_small(alr), dtr=_small(dtr), alc=_small(alr.T), dtc=_small(dtr.T),
                e1=_small(e1), e2=_small(e2),
                dsk=_small(jnp.repeat(d_skip[j], SSM_HEADDIM).reshape(1, di)),
                ng=_w(norm_g.reshape(nl, 1, di), (None, 1, di), (j, 0, 0)),
                wo=_w(w_out, (None, di, d), (j, 0, 0)))


def kernel(x_prompt, x_sample, state_rglru_conv, state_rglru_h, state_gdn_conv, state_gdn_S, state_ssd_conv, state_ssd_h, c_prompt, c_sample, w_mod, b_mod, w_mlp_up, w_mlp_down, final_norm_g, rg_w_in, rg_conv_w, rg_conv_b, rg_gate_w, rg_gate_b, rg_lambda, rg_w_out, gdn_w_in, gdn_conv_w, gdn_A_log, gdn_dt_bias, gdn_norm_g, gdn_w_out, ssd_w_in, ssd_conv_w, ssd_conv_b, ssd_A_log, ssd_dt_bias, ssd_D, ssd_norm_g, ssd_w_out):
    bp, l, d = x_prompt.shape
    ns = x_sample.shape[0]
    hid = w_mlp_up.shape[-1]
    assert x_sample.shape[1] == 1 and l % ROW_TILE == 0 and ns % SUBLANES == 0 and bp % GDN_SEQS == 0

    mod = _modulation(jnp.concatenate([c_prompt, c_sample], axis=0), w_mod, b_mod)
    mod = mod.reshape(DEPTH, bp + ns, 6, d)
    mod_p = jnp.transpose(mod[:, :bp], (0, 2, 1, 3))[:, :, :, None, :]
    mod_s = jnp.transpose(mod[:, bp:], (0, 2, 1, 3))
    fg = _small(final_norm_g.reshape(1, d))

    w_up, w_down = w_mlp_up.astype(BF16), w_mlp_down.astype(BF16)
    rg_in, rg_gate, rg_out = rg_w_in.astype(BF16), rg_gate_w.astype(BF16), rg_w_out.astype(BF16)
    gdn_in, gdn_out = gdn_w_in.astype(BF16), gdn_w_out.astype(BF16)
    ssd_in, ssd_out = ssd_w_in.astype(BF16), ssd_w_out.astype(BF16)

    xp = x_prompt
    xs = x_sample.reshape(ns, d)
    tails = lambda t: t[:, SUBLANES - 3:, :]
    out = {k: [] for k in ("p_rg_conv", "p_rg_h", "p_gdn_conv", "p_gdn_S", "p_ssd_conv", "p_ssd_h",
                           "s_rg_conv", "s_rg_h", "s_gdn_conv", "s_gdn_S", "s_ssd_conv", "s_ssd_h")}
    for i in range(DEPTH):
        j = i // N_MIXERS
        kind = i % N_MIXERS
        if kind == 0:
            p = _rg_params(j, rg_in, rg_conv_w, rg_conv_b, rg_gate, rg_gate_b, rg_lambda, rg_out)
            yp, tail, h_last = _rg_prompt(xp, mod_p[i], p)
            out["p_rg_conv"].append(tail[:, SUBLANES - 1::SUBLANES, :])
            out["p_rg_h"].append(h_last[:, 0, :])
            ysm, nb, h_new = _rg_decode(xs, mod_s[i], p, state_rglru_conv, state_rglru_h, j)
            out["s_rg_conv"].append(nb)
            out["s_rg_h"].append(h_new)
        elif kind == 1:
            p = _gdn_params(j, gdn_in, gdn_conv_w, gdn_A_log, gdn_dt_bias, gdn_norm_g, gdn_out)
            q, k, v, gz, gb, gr, tail = _gdn_in(xp, mod_p[i], p)
            yp, s_fin = _gdn_scan(q, k, v, gz, gb, gr, p["ng"])
            out["p_gdn_conv"].append(tails(tail))
            out["p_gdn_S"].append(s_fin)
            qt, kt, v1, gz1, eg, be, nb = _gdn_dec_in(xs, mod_s[i], p, state_gdn_conv, j)
            ysm, s_new = _gdn_dec_state(qt, kt, v1, gz1, eg, be, p["ng"], state_gdn_S, j)
            out["s_gdn_conv"].append(nb)
            out["s_gdn_S"].append(s_new)
        else:
            p = _ssd_params(j, ssd_in, ssd_conv_w, ssd_conv_b, ssd_A_log, ssd_dt_bias, ssd_D, ssd_norm_g,
                            ssd_out)
            zs, xc, bm, cm, da, at, dtt, tail = _ssd_in(xp, mod_p[i], p)
            yp, h_fin = _ssd_scan(zs, xc, bm, cm, da, at, dtt, p)
            out["p_ssd_conv"].append(tails(tail))
            out["p_ssd_h"].append(h_fin.reshape(bp, -1, SSM_HEADDIM, SSM_STATE))
            zs1, xc1, bm1, cm1, xt, ea, xdt, eae, nb = _ssd_dec_in(xs, mod_s[i], p, state_ssd_conv, j)
            yo, h_new = _ssd_dec_state(xt, bm1, cm1, ea, state_ssd_h, j)
            ysm = _ssd_dec_out(yo, xc1, bm1, cm1, zs1, eae, xdt, p)
            out["s_ssd_conv"].append(nb)
            out["s_ssd_h"].append(h_new)
        final = i == DEPTH - 1
        weights = [p["wo"], _w(w_up, (None, d, hid), (i, 0, 0)), _w(w_down, (None, hid, d), (i, 0, 0)), fg]
        y_rows = yp if yp.ndim == 4 else yp.reshape(bp * l, -1)
        xp = _outproj_mlp(xp.reshape(bp * l, d), y_rows, mod_p[i], weights,
                          tm=ROW_TILE, rows_per_mod=l, final=final).reshape(bp, l, d)
        xs = _outproj_mlp(xs, ysm, mod_s[i], weights, tm=ns, rows_per_mod=1, final=final)

    st = {k: jnp.stack(v) for k, v in out.items()}
    return (xp, xs.reshape(ns, 1, d),
            st["p_rg_conv"], st["p_rg_h"], st["p_gdn_conv"], st["p_gdn_S"], st["p_ssd_conv"], st["p_ssd_h"],
            st["s_rg_conv"], st["s_rg_h"], st["s_gdn_conv"], st["s_gdn_S"], st["s_ssd_conv"], st["s_ssd_h"])
```

```python
import functools

import jax
import jax.numpy as jnp
from jax import lax
from jax.experimental import pallas as pl
from jax.experimental.pallas import tpu as pltpu

F32 = jnp.float32
BF16 = jnp.bfloat16

DEPTH = 4
N_MIXERS = 3
EPS = 1e-6
RG_C = 8.0
RG_BLOCK = 256
GDN_DK = 128
GDN_DV = 128
SSM_HEADDIM = 64
SSM_STATE = 128
SSM_GROUPS = 4

SUBLANES = 8
LANES = 128

VMEM_LIMIT = 56 * 1024 * 1024
ROW_TILE = 512
GDN_CHUNK = 64
GDN_STEP = 128
GDN_SEQS = 1
SSD_CHUNK = 128
MLP_HCHUNK = 1024
WBLK = 1024
TAIL_ROWS = 3 * SUBLANES


def _params(*sem):
    return pltpu.CompilerParams(dimension_semantics=sem, vmem_limit_bytes=VMEM_LIMIT)


def _w(arr, blk, idx):
    return arr, pl.BlockSpec(blk, lambda *_: idx, pipeline_mode=pl.Buffered(1))


def _unzip(pairs):
    return [a for a, _ in pairs], [s for _, s in pairs]


def _bdot(a, b):
    return jnp.dot(a.astype(BF16), b.astype(BF16), preferred_element_type=F32)


def _bdot_nt(a, b):
    return lax.dot_general(a.astype(BF16), b.astype(BF16), (((1,), (1,)), ((), ())),
                           preferred_element_type=F32)


def _split(a):
    hi = a.astype(BF16)
    lo = (a - hi.astype(F32)).astype(BF16)
    return hi, lo


def _sdot_l(a, b_exact):
    hi, lo = _split(a)
    return (jnp.dot(hi, b_exact, preferred_element_type=F32)
            + jnp.dot(lo, b_exact, preferred_element_type=F32))


def _sdot_r(a_exact, b):
    hi, lo = _split(b)
    return (jnp.dot(a_exact, hi, preferred_element_type=F32)
            + jnp.dot(a_exact, lo, preferred_element_type=F32))


def _modnorm(x, sc, sh):
    ms = jnp.mean(x * x, axis=-1, keepdims=True)
    return x * lax.rsqrt(ms + EPS) * (1.0 + sc) + sh


def _sigmoid(x):
    return 0.5 * jnp.tanh(0.5 * x) + 0.5


def _silu(x):
    hx = 0.5 * x
    return hx * jnp.tanh(hx) + hx


def _conv_seq(xpre, tail, w_ref, bias):
    xcat = jnp.concatenate([tail, xpre], axis=0)
    acc = xpre * w_ref[3:4, :]
    for k in (1, 2, 3):
        shifted = pltpu.roll(xcat, k, 0)[SUBLANES:]
        acc = acc + shifted * w_ref[3 - k:4 - k, :]
    if bias is not None:
        acc = acc + bias
    return acc


def _interleave_rows(u_t):
    t = u_t.shape[0]
    r = lax.broadcasted_iota(jnp.int32, (t, t), 0)
    c = lax.broadcasted_iota(jnp.int32, (t, t), 1)
    perm = jnp.where(c == (r % SUBLANES) * (t // SUBLANES) + r // SUBLANES, 1.0, 0.0).astype(BF16)
    return jnp.dot(perm, u_t, preferred_element_type=F32).astype(BF16)


def _conv_interleaved(pre, tail_ref, cs, w_ref, bias):
    ng = pre.shape[0] // SUBLANES
    ntail = tail_ref.shape[0] // SUBLANES
    sub = lax.broadcasted_iota(jnp.int32, (SUBLANES, pre.shape[1]), 0)
    grp = lambda a, g: a[g * SUBLANES:(g + 1) * SUBLANES]
    acc = pre * w_ref[3:4, cs]
    if bias is not None:
        acc = acc + bias
    for k in (1, 2, 3):
        top = [jnp.where(sub == 0,
                         pltpu.roll(tail_ref[(ntail - k + i) * SUBLANES:(ntail - k + i + 1) * SUBLANES, cs], 1, 0),
                         pltpu.roll(grp(pre, ng - k + i), 1, 0)) for i in range(k)]
        shifted = jnp.concatenate(top + [pre[:(ng - k) * SUBLANES]], axis=0)
        acc = acc + shifted * w_ref[3 - k:4 - k, cs]
    tail_ref[:, cs] = pre[(ng - ntail) * SUBLANES:, :]
    return acc


def _conv_step(pre, cs_ref, nb_ref, w_ref, cs, bias):
    nb_ref[:, 0, cs] = cs_ref[:, 1, cs]
    nb_ref[:, 1, cs] = cs_ref[:, 2, cs]
    nb_ref[:, 2, cs] = pre
    out = (cs_ref[:, 0, cs] * w_ref[0:1, cs] + cs_ref[:, 1, cs] * w_ref[1:2, cs]
           + cs_ref[:, 2, cs] * w_ref[2:3, cs] + pre * w_ref[3:4, cs])
    return out if bias is None else out + bias


DRAIN_SLOTS = 3
DRAIN_AHEAD = 2


def _zero_operand():
    return jnp.zeros((1,), jnp.int32), pl.BlockSpec(memory_space=pltpu.SMEM)


def _pipelined(dots, consume, buf, zero):
    n = len(dots)
    for j in range(min(DRAIN_AHEAD, n)):
        buf[zero + j % DRAIN_SLOTS] = dots[j]()
    for j in range(n):
        if j + DRAIN_AHEAD < n:
            buf[zero + (j + DRAIN_AHEAD) % DRAIN_SLOTS] = dots[j + DRAIN_AHEAD]()
        consume(j, buf[zero + j % DRAIN_SLOTS])


def _block_tri(n, chunk, lower):
    r = lax.broadcasted_iota(jnp.int32, (n, n), 0)
    c = lax.broadcasted_iota(jnp.int32, (n, n), 1)
    same = (r // chunk) == (c // chunk)
    tri = (c <= r) if lower else (r <= c)
    return jnp.where(same & tri, 1.0, 0.0).astype(BF16)


def _mod_kernel(c_ref, w_ref, b_ref, o_ref):
    o_ref[...] = _bdot(_silu(c_ref[...]), w_ref[...]) + b_ref[...]


def _modulation(c_all, w_mod, b_mod):
    nb, d = c_all.shape
    n6 = w_mod.shape[-1]
    return pl.pallas_call(
        _mod_kernel,
        grid=(DEPTH, n6 // d),
        in_specs=[pl.BlockSpec((nb, d), lambda l, n: (0, 0)),
                  pl.BlockSpec((None, d, d), lambda l, n: (l, 0, n)),
                  pl.BlockSpec((None, 1, d), lambda l, n: (l, 0, n))],
        out_specs=pl.BlockSpec((None, None, nb, d), lambda l, n: (l, n, 0, 0)),
        out_shape=jax.ShapeDtypeStruct((DEPTH, n6 // d, nb, d), F32),
        compiler_params=_params("parallel", "parallel"),
        name="modulation",
    )(c_all, w_mod, b_mod.reshape(DEPTH, 1, n6))


def _mod_operand(mod):
    arr, layer, n_rows, first_row = mod
    return arr, pl.BlockSpec((None, arr.shape[1], n_rows, arr.shape[3]),
                             lambda *_: (layer, 0, first_row // n_rows, 0))


def _mrow(mod_ref, k, row):
    return mod_ref[k] if row is None else mod_ref[k, pl.ds(row, 1), :]


def _time_rows(blk_ref):
    ncb, t, _ = blk_ref.shape
    ng = t // SUBLANES
    return jnp.concatenate(
        [jnp.concatenate([blk_ref[cb, pl.ds(s, ng, stride=SUBLANES), :] for cb in range(ncb)], axis=1)
         for s in range(SUBLANES)], axis=0)


def _outproj_mlp_kernel(x_ref, y_ref, mod_ref, wo_ref, wu_ref, wd_ref, fg_ref, o_ref, *, final, tiles_per_seq):
    row = None if tiles_per_seq is None else pl.program_id(0) // tiles_per_seq
    y = y_ref[...] if len(y_ref.shape) == 2 else _time_rows(y_ref).astype(BF16)
    x1 = x_ref[...] + _mrow(mod_ref, 2, row) * jnp.dot(y, wo_ref[...], preferred_element_type=F32)
    u = _modnorm(x1, _mrow(mod_ref, 4, row), _mrow(mod_ref, 3, row)).astype(BF16)
    hidden = wu_ref.shape[1]
    acc = None
    for c in range(hidden // MLP_HCHUNK):
        cs = slice(c * MLP_HCHUNK, (c + 1) * MLP_HCHUNK)
        h = jnp.maximum(jnp.dot(u, wu_ref[:, cs], preferred_element_type=F32), 0.0)
        p = jnp.dot((h * h).astype(BF16), wd_ref[cs, :], preferred_element_type=F32)
        acc = p if acc is None else acc + p
    x2 = x1 + _mrow(mod_ref, 5, row) * acc
    if final:
        ms = jnp.mean(x2 * x2, axis=-1, keepdims=True)
        x2 = x2 * lax.rsqrt(ms + EPS) * fg_ref[...]
    o_ref[...] = x2


def _outproj_mlp(x, y, mod, weights, *, tm, rows_per_mod, final):
    n, d = x.shape
    per = None if rows_per_mod == 1 else rows_per_mod // tm
    assert per is not None or n == tm
    mod_arr, mod_spec = _mod_operand(mod)
    if y.ndim == 2:
        y_spec = pl.BlockSpec((tm, y.shape[1]), lambda i: (i, 0))
    else:
        assert tm == ROW_TILE and rows_per_mod == y.shape[2]
        y_spec = pl.BlockSpec((None, y.shape[1], tm, LANES), lambda i: (i // per, 0, i % per, 0))
    w_arrs, w_specs = _unzip(weights)
    return pl.pallas_call(
        functools.partial(_outproj_mlp_kernel, final=final, tiles_per_seq=per),
        grid=(n // tm,),
        in_specs=[pl.BlockSpec((tm, d), lambda i: (i, 0)), y_spec, mod_spec] + w_specs,
        out_specs=pl.BlockSpec((tm, d), lambda i: (i, 0)),
        out_shape=jax.ShapeDtypeStruct((n, d), F32),
        compiler_params=_params("parallel"),
        name="outproj_mlp",
    )(x, y, mod_arr, *w_arrs)


def _rg_gate_block(n, xbr, gwr_ref, gwi_ref, gb_ref, logsig, a_s, b_s):
    sl = slice(n * RG_BLOCK, (n + 1) * RG_BLOCK)
    xb = xbr.astype(BF16)
    gr = jnp.dot(xb, gwr_ref[n], preferred_element_type=F32)
    gi = jnp.dot(xb, gwi_ref[n], preferred_element_type=F32)
    log_a = (RG_C * logsig[:, sl]) * _sigmoid(gr + gb_ref[0:1, sl])
    a_s[:, sl] = jnp.exp(log_a)
    th = jnp.tanh(log_a)
    z = -2.0 * th / (1.0 - th)
    root = jnp.where(z > 0.0, z * lax.rsqrt(z), 0.0)
    b_s[:, sl] = root * _sigmoid(gi + gb_ref[1:2, sl]) * xbr


def _rg_gates(xbr, gwr_ref, gwi_ref, gb_ref, logsig, a_s, b_s):
    for n in range(xbr.shape[1] // RG_BLOCK):
        _rg_gate_block(n, xbr[:, n * RG_BLOCK:(n + 1) * RG_BLOCK], gwr_ref, gwi_ref, gb_ref, logsig, a_s, b_s)


def _rg_prompt_kernel(z_ref, x_ref, mod_ref, wy_ref, wx_ref, cw_ref, cb_ref, gwr_ref, gwi_ref, gb_ref,
                      lam_ref, y_ref, tail_ref, h_ref, a_s, b_s, y_s, p_s):
    t, w = a_s.shape
    ng = t // SUBLANES
    ncb = w // LANES

    @pl.when(pl.program_id(1) == 0)
    def _():
        tail_ref[...] = jnp.zeros_like(tail_ref)
        h_ref[...] = jnp.zeros_like(h_ref)

    seq = pl.program_id(0)
    u_t = _modnorm(x_ref[...], _mrow(mod_ref, 1, seq), _mrow(mod_ref, 0, seq)).astype(BF16)
    u = _interleave_rows(u_t)
    logsig = jax.nn.log_sigmoid(lam_ref[...])
    nblk = w // RG_BLOCK
    order = [(kind, n) for n in range(nblk) for kind in ("x", "y")]

    def consume(i, pre):
        kind, n = order[i]
        cs = slice(n * RG_BLOCK, (n + 1) * RG_BLOCK)
        if kind == "y":
            y_s[:, cs] = jax.nn.gelu(pre)
            return
        acc = _conv_interleaved(pre, tail_ref, cs, cw_ref, cb_ref[:, cs])
        _rg_gate_block(n, acc, gwr_ref, gwi_ref, gb_ref, logsig, a_s, b_s)

    def dot_of(kind, n):
        w_ref = wx_ref if kind == "x" else wy_ref
        return jnp.dot(u, w_ref[:, n * RG_BLOCK:(n + 1) * RG_BLOCK], preferred_element_type=F32)

    _pipelined([functools.partial(dot_of, *blk) for blk in order], consume, p_s, z_ref[0])

    def compose(gi, carry):
        ca, cbb = carry
        r0 = pl.multiple_of(gi * SUBLANES, SUBLANES)
        a = a_s[pl.ds(r0, SUBLANES), :]
        ca = a * ca
        cbb = a * cbb + b_s[pl.ds(r0, SUBLANES), :]
        a_s[pl.ds(r0, SUBLANES), :] = ca
        b_s[pl.ds(r0, SUBLANES), :] = cbb
        return ca, cbb

    a_end, b_end = lax.fori_loop(0, ng, compose, (jnp.ones((SUBLANES, w), F32), jnp.zeros((SUBLANES, w), F32)),
                                 unroll=4)
    h_in = jnp.zeros((SUBLANES, w), F32)
    for _ in range(SUBLANES):
        h_in = jnp.where(lax.broadcasted_iota(jnp.int32, (SUBLANES, w), 0) == 0, h_ref[...],
                         pltpu.roll(a_end * h_in + b_end, 1, 0))
    h_ref[...] = (a_end * h_in + b_end)[SUBLANES - 1:, :]
    hy = (a_s[...] * jnp.concatenate([h_in] * ng, axis=0) + b_s[...]) * y_s[...]
    for cb in range(ncb):
        y_ref[cb] = hy[:, cb * LANES:(cb + 1) * LANES]


def _rg_weights(p):
    return [p["wy"], p["wx"], p["cw"], p["cb"], p["gwr"], p["gwi"], p["gb"], p["lam"]]


def _rg_prompt(x, mod, p):
    b, l, d = x.shape
    w = p["width"]
    t = ROW_TILE
    w_arrs, w_specs = _unzip(_rg_weights(p))
    ncb = w // LANES
    ntail = (4 - 1) * SUBLANES
    zero, zero_spec = _zero_operand()
    return pl.pallas_call(
        _rg_prompt_kernel,
        grid=(b, l // t),
        in_specs=[zero_spec, pl.BlockSpec((None, t, d), lambda i, c: (i, c, 0)),
                  _mod_operand(mod)[1]] + w_specs,
        out_specs=[pl.BlockSpec((None, ncb, t, LANES), lambda i, c: (i, 0, c, 0)),
                   pl.BlockSpec((None, ntail, w), lambda i, c: (i, 0, 0)),
                   pl.BlockSpec((None, 1, w), lambda i, c: (i, 0, 0))],
        out_shape=[jax.ShapeDtypeStruct((b, ncb, l, LANES), F32),
                   jax.ShapeDtypeStruct((b, ntail, w), F32),
                   jax.ShapeDtypeStruct((b, 1, w), F32)],
        scratch_shapes=[pltpu.VMEM((t, w), F32)] * 3 + [pltpu.VMEM((DRAIN_SLOTS, t, RG_BLOCK), F32)],
        compiler_params=_params("parallel", "arbitrary"),
        name="rglru_prompt",
    )(zero, x, mod[0], *w_arrs)


def _rg_decode_kernel(x_ref, mod_ref, wy_ref, wx_ref, cw_ref, cb_ref, gwr_ref, gwi_ref, gb_ref,
                      lam_ref, cs_ref, h0_ref, y_ref, nb_ref, h_ref, a_s, b_s):
    u = _modnorm(x_ref[...], mod_ref[1], mod_ref[0]).astype(BF16)
    xpre = jnp.dot(u, wx_ref[...], preferred_element_type=F32)
    xbr = _conv_step(xpre, cs_ref, nb_ref, cw_ref, slice(None), cb_ref[...])
    _rg_gates(xbr, gwr_ref, gwi_ref, gb_ref, jax.nn.log_sigmoid(lam_ref[...]), a_s, b_s)
    h = a_s[...] * h0_ref[...] + b_s[...]
    h_ref[...] = h
    y_br = jax.nn.gelu(jnp.dot(u, wy_ref[...], preferred_element_type=F32))
    y_ref[...] = (h * y_br).astype(BF16)


def _full(shape):
    return pl.BlockSpec(shape, lambda *_: (0,) * len(shape))


def _rg_decode(x, mod, p, conv_state, h0, j):
    n, d = x.shape
    w = p["width"]
    w_arrs, w_specs = _unzip(_rg_weights(p))
    return pl.pallas_call(
        _rg_decode_kernel,
        grid=(1,),
        in_specs=[_full((n, d)), _mod_operand(mod)[1]] + w_specs
        + [_state_spec(conv_state, j), _state_spec(h0, j)],
        out_specs=[_full((n, w)), _full((n, 3, w)), _full((n, w))],
        out_shape=[jax.ShapeDtypeStruct((n, w), BF16),
                   jax.ShapeDtypeStruct((n, 3, w), F32),
                   jax.ShapeDtypeStruct((n, w), F32)],
        scratch_shapes=[pltpu.VMEM((n, w), F32), pltpu.VMEM((n, w), F32)],
        compiler_params=_params("arbitrary"),
        name="rglru_decode",
    )(x, mod[0], *w_arrs, conv_state, h0)


def _gdn_qkvg(u, u_gate, wqkv_ref, wg_ref, conv_fn, q_ref, k_ref, v_ref, gz_ref, buf, zero):
    hk = gz_ref.shape[-1]
    nh = hk // GDN_DK
    dsts = ((q_ref, GDN_DK ** -0.5), (k_ref, 1.0), (v_ref, None))
    order = (0, None, 1, 2)

    def consume(i, pre):
        j = order[i]
        if j is None:
            gz_ref[...] = _silu(pre)
            return
        dst, scale = dsts[j]
        act = _silu(conv_fn(pre, j))
        for h in range(nh):
            xh = act[:, h * GDN_DK:(h + 1) * GDN_DK]
            if scale is not None:
                ss = jnp.sum(xh * xh, axis=-1, keepdims=True)
                xh = xh * (lax.rsqrt(ss + EPS) * scale)
            dst[:, h * GDN_DK:(h + 1) * GDN_DK] = xh

    def dot_of(j):
        if j is None:
            return jnp.dot(u_gate, wg_ref[...], preferred_element_type=F32)
        return jnp.dot(u, wqkv_ref[:, j * hk:(j + 1) * hk], preferred_element_type=F32)

    _pipelined([functools.partial(dot_of, j) for j in order], consume, buf, zero)


def _gdn_in_kernel(z_ref, x_ref, mod_ref, wqkv_ref, wg_ref, wab_ref, wabt_ref, cw_ref, alr_ref, dtr_ref,
                   alc_ref, dtc_ref, q_ref, k_ref, v_ref, gz_ref, gb_ref, gr_ref, tail_ref, p_s):
    t = x_ref.shape[0]
    hk = gz_ref.shape[-1]
    nh = hk // GDN_DK

    @pl.when(pl.program_id(1) == 0)
    def _():
        tail_ref[...] = jnp.zeros_like(tail_ref)

    seq = pl.program_id(0)
    u = _modnorm(x_ref[...], _mrow(mod_ref, 1, seq), _mrow(mod_ref, 0, seq)).astype(BF16)

    def conv_fn(pre, j):
        cs = slice(j * hk, (j + 1) * hk)
        out = _conv_seq(pre, tail_ref[:, cs], cw_ref.at[:, cs], None)
        tail_ref[:, cs] = pre[t - SUBLANES:, :]
        return out

    _gdn_qkvg(u, u, wqkv_ref, wg_ref, conv_fn, q_ref, k_ref, v_ref, gz_ref, p_s, z_ref[0])

    ab = jnp.dot(u, wab_ref[...].astype(BF16), preferred_element_type=F32)
    abt = lax.dot_general(wabt_ref[...].astype(BF16), u, (((1,), (1,)), ((), ())),
                          preferred_element_type=F32)
    g_col = -jnp.exp(alr_ref[...]) * jax.nn.softplus(ab + dtr_ref[...])
    g_row = -jnp.exp(alc_ref[...]) * jax.nn.softplus(abt + dtc_ref[...])
    gc = _sdot_r(_block_tri(t, GDN_CHUNK, True), g_col)
    gr = _sdot_l(g_row, _block_tri(t, GDN_CHUNK, False))
    lane = lax.broadcasted_iota(jnp.int32, ab.shape, 1)
    gb_ref[...] = jnp.where(lane < nh, gc, _sigmoid(ab))
    gr_ref[...] = gr[:SUBLANES, :]


def _gdn_in(x, mod, p):
    b, l, d = x.shape
    hk = p["hk"]
    t = ROW_TILE
    tok = lambda width: pl.BlockSpec((None, t, width), lambda i, c: (i, c, 0))
    w_arrs, w_specs = _unzip([p["wqkv"], p["wg"], p["wab"], p["wabt"], p["cw"], p["alr"], p["dtr"],
                              p["alc"], p["dtc"]])
    zero, zero_spec = _zero_operand()
    return pl.pallas_call(
        _gdn_in_kernel,
        grid=(b, l // t),
        in_specs=[zero_spec, tok(d), _mod_operand(mod)[1]] + w_specs,
        out_specs=[tok(hk), tok(hk), tok(hk), tok(hk), tok(LANES),
                   pl.BlockSpec((None, SUBLANES, t), lambda i, c: (i, 0, c)),
                   pl.BlockSpec((None, SUBLANES, 3 * hk), lambda i, c: (i, 0, 0))],
        out_shape=[jax.ShapeDtypeStruct((b, l, hk), F32)] * 4
        + [jax.ShapeDtypeStruct((b, l, LANES), F32),
           jax.ShapeDtypeStruct((b, SUBLANES, l), F32),
           jax.ShapeDtypeStruct((b, SUBLANES, 3 * hk), F32)],
        scratch_shapes=[pltpu.VMEM((DRAIN_SLOTS, t, hk), F32)],
        compiler_params=_params("parallel", "arbitrary"),
        name="gdn_in",
    )(zero, x, mod[0], *w_arrs)


def _gdn_scan_kernel(q_ref, k_ref, v_ref, gz_ref, gb_ref, gr_ref, ng_ref, y_ref, s_ref):
    c = GDN_CHUNK
    nseq, nh = s_ref.shape[0], s_ref.shape[1]

    @pl.when(pl.program_id(1) == 0)
    def _():
        s_ref[...] = jnp.zeros_like(s_ref)

    r = lax.broadcasted_iota(jnp.int32, (c, c), 0)
    cc = lax.broadcasted_iota(jnp.int32, (c, c), 1)
    causal = r >= cc
    strict = r > cc
    eye = jnp.where(r == cc, 1.0, 0.0)
    merge = []
    blk = 1
    while blk < c:
        merge.append((r // (2 * blk) == cc // (2 * blk)) & ((r // blk) % 2 == 1) & ((cc // blk) % 2 == 0))
        blk *= 2

    nchunk = gz_ref.shape[1] // c
    trip = [(b, j, h) for b in range(nseq) for j in range(nchunk) for h in range(nh)]
    rows = lambda j: slice(j * c, (j + 1) * c)
    cols = lambda h: slice(h * GDN_DK, (h + 1) * GDN_DK)
    g_col = lambda b, j, h: gb_ref[b, rows(j), h:h + 1]
    beta_of = lambda b, j, h: gb_ref[b, rows(j), nh + h:nh + h + 1]
    g_row = lambda b, j, h: gr_ref[b, h:h + 1, rows(j)]
    blk_of = lambda ref, b, j, h: ref[b, rows(j), cols(h)]

    kks = [_bdot_nt(jnp.concatenate([blk_of(k_ref, *t) * beta_of(*t), blk_of(q_ref, *t)], axis=0),
                    blk_of(k_ref, *t)) for t in trip]
    a_s, qk_s = [], []
    for t, kk in zip(trip, kks):
        decay = jnp.where(causal, jnp.exp(jnp.where(causal, g_col(*t) - g_row(*t), 0.0)), 0.0)
        a_s.append(jnp.where(strict, kk[:c] * decay, 0.0))
        qk_s.append((kk[c:] * decay).astype(BF16))
    xs = [eye - jnp.where(merge[0], a, 0.0) for a in a_s]
    for m in merge[1:]:
        ts = [_bdot(jnp.where(m, a, 0.0), x) for a, x in zip(a_s, xs)]
        xs = [x - _bdot(x, t_) for x, t_ in zip(xs, ts)]
    uws = {}
    for t, x in zip(trip, xs):
        kb = blk_of(k_ref, *t) * beta_of(*t)
        rhs = jnp.concatenate([blk_of(v_ref, *t) * beta_of(*t), kb * jnp.exp(g_col(*t))], axis=1)
        uws[t] = _bdot(x, rhs)
    qks = dict(zip(trip, qk_s))

    for j in range(nchunk):
        cur = [(b, j, h) for b in range(nseq) for h in range(nh)]
        states = {t: s_ref[t[0], t[2]] for t in cur}
        ws_qs = {t: _bdot(jnp.concatenate([uws[t][:, GDN_DV:], blk_of(q_ref, *t) * jnp.exp(g_col(*t))],
                                          axis=0), states[t]) for t in cur}
        v_news = {t: uws[t][:, :GDN_DV] - ws_qs[t][:c] for t in cur}
        outs = {t: ws_qs[t][c:] + jnp.dot(qks[t], v_news[t].astype(BF16), preferred_element_type=F32)
                for t in cur}
        for t in cur:
            g_last = g_row(*t)[:, c - 1:c]
            k_dec = blk_of(k_ref, *t) * jnp.exp(g_last - g_col(*t))
            s_ref[t[0], t[2]] = jnp.exp(g_last) * states[t] + lax.dot_general(
                k_dec.astype(BF16), v_news[t].astype(BF16), (((0,), (0,)), ((), ())),
                preferred_element_type=F32)
        for t in cur:
            o = outs[t]
            ms = jnp.mean(o * o, axis=-1, keepdims=True)
            y_ref[t[0], rows(j), cols(t[2])] = (o * lax.rsqrt(ms + EPS) * ng_ref[...]
                                                * gz_ref[t[0], rows(j), cols(t[2])]).astype(BF16)


def _gdn_scan(q, k, v, gz, gb, gr, norm_g):
    b, l, hk = gz.shape
    nh = hk // GDN_DK
    t = GDN_STEP
    ns = GDN_SEQS
    tok = lambda width: pl.BlockSpec((ns, t, width), lambda i, c: (i, c, 0))
    ng_arr, ng_spec = norm_g
    return pl.pallas_call(
        _gdn_scan_kernel,
        grid=(b // ns, l // t),
        in_specs=[tok(hk), tok(hk), tok(hk), tok(hk), tok(LANES),
                  pl.BlockSpec((ns, SUBLANES, t), lambda i, c: (i, 0, c)), ng_spec],
        out_specs=[tok(hk), pl.BlockSpec((ns, nh, GDN_DK, GDN_DV), lambda i, c: (i, 0, 0, 0))],
        out_shape=[jax.ShapeDtypeStruct((b, l, hk), BF16),
                   jax.ShapeDtypeStruct((b, nh, GDN_DK, GDN_DV), F32)],
        compiler_params=_params("parallel", "arbitrary"),
        name="gdn_scan",
    )(q, k, v, gz, gb, gr, ng_arr)


def _gdn_dec_in_kernel(z_ref, x_ref, mod_ref, wqkv_ref, wg_ref, wab_ref, cw_ref, alr_ref, dtr_ref, cs_ref,
                       qt_ref, kt_ref, v_ref, gz_ref, eg_ref, be_ref, nb_ref, q_s, k_s, p_s):
    hk = v_ref.shape[-1]
    nh = hk // GDN_DK
    u = _modnorm(x_ref[...], mod_ref[1], mod_ref[0]).astype(BF16)

    def conv_fn(pre, j):
        return _conv_step(pre, cs_ref, nb_ref, cw_ref, slice(j * hk, (j + 1) * hk), None)

    _gdn_qkvg(u, u, wqkv_ref, wg_ref, conv_fn, q_s, k_s, v_ref, gz_ref, p_s, z_ref[0])
    ab = jnp.dot(u, wab_ref[...].astype(BF16), preferred_element_type=F32)
    e_g = jnp.exp(-jnp.exp(alr_ref[...]) * jax.nn.softplus(ab + dtr_ref[...]))
    beta = _sigmoid(ab)
    n = ab.shape[0]
    for h in range(nh):
        hs = slice(h * GDN_DK, (h + 1) * GDN_DK)
        qt_ref[h] = q_s[:, hs].T
        kt_ref[h] = k_s[:, hs].T
        eg_ref[h] = jnp.broadcast_to(e_g[:, h:h + 1], (n, LANES))
        be_ref[h] = jnp.broadcast_to(beta[:, nh + h:nh + h + 1], (n, LANES))


def _state_spec(state, j):
    return pl.BlockSpec((None,) + state.shape[1:], lambda *_: (j,) + (0,) * (state.ndim - 1))


def _gdn_dec_in(x, mod, p, conv_state, j):
    n, d = x.shape
    hk = p["hk"]
    nh = hk // GDN_DK
    w_arrs, w_specs = _unzip([p["wqkv"], p["wg"], p["wab"], p["cw"], p["alr"], p["dtr"]])
    zero, zero_spec = _zero_operand()
    return pl.pallas_call(
        _gdn_dec_in_kernel,
        grid=(1,),
        in_specs=[zero_spec, _full((n, d)), _mod_operand(mod)[1]] + w_specs + [_state_spec(conv_state, j)],
        out_specs=[_full((nh, GDN_DK, n)), _full((nh, GDN_DK, n)), _full((n, hk)), _full((n, hk)),
                   _full((nh, n, LANES)), _full((nh, n, LANES)), _full((n, 3, 3 * hk))],
        out_shape=[jax.ShapeDtypeStruct((nh, GDN_DK, n), F32)] * 2
        + [jax.ShapeDtypeStruct((n, hk), F32)] * 2
        + [jax.ShapeDtypeStruct((nh, n, LANES), F32)] * 2
        + [jax.ShapeDtypeStruct((n, 3, 3 * hk), F32)],
        scratch_shapes=[pltpu.VMEM((n, hk), F32)] * 2 + [pltpu.VMEM((DRAIN_SLOTS, n, hk), F32)],
        compiler_params=_params("arbitrary"),
        name="gdn_decode_in",
    )(zero, x, mod[0], *w_arrs, conv_state)


def _gdn_dec_state_kernel(qt_ref, kt_ref, v_ref, gz_ref, eg_ref, be_ref, ng_ref, s0_ref,
                          y_ref, s_ref, o_s):
    n = s0_ref.shape[0]
    for b in range(n):
        kc = kt_ref[:, b:b + 1]
        qc = qt_ref[:, b:b + 1]
        s0 = s0_ref[b]
        k_s0 = jnp.sum(kc * s0, axis=0, keepdims=True)
        q_s0 = jnp.sum(qc * s0, axis=0, keepdims=True)
        qk = jnp.sum(qc * kc, axis=0, keepdims=True)
        e_g = eg_ref[b:b + 1, :]
        v_new = be_ref[b:b + 1, :] * (v_ref[b:b + 1, :] - e_g * k_s0)
        o_s[b:b + 1, :] = e_g * q_s0 + qk * v_new
        s_ref[b] = e_g * s0 + kc * v_new
    o = o_s[...]
    ms = jnp.mean(o * o, axis=-1, keepdims=True)
    y_ref[...] = (o * lax.rsqrt(ms + EPS) * ng_ref[...] * gz_ref[...]).astype(BF16)


def _gdn_dec_state(qt, kt, v, gz, eg, be, norm_g, s0, j):
    _, n, nh, dk, dv = s0.shape
    head = lambda a, bb: pl.BlockSpec((None, a, bb), lambda h: (h, 0, 0))
    col = pl.BlockSpec((n, dv), lambda h: (0, h))
    ng_arr, ng_spec = norm_g
    return pl.pallas_call(
        _gdn_dec_state_kernel,
        grid=(nh,),
        in_specs=[head(dk, n), head(dk, n), col, col, head(n, LANES), head(n, LANES), ng_spec,
                  pl.BlockSpec((None, n, None, dk, dv), lambda h: (j, 0, h, 0, 0))],
        out_specs=[col, pl.BlockSpec((n, None, dk, dv), lambda h: (0, h, 0, 0))],
        out_shape=[jax.ShapeDtypeStruct((n, nh * dv), BF16), jax.ShapeDtypeStruct((n, nh, dk, dv), F32)],
        scratch_shapes=[pltpu.VMEM((n, dv), F32)],
        compiler_params=_params("parallel"),
        name="gdn_decode_state",
    )(qt, kt, v, gz, eg, be, ng_arr, s0)


def _ssd_proj(u, u_z, wz_ref, wx_refs, conv_fn, zs_ref, xs_ref, bm_ref, cm_ref, buf, zero):
    gn = bm_ref.shape[-1]
    nz = zs_ref.shape[-1] // WBLK
    nx = len(wx_refs)
    order = [blk for pair in zip([("x", j) for j in range(nx)], [("z", j) for j in range(nz)] + [None] * nx)
             for blk in pair if blk is not None]

    def consume(i, pre):
        kind, j = order[i]
        cs = slice(j * WBLK, (j + 1) * WBLK)
        if kind == "z":
            zs_ref[:, cs] = _silu(pre)
            return
        act = _silu(conv_fn(pre, cs))
        if j + 1 < nx:
            xs_ref[:, cs] = act
        else:
            bm_ref[...] = act[:, :gn]
            cm_ref[...] = act[:, gn:]

    def dot_of(kind, j):
        if kind == "z":
            return jnp.dot(u_z, wz_ref[:, j * WBLK:(j + 1) * WBLK], preferred_element_type=F32)
        return jnp.dot(u, wx_refs[j][...], preferred_element_type=F32)

    _pipelined([functools.partial(dot_of, *blk) for blk in order], consume, buf, zero)


def _ssd_in_kernel(z_ref, x_ref, mod_ref, wz_ref, wx0_ref, wx1_ref, wx2_ref, wdt_ref, wdtt_ref, cw_ref,
                   cb_ref, dtr_ref, alr_ref, dtc_ref, alc_ref,
                   zs_ref, xs_ref, bm_ref, cm_ref, da_ref, at_ref, dtt_ref, tail_ref, p_s):
    t = x_ref.shape[0]
    nh = at_ref.shape[0]

    @pl.when(pl.program_id(1) == 0)
    def _():
        tail_ref[...] = jnp.zeros_like(tail_ref)

    seq = pl.program_id(0)
    u = _modnorm(x_ref[...], _mrow(mod_ref, 1, seq), _mrow(mod_ref, 0, seq)).astype(BF16)

    def conv_fn(pre, cs):
        out = _conv_seq(pre, tail_ref[:, cs], cw_ref.at[:, cs], cb_ref[:, cs])
        tail_ref[:, cs] = pre[t - SUBLANES:, :]
        return out

    _ssd_proj(u, u, wz_ref, (wx0_ref, wx1_ref, wx2_ref), conv_fn, zs_ref, xs_ref, bm_ref, cm_ref, p_s, z_ref[0])

    dt_c =jax.nn.softplus(jnp.dot(u, wdt_ref[...].astype(BF16), preferred_element_type=F32) + dtr_ref[...])
    dt_r = jax.nn.softplus(lax.dot_general(wdtt_ref[...].astype(BF16), u, (((1,), (1,)), ((), ())),
                                           preferred_element_type=F32) + dtc_ref[...])
    acs_c = _sdot_r(_block_tri(t, SSD_CHUNK, True), dt_c * -jnp.exp(alr_ref[...]))
    acs_r = _sdot_l(dt_r * -jnp.exp(alc_ref[...]), _block_tri(t, SSD_CHUNK, False))
    lane = lax.broadcasted_iota(jnp.int32, dt_c.shape, 1)
    da_ref[...] = jnp.where(lane < nh, dt_c, pltpu.roll(acs_c, nh, 1))
    at_ref[...] = acs_r[:nh, :]
    dtt_ref[...] = dt_r[:nh, :]


def _ssd_in(x, mod, p):
    b, l, d = x.shape
    di, cd, nh = p["di"], p["cd"], p["nh"]
    gn = (cd - di) // 2
    t = ROW_TILE
    tok = lambda width: pl.BlockSpec((None, t, width), lambda i, c: (i, c, 0))
    w_arrs, w_specs = _unzip([p["wz"], *p["wx"], p["wdt"], p["wdtt"], p["cw"], p["cb"], p["dtr"],
                              p["alr"], p["dtc"], p["alc"]])
    rows_spec = pl.BlockSpec((None, nh, t), lambda i, c: (i, 0, c))
    zero, zero_spec = _zero_operand()
    return pl.pallas_call(
        _ssd_in_kernel,
        grid=(b, l // t),
        in_specs=[zero_spec, tok(d), _mod_operand(mod)[1]] + w_specs,
        out_specs=[tok(di), tok(di), tok(gn), tok(gn), tok(LANES), rows_spec, rows_spec,
                   pl.BlockSpec((None, SUBLANES, cd), lambda i, c: (i, 0, 0))],
        out_shape=[jax.ShapeDtypeStruct((b, l, di), F32), jax.ShapeDtypeStruct((b, l, di), F32),
                   jax.ShapeDtypeStruct((b, l, gn), F32), jax.ShapeDtypeStruct((b, l, gn), F32),
                   jax.ShapeDtypeStruct((b, l, LANES), F32),
                   jax.ShapeDtypeStruct((b, nh, l), F32), jax.ShapeDtypeStruct((b, nh, l), F32),
                   jax.ShapeDtypeStruct((b, SUBLANES, cd), F32)],
        scratch_shapes=[pltpu.VMEM((DRAIN_SLOTS, t, WBLK), F32)],
        compiler_params=_params("parallel", "arbitrary"),
        name="ssd_in",
    )(zero, x, mod[0], *w_arrs)


def _ssd_scan_kernel(zs_ref, xs_ref, bm_ref, cm_ref, da_ref, at_ref, dtt_ref, e2_ref, dsk_ref, ng_ref,
                     y_ref, hout_ref, ht_s):
    c = zs_ref.shape[0]
    di = zs_ref.shape[1]
    n = SSM_STATE
    pdim = SSM_HEADDIM
    gw = di // SSM_GROUPS
    hpg = gw // pdim
    nh = di // pdim
    ci = pl.program_id(1)

    @pl.when(ci == 0)
    def _():
        ht_s[...] = jnp.zeros_like(ht_s)

    da = da_ref[...]
    lane_a = lax.broadcasted_iota(jnp.int32, da.shape, 1)
    is_acs = (lane_a >= nh) & (lane_a < 2 * nh)
    acs = jnp.where(is_acs, da, 0.0)
    dt_on_acs = jnp.where(is_acs, pltpu.roll(da, nh, 1), 0.0)
    eacs_e = _sdot_l(jnp.where(is_acs, jnp.exp(acs), 0.0), e2_ref[...])
    dsdt_e = _sdot_l(jnp.exp(acs[c - 1:c, :] - acs) * dt_on_acs, e2_ref[...])
    xs = xs_ref[...]
    bms = [bm_ref[:, g * n:(g + 1) * n] for g in range(SSM_GROUPS)]
    cms = [cm_ref[:, g * n:(g + 1) * n] for g in range(SSM_GROUPS)]
    xd = xs * dsdt_e

    r = lax.broadcasted_iota(jnp.int32, (c, c), 0)
    cc = lax.broadcasted_iota(jnp.int32, (c, c), 1)
    causal = r >= cc
    lane = lax.broadcasted_iota(jnp.int32, (c, 2 * pdim), 1)

    groups = range(SSM_GROUPS)
    gsl = lambda g: slice(g * gw, (g + 1) * gw)
    nsl = lambda g: slice(g * n, (g + 1) * n)
    assert n == LANES
    cbs = [_bdot_nt(cms[g], bms[g]) for g in groups]
    hts = [ht_s[g] for g in groups]
    y_offs = [_bdot(cms[g], hts[g]) * eacs_e[:, gsl(g)] for g in groups]
    for g in groups:
        ht_s[g] = eacs_e[c - 1:c, gsl(g)] * hts[g] + _bdot(bms[g].T, xd[:, gsl(g)])
    for g in groups:
        y_parts = []
        for pr in range(hpg // 2):
            ms = []
            for hh in (2 * pr, 2 * pr + 1):
                h = g * hpg + hh
                seg = da[:, nh + h:nh + h + 1] - at_ref[h:h + 1, :]
                lm = jnp.where(causal, jnp.exp(jnp.where(causal, seg, 0.0)), 0.0)
                ms.append((cbs[g] * lm * dtt_ref[h:h + 1, :]).astype(BF16))
            ps = slice(g * gw + pr * 2 * pdim, g * gw + (pr + 1) * 2 * pdim)
            xp = xs[:, ps].astype(BF16)
            zero = jnp.zeros_like(xp)
            rhs = jnp.concatenate([jnp.where(lane < pdim, xp, zero), jnp.where(lane >= pdim, xp, zero)], axis=0)
            y_d = jnp.dot(jnp.concatenate(ms, axis=1), rhs, preferred_element_type=F32)
            os_ = slice(pr * 2 * pdim, (pr + 1) * 2 * pdim)
            y_parts.append(y_d + y_offs[g][:, os_] + dsk_ref[:, ps] * xs[:, ps])
        yz = jnp.concatenate(y_parts, axis=1) * zs_ref[:, gsl(g)]
        ms_ = jnp.mean(yz * yz, axis=-1, keepdims=True)
        y_ref[:, gsl(g)] = (yz * lax.rsqrt(ms_ + EPS) * ng_ref[:, gsl(g)]).astype(BF16)

    @pl.when(ci == pl.num_programs(1) - 1)
    def _():
        for g in groups:
            hout_ref[gsl(g), :] = ht_s[g].T


def _ssd_scan(zs, xs, bm, cm, da, at, dtt, p):
    b, l, di = zs.shape
    gn = bm.shape[-1]
    nh = p["nh"]
    gw = di // SSM_GROUPS
    t = SSD_CHUNK
    tok = lambda width: pl.BlockSpec((None, t, width), lambda i, c: (i, c, 0))
    rows_spec = pl.BlockSpec((None, nh, t), lambda i, c: (i, 0, c))
    w_arrs, w_specs = _unzip([p["e2"], p["dsk"], p["ng"]])
    return pl.pallas_call(
        _ssd_scan_kernel,
        grid=(b, l // t),
        in_specs=[tok(di), tok(di), tok(gn), tok(gn), tok(LANES), rows_spec, rows_spec] + w_specs,
        out_specs=[tok(di), pl.BlockSpec((None, di, SSM_STATE), lambda i, c: (i, 0, 0))],
        out_shape=[jax.ShapeDtypeStruct((b, l, di), BF16),
                   jax.ShapeDtypeStruct((b, di, SSM_STATE), F32)],
        scratch_shapes=[pltpu.VMEM((SSM_GROUPS, SSM_STATE, gw), F32)],
        compiler_params=_params("parallel", "arbitrary"),
        name="ssd_scan",
    )(zs, xs, bm, cm, da, at, dtt, *w_arrs)


def _ssd_dec_in_kernel(z_ref, x_ref, mod_ref, wz_ref, wx0_ref, wx1_ref, wx2_ref, wdt_ref, cw_ref, cb_ref,
                       dtr_ref, alr_ref, e1_ref, cs_ref,
                       zs_ref, xs_ref, bm_ref, cm_ref, xt_ref, ea_ref, xdt_ref, eae_ref, nb_ref, p_s):
    n = x_ref.shape[0]
    di = zs_ref.shape[-1]
    nh = di // SSM_HEADDIM
    u = _modnorm(x_ref[...], mod_ref[1], mod_ref[0]).astype(BF16)

    def conv_fn(pre, cs):
        return _conv_step(pre, cs_ref, nb_ref, cw_ref, cs, cb_ref[:, cs])

    _ssd_proj(u, u, wz_ref, (wx0_ref, wx1_ref, wx2_ref), conv_fn, zs_ref, xs_ref, bm_ref, cm_ref, p_s, z_ref[0])
    dt = jax.nn.softplus(jnp.dot(u, wdt_ref[...].astype(BF16), preferred_element_type=F32) + dtr_ref[...])
    e_a = jnp.exp(dt * -jnp.exp(alr_ref[...]))
    xdt_ref[...] = xs_ref[...] * _sdot_l(dt, e1_ref[...])
    eae_ref[...] = _sdot_l(e_a, e1_ref[...])
    for j in range(di // LANES):
        xt_ref[j * LANES:(j + 1) * LANES, :] = xdt_ref[:, j * LANES:(j + 1) * LANES].T
    for h in range(nh):
        ea_ref[h] = jnp.broadcast_to(e_a[:, h:h + 1], (n, LANES))


def _ssd_dec_in(x, mod, p, conv_state, j):
    n, d = x.shape
    di, cd, nh = p["di"], p["cd"], p["nh"]
    gn = (cd - di) // 2
    w_arrs, w_specs = _unzip([p["wz"], *p["wx"], p["wdt"], p["cw"], p["cb"], p["dtr"], p["alr"], p["e1"]])
    zero, zero_spec = _zero_operand()
    return pl.pallas_call(
        _ssd_dec_in_kernel,
        grid=(1,),
        in_specs=[zero_spec, _full((n, d)), _mod_operand(mod)[1]] + w_specs + [_state_spec(conv_state, j)],
        out_specs=[_full((n, di)), _full((n, di)), _full((n, gn)), _full((n, gn)), _full((di, n)),
                   _full((nh, n, LANES)), _full((n, di)), _full((n, di)), _full((n, 3, cd))],
        out_shape=[jax.ShapeDtypeStruct((n, di), F32), jax.ShapeDtypeStruct((n, di), F32),
                   jax.ShapeDtypeStruct((n, gn), F32), jax.ShapeDtypeStruct((n, gn), F32),
                   jax.ShapeDtypeStruct((di, n), F32),
                   jax.ShapeDtypeStruct((nh, n, LANES), F32),
                   jax.ShapeDtypeStruct((n, di), F32), jax.ShapeDtypeStruct((n, di), F32),
                   jax.ShapeDtypeStruct((n, 3, cd), F32)],
        scratch_shapes=[pltpu.VMEM((DRAIN_SLOTS, n, WBLK), F32)],
        compiler_params=_params("arbitrary"),
        name="ssd_decode_in",
    )(zero, x, mod[0], *w_arrs, conv_state)


def _ssd_dec_state_kernel(xt_ref, bm_ref, cm_ref, ea_ref, h0_ref, yo_ref, h_ref):
    n = h0_ref.shape[0]
    pdim = h0_ref.shape[2]
    for b in range(n):
        bb = bm_ref[b:b + 1, :]
        c8 = jnp.broadcast_to(cm_ref[b:b + 1, :], (SUBLANES, bb.shape[1]))
        outs = []
        for hh in range(2):
            h0 = h0_ref[b, hh]
            outs.append(_bdot_nt(c8, h0)[0:1, :])
            xc = xt_ref[hh * pdim:(hh + 1) * pdim, b:b + 1]
            h_ref[b, hh] = ea_ref[hh, b:b + 1, :] * h0 + xc * bb
        yo_ref[b:b + 1, :] = jnp.concatenate(outs, axis=1)


def _ssd_dec_state(xt, bm, cm, ea, h0, j):
    _, n, nh, pdim, ns = h0.shape
    hpg = nh // SSM_GROUPS
    return pl.pallas_call(
        _ssd_dec_state_kernel,
        grid=(nh // 2,),
        in_specs=[pl.BlockSpec((2 * pdim, n), lambda i: (i, 0)),
                  pl.BlockSpec((n, ns), lambda i: (0, (2 * i) // hpg)),
                  pl.BlockSpec((n, ns), lambda i: (0, (2 * i) // hpg)),
                  pl.BlockSpec((2, n, LANES), lambda i: (i, 0, 0)),
                  pl.BlockSpec((None, n, 2, pdim, ns), lambda i: (j, 0, i, 0, 0))],
        out_specs=[pl.BlockSpec((n, 2 * pdim), lambda i: (0, i)),
                   pl.BlockSpec((n, 2, pdim, ns), lambda i: (0, i, 0, 0))],
        out_shape=[jax.ShapeDtypeStruct((n, nh * pdim), F32), jax.ShapeDtypeStruct((n, nh, pdim, ns), F32)],
        compiler_params=_params("parallel"),
        name="ssd_decode_state",
    )(xt, bm, cm, ea, h0)


def _ssd_dec_out_kernel(yo_ref, xs_ref, bm_ref, cm_ref, zs_ref, eae_ref, xdt_ref, dsk_ref, ng_ref, y_ref):
    di = xs_ref.shape[1]
    gw = di // SSM_GROUPS
    n = SSM_STATE
    for g in range(SSM_GROUPS):
        gs = slice(g * gw, (g + 1) * gw)
        cb = jnp.sum(cm_ref[:, g * n:(g + 1) * n] * bm_ref[:, g * n:(g + 1) * n], axis=-1, keepdims=True)
        y = cb * xdt_ref[:, gs] + eae_ref[:, gs] * yo_ref[:, gs] + dsk_ref[:, gs] * xs_ref[:, gs]
        yz = y * zs_ref[:, gs]
        ms = jnp.mean(yz * yz, axis=-1, keepdims=True)
        y_ref[:, gs] = (yz * lax.rsqrt(ms + EPS) * ng_ref[:, gs]).astype(BF16)


def _ssd_dec_out(yo, xs, bm, cm, zs, eae, xdt, p):
    n, di = xs.shape
    gn = bm.shape[1]
    w_arrs, w_specs = _unzip([p["dsk"], p["ng"]])
    return pl.pallas_call(
        _ssd_dec_out_kernel,
        grid=(1,),
        in_specs=[_full((n, di)), _full((n, di)), _full((n, gn)), _full((n, gn)), _full((n, di)),
                  _full((n, di)), _full((n, di))] + w_specs,
        out_specs=_full((n, di)),
        out_shape=jax.ShapeDtypeStruct((n, di), BF16),
        compiler_params=_params("arbitrary"),
        name="ssd_decode_out",
    )(yo, xs, bm, cm, zs, eae, xdt, *w_arrs)


def _pad_lanes(v):
    return jnp.pad(v.astype(F32), (0, LANES - v.shape[0])).reshape(1, LANES)


def _pad_cols(w):
    return jnp.pad(w, ((0, 0), (0, LANES - w.shape[1])))


def _small(a):
    return _w(a, a.shape, (0,) * a.ndim)


def _rg_params(j, w_in, conv_w, conv_b, gate_w, gate_b, lam, w_out):
    nl, d, w2 = w_in.shape
    w = w2 // 2
    nblk = w // RG_BLOCK
    gate = (None, None, nblk, RG_BLOCK, RG_BLOCK)
    return dict(width=w,
                wy=_w(w_in, (None, d, w), (j, 0, 0)), wx=_w(w_in, (None, d, w), (j, 0, 1)),
                cw=_w(conv_w, (None, 4, w), (j, 0, 0)), cb=_w(conv_b.reshape(nl, 1, w), (None, 1, w), (j, 0, 0)),
                gwr=_w(gate_w, gate, (j, 0, 0, 0, 0)), gwi=_w(gate_w, gate, (j, 1, 0, 0, 0)),
                gb=_w(gate_b, (None, 2, w), (j, 0, 0)), lam=_w(lam.reshape(nl, 1, w), (None, 1, w), (j, 0, 0)),
                wo=_w(w_out, (None, w, d), (j, 0, 0)))


def _gdn_params(j, w_in, w_in_f32, conv_w, a_log, dt_bias, norm_g, w_out):
    nl, d, _ = w_in.shape
    qkv = conv_w.shape[-1]
    hk = qkv // 3
    wab = _pad_cols(w_in_f32[j, :, qkv + hk:])
    alr, dtr = _pad_lanes(a_log[j]), _pad_lanes(dt_bias[j])
    return dict(hk=hk,
                wqkv=_w(w_in, (None, d, qkv), (j, 0, 0)), wg=_w(w_in, (None, d, hk), (j, 0, qkv // hk)),
                wab=_small(wab), wabt=_small(wab.T), cw=_w(conv_w, (None, 4, qkv), (j, 0, 0)),
                alr=_small(alr), dtr=_small(dtr), alc=_small(alr.T), dtc=_small(dtr.T),
                ng=_w(norm_g.reshape(nl, 1, GDN_DV), (None, 1, GDN_DV), (j, 0, 0)),
                wo=_w(w_out, (None, hk, d), (j, 0, 0)))


def _ssd_params(j, w_in, w_in_f32, conv_w, conv_b, a_log, dt_bias, d_skip, norm_g, w_out):
    nl, d, _ = w_in.shape
    cd = conv_w.shape[-1]
    nh = a_log.shape[-1]
    di = nh * SSM_HEADDIM
    assert di % WBLK == 0 and (cd - di) == WBLK
    wdt = _pad_cols(w_in_f32[j, :, di + cd:])
    alr, dtr = _pad_lanes(a_log[j]), _pad_lanes(dt_bias[j])
    head_of = jnp.arange(di, dtype=jnp.int32) // SSM_HEADDIM
    rows = jnp.arange(LANES, dtype=jnp.int32)[:, None]
    e1 = (rows == head_of[None, :]).astype(BF16)
    e2 = (rows == head_of[None, :] + nh).astype(BF16)
    return dict(di=di, cd=cd, nh=nh,
                wz=_w(w_in, (None, d, di), (j, 0, 0)),
                wx=[_w(w_in, (None, d, WBLK), (j, 0, di // WBLK + i)) for i in range(cd // WBLK)],
                wdt=_small(wdt), wdtt=_small(wdt.T), cw=_w(conv_w, (None, 4, cd), (j, 0, 0)),
                cb=_w(conv_b.reshape(nl, 1, cd), (None, 1, cd), (j, 0, 0)),
                alr=_small(alr), dtr=_small(dtr), alc=_small(alr.T), dtc=_small(dtr.T),
                e1=_small(e1), e2=_small(e2),
                dsk=_small(jnp.repeat(d_skip[j], SSM_HEADDIM).reshape(1, di)),
                ng=_w(norm_g.reshape(nl, 1, di), (None, 1, di), (j, 0, 0)),
                wo=_w(w_out, (None, di, d), (j, 0, 0)))


def kernel(x_prompt, x_sample, state_rglru_conv, state_rglru_h, state_gdn_conv, state_gdn_S, state_ssd_conv, state_ssd_h, c_prompt, c_sample, w_mod, b_mod, w_mlp_up, w_mlp_down, final_norm_g, rg_w_in, rg_conv_w, rg_conv_b, rg_gate_w, rg_gate_b, rg_lambda, rg_w_out, gdn_w_in, gdn_conv_w, gdn_A_log, gdn_dt_bias, gdn_norm_g, gdn_w_out, ssd_w_in, ssd_conv_w, ssd_conv_b, ssd_A_log, ssd_dt_bias, ssd_D, ssd_norm_g, ssd_w_out):
    bp, l, d = x_prompt.shape
    ns = x_sample.shape[0]
    hid = w_mlp_up.shape[-1]
    assert x_sample.shape[1] == 1 and l % ROW_TILE == 0 and ns % SUBLANES == 0 and bp % GDN_SEQS == 0

    assert ns % bp == 0 and bp % SUBLANES == 0
    mod = _modulation(jnp.concatenate([c_sample, c_prompt], axis=0), w_mod, b_mod)
    mod_p = [(mod, i, bp, ns) for i in range(DEPTH)]
    mod_s = [(mod, i, ns, 0) for i in range(DEPTH)]
    fg = _small(final_norm_g.reshape(1, d))

    w_up, w_down = w_mlp_up.astype(BF16), w_mlp_down.astype(BF16)
    rg_in, rg_gate, rg_out = rg_w_in.astype(BF16), rg_gate_w.astype(BF16), rg_w_out.astype(BF16)
    gdn_wide = gdn_conv_w.shape[-1] + gdn_w_out.shape[1]
    ssd_wide = ssd_conv_w.shape[-1] + ssd_w_out.shape[1]
    gdn_in, gdn_out = gdn_w_in[:, :, :gdn_wide].astype(BF16), gdn_w_out.astype(BF16)
    ssd_in, ssd_out = ssd_w_in[:, :, :ssd_wide].astype(BF16), ssd_w_out.astype(BF16)

    xp = x_prompt
    xs = x_sample.reshape(ns, d)
    tails = lambda t: t[:, SUBLANES - 3:, :]
    tails_interleaved = lambda t: t[:, SUBLANES - 1::SUBLANES, :]
    out = {k: [] for k in ("p_rg_conv", "p_rg_h", "p_gdn_conv", "p_gdn_S", "p_ssd_conv", "p_ssd_h",
                           "s_rg_conv", "s_rg_h", "s_gdn_conv", "s_gdn_S", "s_ssd_conv", "s_ssd_h")}
    for i in range(DEPTH):
        j = i // N_MIXERS
        kind = i % N_MIXERS
        if kind == 0:
            p = _rg_params(j, rg_in, rg_conv_w, rg_conv_b, rg_gate, rg_gate_b, rg_lambda, rg_out)
            yp, tail, h_last = _rg_prompt(xp, mod_p[i], p)
            out["p_rg_conv"].append(tails_interleaved(tail))
            out["p_rg_h"].append(h_last[:, 0, :])
            ysm, nb, h_new = _rg_decode(xs, mod_s[i], p, state_rglru_conv, state_rglru_h, j)
            out["s_rg_conv"].append(nb)
            out["s_rg_h"].append(h_new)
        elif kind == 1:
            p = _gdn_params(j, gdn_in, gdn_w_in, gdn_conv_w, gdn_A_log, gdn_dt_bias, gdn_norm_g, gdn_out)
            q, k, v, gz, gb, gr, tail = _gdn_in(xp, mod_p[i], p)
            yp, s_fin = _gdn_scan(q, k, v, gz, gb, gr, p["ng"])
            out["p_gdn_conv"].append(tails(tail))
            out["p_gdn_S"].append(s_fin)
            qt, kt, v1, gz1, eg, be, nb = _gdn_dec_in(xs, mod_s[i], p, state_gdn_conv, j)
            ysm, s_new = _gdn_dec_state(qt, kt, v1, gz1, eg, be, p["ng"], state_gdn_S, j)
            out["s_gdn_conv"].append(nb)
            out["s_gdn_S"].append(s_new)
        else:
            p = _ssd_params(j, ssd_in, ssd_w_in, ssd_conv_w, ssd_conv_b, ssd_A_log, ssd_dt_bias, ssd_D, ssd_norm_g,
                            ssd_out)
            zs, xc, bm, cm, da, at, dtt, tail = _ssd_in(xp, mod_p[i], p)
            yp, h_fin = _ssd_scan(zs, xc, bm, cm, da, at, dtt, p)
            out["p_ssd_conv"].append(tails(tail))
            out["p_ssd_h"].append(h_fin.reshape(bp, -1, SSM_HEADDIM, SSM_STATE))
            zs1, xc1, bm1, cm1, xt, ea, xdt, eae, nb = _ssd_dec_in(xs, mod_s[i], p, state_ssd_conv, j)
            yo, h_new = _ssd_dec_state(xt, bm1, cm1, ea, state_ssd_h, j)
            ysm = _ssd_dec_out(yo, xc1, bm1, cm1, zs1, eae, xdt, p)
            out["s_ssd_conv"].append(nb)
            out["s_ssd_h"].append(h_new)
        final = i == DEPTH - 1
        weights = [p["wo"], _w(w_up, (None, d, hid), (i, 0, 0)), _w(w_down, (None, hid, d), (i, 0, 0)), fg]
        y_rows = yp if yp.ndim == 4 else yp.reshape(bp * l, -1)
        xp = _outproj_mlp(xp.reshape(bp * l, d), y_rows, mod_p[i], weights,
                          tm=ROW_TILE, rows_per_mod=l, final=final).reshape(bp, l, d)
        xs = _outproj_mlp(xs, ysm, mod_s[i], weights, tm=ns, rows_per_mod=1, final=final)

    st = {k: jnp.stack(v) for k, v in out.items()}
    return (xp, xs.reshape(ns, 1, d),
            st["p_rg_conv"], st["p_rg_h"], st["p_gdn_conv"], st["p_gdn_S"], st["p_ssd_conv"], st["p_ssd_h"],
            st["s_rg_conv"], st["s_rg_h"], st["s_gdn_conv"], st["s_gdn_S"], st["s_ssd_conv"], st["s_ssd_h"])
```

```python
import functools

import jax
import jax.numpy as jnp
from jax import lax
from jax.experimental import pallas as pl
from jax.experimental.pallas import tpu as pltpu

F32 = jnp.float32
BF16 = jnp.bfloat16

DEPTH = 4
N_MIXERS = 3
EPS = 1e-6
RG_C = 8.0
RG_BLOCK = 256
GDN_DK = 128
GDN_DV = 128
SSM_HEADDIM = 64
SSM_STATE = 128
SSM_GROUPS = 4

SUBLANES = 8
LANES = 128

VMEM_LIMIT = 56 * 1024 * 1024
ROW_TILE = 512
GDN_CHUNK = 64
GDN_STEP = 128
GDN_SEQS = 1
SSD_CHUNK = 128
MLP_HCHUNK = 1024
WBLK = 1024
TAIL_ROWS = 3 * SUBLANES


def _params(*sem):
    return pltpu.CompilerParams(dimension_semantics=sem, vmem_limit_bytes=VMEM_LIMIT)


def _w(arr, blk, idx):
    return arr, pl.BlockSpec(blk, lambda *_: idx, pipeline_mode=pl.Buffered(1))


def _unzip(pairs):
    return [a for a, _ in pairs], [s for _, s in pairs]


def _bdot(a, b):
    return jnp.dot(a.astype(BF16), b.astype(BF16), preferred_element_type=F32)


def _bdot_nt(a, b):
    return lax.dot_general(a.astype(BF16), b.astype(BF16), (((1,), (1,)), ((), ())),
                           preferred_element_type=F32)


def _split(a):
    hi = a.astype(BF16)
    lo = (a - hi.astype(F32)).astype(BF16)
    return hi, lo


def _sdot_l(a, b_exact):
    hi, lo = _split(a)
    return (jnp.dot(hi, b_exact, preferred_element_type=F32)
            + jnp.dot(lo, b_exact, preferred_element_type=F32))


def _sdot_r(a_exact, b):
    hi, lo = _split(b)
    return (jnp.dot(a_exact, hi, preferred_element_type=F32)
            + jnp.dot(a_exact, lo, preferred_element_type=F32))


def _modnorm(x, sc, sh):
    ms = jnp.mean(x * x, axis=-1, keepdims=True)
    return x * lax.rsqrt(ms + EPS) * (1.0 + sc) + sh


def _sigmoid(x):
    return 0.5 * jnp.tanh(0.5 * x) + 0.5


def _silu(x):
    hx = 0.5 * x
    return hx * jnp.tanh(hx) + hx


def _conv_seq(xpre, tail, w_ref, bias):
    xcat = jnp.concatenate([tail, xpre], axis=0)
    acc = xpre * w_ref[3:4, :]
    for k in (1, 2, 3):
        shifted = pltpu.roll(xcat, k, 0)[SUBLANES:]
        acc = acc + shifted * w_ref[3 - k:4 - k, :]
    if bias is not None:
        acc = acc + bias
    return acc


def _interleave_rows(u_t):
    t = u_t.shape[0]
    r = lax.broadcasted_iota(jnp.int32, (t, t), 0)
    c = lax.broadcasted_iota(jnp.int32, (t, t), 1)
    perm = jnp.where(c == (r % SUBLANES) * (t // SUBLANES) + r // SUBLANES, 1.0, 0.0).astype(BF16)
    return jnp.dot(perm, u_t, preferred_element_type=F32).astype(BF16)


def _conv_interleaved(pre, tail_ref, cs, w_ref, bias):
    ng = pre.shape[0] // SUBLANES
    ntail = tail_ref.shape[0] // SUBLANES
    sub = lax.broadcasted_iota(jnp.int32, (SUBLANES, pre.shape[1]), 0)
    grp = lambda a, g: a[g * SUBLANES:(g + 1) * SUBLANES]
    acc = pre * w_ref[3:4, cs]
    if bias is not None:
        acc = acc + bias
    for k in (1, 2, 3):
        top = [jnp.where(sub == 0,
                         pltpu.roll(tail_ref[(ntail - k + i) * SUBLANES:(ntail - k + i + 1) * SUBLANES, cs], 1, 0),
                         pltpu.roll(grp(pre, ng - k + i), 1, 0)) for i in range(k)]
        shifted = jnp.concatenate(top + [pre[:(ng - k) * SUBLANES]], axis=0)
        acc = acc + shifted * w_ref[3 - k:4 - k, cs]
    tail_ref[:, cs] = pre[(ng - ntail) * SUBLANES:, :]
    return acc


def _conv_step(pre, cs_ref, nb_ref, w_ref, cs, bias):
    nb_ref[:, 0, cs] = cs_ref[:, 1, cs]
    nb_ref[:, 1, cs] = cs_ref[:, 2, cs]
    nb_ref[:, 2, cs] = pre
    out = (cs_ref[:, 0, cs] * w_ref[0:1, cs] + cs_ref[:, 1, cs] * w_ref[1:2, cs]
           + cs_ref[:, 2, cs] * w_ref[2:3, cs] + pre * w_ref[3:4, cs])
    return out if bias is None else out + bias


DRAIN_SLOTS = 3
DRAIN_AHEAD = 2


def _zero_operand():
    return jnp.zeros((1,), jnp.int32), pl.BlockSpec(memory_space=pltpu.SMEM)


def _pipelined(dots, consume, buf, zero, after_first=None):
    n = len(dots)
    for j in range(min(DRAIN_AHEAD, n)):
        buf[zero + j % DRAIN_SLOTS] = dots[j]()
    for j in range(n):
        if j + DRAIN_AHEAD < n:
            buf[zero + (j + DRAIN_AHEAD) % DRAIN_SLOTS] = dots[j + DRAIN_AHEAD]()
        consume(j, buf[zero + j % DRAIN_SLOTS])
        if j == 0 and after_first is not None:
            after_first()


def _block_tri(n, chunk, lower):
    r = lax.broadcasted_iota(jnp.int32, (n, n), 0)
    c = lax.broadcasted_iota(jnp.int32, (n, n), 1)
    same = (r // chunk) == (c // chunk)
    tri = (c <= r) if lower else (r <= c)
    return jnp.where(same & tri, 1.0, 0.0).astype(BF16)


def _mod_kernel(c_ref, w_ref, b_ref, o_ref):
    o_ref[...] = _bdot(_silu(c_ref[...]), w_ref[...]) + b_ref[...]


def _modulation(c_all, w_mod, b_mod):
    nb, d = c_all.shape
    n6 = w_mod.shape[-1]
    return pl.pallas_call(
        _mod_kernel,
        grid=(DEPTH, n6 // d),
        in_specs=[pl.BlockSpec((nb, d), lambda l, n: (0, 0)),
                  pl.BlockSpec((None, d, d), lambda l, n: (l, 0, n)),
                  pl.BlockSpec((None, 1, d), lambda l, n: (l, 0, n))],
        out_specs=pl.BlockSpec((None, None, nb, d), lambda l, n: (l, n, 0, 0)),
        out_shape=jax.ShapeDtypeStruct((DEPTH, n6 // d, nb, d), F32),
        compiler_params=_params("parallel", "parallel"),
        name="modulation",
    )(c_all, w_mod, b_mod.reshape(DEPTH, 1, n6))


def _mod_operand(mod):
    arr, layer, n_rows, first_row = mod
    return arr, pl.BlockSpec((None, arr.shape[1], n_rows, arr.shape[3]),
                             lambda *_: (layer, 0, first_row // n_rows, 0))


def _mrow(mod_ref, k, row):
    return mod_ref[k] if row is None else mod_ref[k, pl.ds(row, 1), :]


def _time_rows(blk_ref):
    ncb, t, _ = blk_ref.shape
    ng = t // SUBLANES
    return jnp.concatenate(
        [jnp.concatenate([blk_ref[cb, pl.ds(s, ng, stride=SUBLANES), :] for cb in range(ncb)], axis=1)
         for s in range(SUBLANES)], axis=0)


def _outproj_mlp_kernel(x_ref, y_ref, mod_ref, wo_ref, wu_ref, wd_ref, fg_ref, o_ref, *, final, tiles_per_seq):
    row = None if tiles_per_seq is None else pl.program_id(0) // tiles_per_seq
    y = y_ref[...] if len(y_ref.shape) == 2 else _time_rows(y_ref).astype(BF16)
    x1 = x_ref[...] + _mrow(mod_ref, 2, row) * jnp.dot(y, wo_ref[...], preferred_element_type=F32)
    u = _modnorm(x1, _mrow(mod_ref, 4, row), _mrow(mod_ref, 3, row)).astype(BF16)
    hidden = wu_ref.shape[1]
    acc = None
    for c in range(hidden // MLP_HCHUNK):
        cs = slice(c * MLP_HCHUNK, (c + 1) * MLP_HCHUNK)
        h = jnp.maximum(jnp.dot(u, wu_ref[:, cs].astype(BF16), preferred_element_type=F32), 0.0)
        p = jnp.dot((h * h).astype(BF16), wd_ref[cs, :].astype(BF16), preferred_element_type=F32)
        acc = p if acc is None else acc + p
    x2 = x1 + _mrow(mod_ref, 5, row) * acc
    if final:
        ms = jnp.mean(x2 * x2, axis=-1, keepdims=True)
        x2 = x2 * lax.rsqrt(ms + EPS) * fg_ref[...]
    o_ref[...] = x2


def _outproj_mlp(x, y, mod, weights, *, tm, rows_per_mod, final):
    n, d = x.shape
    per = None if rows_per_mod == 1 else rows_per_mod // tm
    assert per is not None or n == tm
    mod_arr, mod_spec = _mod_operand(mod)
    if y.ndim == 2:
        y_spec = pl.BlockSpec((tm, y.shape[1]), lambda i: (i, 0))
    else:
        assert tm == ROW_TILE and rows_per_mod == y.shape[2]
        y_spec = pl.BlockSpec((None, y.shape[1], tm, LANES), lambda i: (i // per, 0, i % per, 0))
    w_arrs, w_specs = _unzip(weights)
    return pl.pallas_call(
        functools.partial(_outproj_mlp_kernel, final=final, tiles_per_seq=per),
        grid=(n // tm,),
        in_specs=[pl.BlockSpec((tm, d), lambda i: (i, 0)), y_spec, mod_spec] + w_specs,
        out_specs=pl.BlockSpec((tm, d), lambda i: (i, 0)),
        out_shape=jax.ShapeDtypeStruct((n, d), F32),
        compiler_params=_params("parallel"),
        name="outproj_mlp",
    )(x, y, mod_arr, *w_arrs)


def _rg_gate_block(n, xbr, gwr_ref, gwi_ref, gb_ref, logsig, a_s, b_s):
    sl = slice(n * RG_BLOCK, (n + 1) * RG_BLOCK)
    xb = xbr.astype(BF16)
    gr = jnp.dot(xb, gwr_ref[n], preferred_element_type=F32)
    gi = jnp.dot(xb, gwi_ref[n], preferred_element_type=F32)
    log_a = (RG_C * logsig[:, sl]) * _sigmoid(gr + gb_ref[0:1, sl])
    a_s[:, sl] = jnp.exp(log_a)
    th = jnp.tanh(log_a)
    z = -2.0 * th / (1.0 - th)
    root = jnp.where(z > 0.0, z * lax.rsqrt(z), 0.0)
    b_s[:, sl] = root * _sigmoid(gi + gb_ref[1:2, sl]) * xbr


def _rg_gates(xbr, gwr_ref, gwi_ref, gb_ref, logsig, a_s, b_s):
    for n in range(xbr.shape[1] // RG_BLOCK):
        _rg_gate_block(n, xbr[:, n * RG_BLOCK:(n + 1) * RG_BLOCK], gwr_ref, gwi_ref, gb_ref, logsig, a_s, b_s)


def _rg_prompt_kernel(z_ref, x_ref, mod_ref, wy_ref, wx_ref, cw_ref, cb_ref, gwr_ref, gwi_ref, gb_ref,
                      lam_ref, y_ref, tail_ref, h_ref, a_s, b_s, y_s, p_s):
    t, w = a_s.shape
    ng = t // SUBLANES
    ncb = w // LANES

    @pl.when(pl.program_id(1) == 0)
    def _():
        tail_ref[...] = jnp.zeros_like(tail_ref)
        h_ref[...] = jnp.zeros_like(h_ref)

    seq = pl.program_id(0)
    u_t = _modnorm(x_ref[...], _mrow(mod_ref, 1, seq), _mrow(mod_ref, 0, seq)).astype(BF16)
    u = _interleave_rows(u_t)
    logsig = jax.nn.log_sigmoid(lam_ref[...])
    nblk = w // RG_BLOCK
    order = [(kind, n) for n in range(nblk) for kind in ("x", "y")]

    def consume(i, pre):
        kind, n = order[i]
        cs = slice(n * RG_BLOCK, (n + 1) * RG_BLOCK)
        if kind == "y":
            y_s[:, cs] = jax.nn.gelu(pre)
            return
        acc = _conv_interleaved(pre, tail_ref, cs, cw_ref, cb_ref[:, cs])
        _rg_gate_block(n, acc, gwr_ref, gwi_ref, gb_ref, logsig, a_s, b_s)

    def dot_of(kind, n):
        w_ref = wx_ref if kind == "x" else wy_ref
        return jnp.dot(u, w_ref[:, n * RG_BLOCK:(n + 1) * RG_BLOCK], preferred_element_type=F32)

    _pipelined([functools.partial(dot_of, *blk) for blk in order], consume, p_s, z_ref[0])

    def compose(gi, carry):
        ca, cbb = carry
        r0 = pl.multiple_of(gi * SUBLANES, SUBLANES)
        a = a_s[pl.ds(r0, SUBLANES), :]
        ca = a * ca
        cbb = a * cbb + b_s[pl.ds(r0, SUBLANES), :]
        a_s[pl.ds(r0, SUBLANES), :] = ca
        b_s[pl.ds(r0, SUBLANES), :] = cbb
        return ca, cbb

    a_end, b_end = lax.fori_loop(0, ng, compose, (jnp.ones((SUBLANES, w), F32), jnp.zeros((SUBLANES, w), F32)),
                                 unroll=4)
    h_in = jnp.zeros((SUBLANES, w), F32)
    for _ in range(SUBLANES):
        h_in = jnp.where(lax.broadcasted_iota(jnp.int32, (SUBLANES, w), 0) == 0, h_ref[...],
                         pltpu.roll(a_end * h_in + b_end, 1, 0))
    h_ref[...] = (a_end * h_in + b_end)[SUBLANES - 1:, :]
    hy = (a_s[...] * jnp.concatenate([h_in] * ng, axis=0) + b_s[...]) * y_s[...]
    for cb in range(ncb):
        y_ref[cb] = hy[:, cb * LANES:(cb + 1) * LANES]


def _rg_weights(p):
    return [p["wy"], p["wx"], p["cw"], p["cb"], p["gwr"], p["gwi"], p["gb"], p["lam"]]


def _rg_prompt(x, mod, p):
    b, l, d = x.shape
    w = p["width"]
    t = ROW_TILE
    w_arrs, w_specs = _unzip(_rg_weights(p))
    ncb = w // LANES
    ntail = (4 - 1) * SUBLANES
    zero, zero_spec = _zero_operand()
    return pl.pallas_call(
        _rg_prompt_kernel,
        grid=(b, l // t),
        in_specs=[zero_spec, pl.BlockSpec((None, t, d), lambda i, c: (i, c, 0)),
                  _mod_operand(mod)[1]] + w_specs,
        out_specs=[pl.BlockSpec((None, ncb, t, LANES), lambda i, c: (i, 0, c, 0)),
                   pl.BlockSpec((None, ntail, w), lambda i, c: (i, 0, 0)),
                   pl.BlockSpec((None, 1, w), lambda i, c: (i, 0, 0))],
        out_shape=[jax.ShapeDtypeStruct((b, ncb, l, LANES), F32),
                   jax.ShapeDtypeStruct((b, ntail, w), F32),
                   jax.ShapeDtypeStruct((b, 1, w), F32)],
        scratch_shapes=[pltpu.VMEM((t, w), F32)] * 3 + [pltpu.VMEM((DRAIN_SLOTS, t, RG_BLOCK), F32)],
        compiler_params=_params("parallel", "arbitrary"),
        name="rglru_prompt",
    )(zero, x, mod[0], *w_arrs)


def _rg_decode_kernel(x_ref, mod_ref, wy_ref, wx_ref, cw_ref, cb_ref, gwr_ref, gwi_ref, gb_ref,
                      lam_ref, cs_ref, h0_ref, y_ref, nb_ref, h_ref, a_s, b_s):
    u = _modnorm(x_ref[...], mod_ref[1], mod_ref[0]).astype(BF16)
    xpre = jnp.dot(u, wx_ref[...], preferred_element_type=F32)
    xbr = _conv_step(xpre, cs_ref, nb_ref, cw_ref, slice(None), cb_ref[...])
    _rg_gates(xbr, gwr_ref, gwi_ref, gb_ref, jax.nn.log_sigmoid(lam_ref[...]), a_s, b_s)
    h = a_s[...] * h0_ref[...] + b_s[...]
    h_ref[...] = h
    y_br = jax.nn.gelu(jnp.dot(u, wy_ref[...], preferred_element_type=F32))
    y_ref[...] = (h * y_br).astype(BF16)


def _full(shape):
    return pl.BlockSpec(shape, lambda *_: (0,) * len(shape))


def _rg_decode(x, mod, p, conv_state, h0, j):
    n, d = x.shape
    w = p["width"]
    w_arrs, w_specs = _unzip(_rg_weights(p))
    return pl.pallas_call(
        _rg_decode_kernel,
        grid=(1,),
        in_specs=[_full((n, d)), _mod_operand(mod)[1]] + w_specs
        + [_state_spec(conv_state, j), _state_spec(h0, j)],
        out_specs=[_full((n, w)), _full((n, 3, w)), _full((n, w))],
        out_shape=[jax.ShapeDtypeStruct((n, w), BF16),
                   jax.ShapeDtypeStruct((n, 3, w), F32),
                   jax.ShapeDtypeStruct((n, w), F32)],
        scratch_shapes=[pltpu.VMEM((n, w), F32), pltpu.VMEM((n, w), F32)],
        compiler_params=_params("arbitrary"),
        name="rglru_decode",
    )(x, mod[0], *w_arrs, conv_state, h0)


def _gdn_qkvg(u, u_gate, wqkv_ref, wg_ref, conv_fn, q_ref, k_ref, v_ref, gz_ref, buf, zero, after_first=None):
    hk = gz_ref.shape[-1]
    nh = hk // GDN_DK
    dsts = ((q_ref, GDN_DK ** -0.5), (k_ref, 1.0), (v_ref, None))
    order = (0, None, 1, 2)

    def consume(i, pre):
        j = order[i]
        if j is None:
            gz_ref[...] = _silu(pre)
            return
        dst, scale = dsts[j]
        act = _silu(conv_fn(pre, j))
        for h in range(nh):
            xh = act[:, h * GDN_DK:(h + 1) * GDN_DK]
            if scale is not None:
                ss = jnp.sum(xh * xh, axis=-1, keepdims=True)
                xh = xh * (lax.rsqrt(ss + EPS) * scale)
            dst[:, h * GDN_DK:(h + 1) * GDN_DK] = xh

    def dot_of(j):
        if j is None:
            return jnp.dot(u_gate, wg_ref[...], preferred_element_type=F32)
        return jnp.dot(u, wqkv_ref[:, j * hk:(j + 1) * hk], preferred_element_type=F32)

    _pipelined([functools.partial(dot_of, j) for j in order], consume, buf, zero, after_first)


def _gdn_in_kernel(z_ref, x_ref, mod_ref, wqkv_ref, wg_ref, wab_ref, wabt_ref, cw_ref, alr_ref, dtr_ref,
                   alc_ref, dtc_ref, q_ref, k_ref, v_ref, gz_ref, gb_ref, gr_ref, tail_ref, p_s):
    t = x_ref.shape[0]
    hk = gz_ref.shape[-1]
    nh = hk // GDN_DK

    @pl.when(pl.program_id(1) == 0)
    def _():
        tail_ref[...] = jnp.zeros_like(tail_ref)

    seq = pl.program_id(0)
    u = _modnorm(x_ref[...], _mrow(mod_ref, 1, seq), _mrow(mod_ref, 0, seq)).astype(BF16)

    def conv_fn(pre, j):
        cs = slice(j * hk, (j + 1) * hk)
        out = _conv_seq(pre, tail_ref[:, cs], cw_ref.at[:, cs], None)
        tail_ref[:, cs] = pre[t - SUBLANES:, :]
        return out

    def decays():
        ab = jnp.dot(u, wab_ref[...].astype(BF16), preferred_element_type=F32)
        abt = lax.dot_general(wabt_ref[...].astype(BF16), u, (((1,), (1,)), ((), ())),
                              preferred_element_type=F32)
        g_col = -jnp.exp(alr_ref[...]) * jax.nn.softplus(ab + dtr_ref[...])
        g_row = -jnp.exp(alc_ref[...]) * jax.nn.softplus(abt + dtc_ref[...])
        gc = _sdot_r(_block_tri(t, GDN_CHUNK, True), g_col)
        gr = _sdot_l(g_row, _block_tri(t, GDN_CHUNK, False))
        lane = lax.broadcasted_iota(jnp.int32, ab.shape, 1)
        gb_ref[...] = jnp.where(lane < nh, gc, _sigmoid(ab))
        gr_ref[...] = gr[:SUBLANES, :]

    _gdn_qkvg(u, u, wqkv_ref, wg_ref, conv_fn, q_ref, k_ref, v_ref, gz_ref, p_s, z_ref[0], decays)


def _gdn_in(x, mod, p):
    b, l, d = x.shape
    hk = p["hk"]
    t = ROW_TILE
    tok = lambda width: pl.BlockSpec((None, t, width), lambda i, c: (i, c, 0))
    w_arrs, w_specs = _unzip([p["wqkv"], p["wg"], p["wab"], p["wabt"], p["cw"], p["alr"], p["dtr"],
                              p["alc"], p["dtc"]])
    zero, zero_spec = _zero_operand()
    return pl.pallas_call(
        _gdn_in_kernel,
        grid=(b, l // t),
        in_specs=[zero_spec, tok(d), _mod_operand(mod)[1]] + w_specs,
        out_specs=[tok(hk), tok(hk), tok(hk), tok(hk), tok(LANES),
                   pl.BlockSpec((None, SUBLANES, t), lambda i, c: (i, 0, c)),
                   pl.BlockSpec((None, SUBLANES, 3 * hk), lambda i, c: (i, 0, 0))],
        out_shape=[jax.ShapeDtypeStruct((b, l, hk), F32)] * 4
        + [jax.ShapeDtypeStruct((b, l, LANES), F32),
           jax.ShapeDtypeStruct((b, SUBLANES, l), F32),
           jax.ShapeDtypeStruct((b, SUBLANES, 3 * hk), F32)],
        scratch_shapes=[pltpu.VMEM((DRAIN_SLOTS, t, hk), F32)],
        compiler_params=_params("parallel", "arbitrary"),
        name="gdn_in",
    )(zero, x, mod[0], *w_arrs)


def _gdn_scan_kernel(q_ref, k_ref, v_ref, gz_ref, gb_ref, gr_ref, ng_ref, y_ref, s_ref):
    c = GDN_CHUNK
    nseq, nh = s_ref.shape[0], s_ref.shape[1]

    @pl.when(pl.program_id(1) == 0)
    def _():
        s_ref[...] = jnp.zeros_like(s_ref)

    r = lax.broadcasted_iota(jnp.int32, (c, c), 0)
    cc = lax.broadcasted_iota(jnp.int32, (c, c), 1)
    causal = r >= cc
    strict = r > cc
    eye = jnp.where(r == cc, 1.0, 0.0)
    merge = []
    blk = 1
    while blk < c:
        merge.append((r // (2 * blk) == cc // (2 * blk)) & ((r // blk) % 2 == 1) & ((cc // blk) % 2 == 0))
        blk *= 2

    nchunk = gz_ref.shape[1] // c
    trip = [(b, j, h) for b in range(nseq) for j in range(nchunk) for h in range(nh)]
    rows = lambda j: slice(j * c, (j + 1) * c)
    cols = lambda h: slice(h * GDN_DK, (h + 1) * GDN_DK)
    g_col = lambda b, j, h: gb_ref[b, rows(j), h:h + 1]
    beta_of = lambda b, j, h: gb_ref[b, rows(j), nh + h:nh + h + 1]
    g_row = lambda b, j, h: gr_ref[b, h:h + 1, rows(j)]
    blk_of = lambda ref, b, j, h: ref[b, rows(j), cols(h)]

    kks = [_bdot_nt(jnp.concatenate([blk_of(k_ref, *t) * beta_of(*t), blk_of(q_ref, *t)], axis=0),
                    blk_of(k_ref, *t)) for t in trip]
    a_s, qk_s = [], []
    for t, kk in zip(trip, kks):
        decay = jnp.where(causal, jnp.exp(jnp.where(causal, g_col(*t) - g_row(*t), 0.0)), 0.0)
        a_s.append(jnp.where(strict, kk[:c] * decay, 0.0))
        qk_s.append((kk[c:] * decay).astype(BF16))
    xs = [eye - jnp.where(merge[0], a, 0.0) for a in a_s]
    for m in merge[1:]:
        ts = [_bdot(jnp.where(m, a, 0.0), x) for a, x in zip(a_s, xs)]
        xs = [x - _bdot(x, t_) for x, t_ in zip(xs, ts)]
    uws = {}
    for t, x in zip(trip, xs):
        kb = blk_of(k_ref, *t) * beta_of(*t)
        rhs = jnp.concatenate([blk_of(v_ref, *t) * beta_of(*t), kb * jnp.exp(g_col(*t))], axis=1)
        uws[t] = _bdot(x, rhs)
    qks = dict(zip(trip, qk_s))

    for j in range(nchunk):
        cur = [(b, j, h) for b in range(nseq) for h in range(nh)]
        states = {t: s_ref[t[0], t[2]] for t in cur}
        ws_qs = {t: _bdot(jnp.concatenate([uws[t][:, GDN_DV:], blk_of(q_ref, *t) * jnp.exp(g_col(*t))],
                                          axis=0), states[t]) for t in cur}
        v_news = {t: uws[t][:, :GDN_DV] - ws_qs[t][:c] for t in cur}
        outs = {t: ws_qs[t][c:] + jnp.dot(qks[t], v_news[t].astype(BF16), preferred_element_type=F32)
                for t in cur}
        for t in cur:
            g_last = g_row(*t)[:, c - 1:c]
            k_dec = blk_of(k_ref, *t) * jnp.exp(g_last - g_col(*t))
            s_ref[t[0], t[2]] = jnp.exp(g_last) * states[t] + lax.dot_general(
                k_dec.astype(BF16), v_news[t].astype(BF16), (((0,), (0,)), ((), ())),
                preferred_element_type=F32)
        for t in cur:
            o = outs[t]
            ms = jnp.mean(o * o, axis=-1, keepdims=True)
            y_ref[t[0], rows(j), cols(t[2])] = (o * lax.rsqrt(ms + EPS) * ng_ref[...]
                                                * gz_ref[t[0], rows(j), cols(t[2])]).astype(BF16)


def _gdn_scan(q, k, v, gz, gb, gr, norm_g):
    b, l, hk = gz.shape
    nh = hk // GDN_DK
    t = GDN_STEP
    ns = GDN_SEQS
    tok = lambda width: pl.BlockSpec((ns, t, width), lambda i, c: (i, c, 0))
    ng_arr, ng_spec = norm_g
    return pl.pallas_call(
        _gdn_scan_kernel,
        grid=(b // ns, l // t),
        in_specs=[tok(hk), tok(hk), tok(hk), tok(hk), tok(LANES),
                  pl.BlockSpec((ns, SUBLANES, t), lambda i, c: (i, 0, c)), ng_spec],
        out_specs=[tok(hk), pl.BlockSpec((ns, nh, GDN_DK, GDN_DV), lambda i, c: (i, 0, 0, 0))],
        out_shape=[jax.ShapeDtypeStruct((b, l, hk), BF16),
                   jax.ShapeDtypeStruct((b, nh, GDN_DK, GDN_DV), F32)],
        compiler_params=_params("parallel", "arbitrary"),
        name="gdn_scan",
    )(q, k, v, gz, gb, gr, ng_arr)


def _gdn_dec_in_kernel(z_ref, x_ref, mod_ref, wqkv_ref, wg_ref, wab_ref, cw_ref, alr_ref, dtr_ref, cs_ref,
                       qt_ref, kt_ref, v_ref, gz_ref, eg_ref, be_ref, nb_ref, q_s, k_s, p_s):
    hk = v_ref.shape[-1]
    nh = hk // GDN_DK
    u = _modnorm(x_ref[...], mod_ref[1], mod_ref[0]).astype(BF16)

    def conv_fn(pre, j):
        return _conv_step(pre, cs_ref, nb_ref, cw_ref, slice(j * hk, (j + 1) * hk), None)

    _gdn_qkvg(u, u, wqkv_ref, wg_ref, conv_fn, q_s, k_s, v_ref, gz_ref, p_s, z_ref[0])
    ab = jnp.dot(u, wab_ref[...].astype(BF16), preferred_element_type=F32)
    e_g = jnp.exp(-jnp.exp(alr_ref[...]) * jax.nn.softplus(ab + dtr_ref[...]))
    beta = _sigmoid(ab)
    n = ab.shape[0]
    for h in range(nh):
        hs = slice(h * GDN_DK, (h + 1) * GDN_DK)
        qt_ref[h] = q_s[:, hs].T
        kt_ref[h] = k_s[:, hs].T
        eg_ref[h] = jnp.broadcast_to(e_g[:, h:h + 1], (n, LANES))
        be_ref[h] = jnp.broadcast_to(beta[:, nh + h:nh + h + 1], (n, LANES))


def _state_spec(state, j):
    return pl.BlockSpec((None,) + state.shape[1:], lambda *_: (j,) + (0,) * (state.ndim - 1))


def _gdn_dec_in(x, mod, p, conv_state, j):
    n, d = x.shape
    hk = p["hk"]
    nh = hk // GDN_DK
    w_arrs, w_specs = _unzip([p["wqkv"], p["wg"], p["wab"], p["cw"], p["alr"], p["dtr"]])
    zero, zero_spec = _zero_operand()
    return pl.pallas_call(
        _gdn_dec_in_kernel,
        grid=(1,),
        in_specs=[zero_spec, _full((n, d)), _mod_operand(mod)[1]] + w_specs + [_state_spec(conv_state, j)],
        out_specs=[_full((nh, GDN_DK, n)), _full((nh, GDN_DK, n)), _full((n, hk)), _full((n, hk)),
                   _full((nh, n, LANES)), _full((nh, n, LANES)), _full((n, 3, 3 * hk))],
        out_shape=[jax.ShapeDtypeStruct((nh, GDN_DK, n), F32)] * 2
        + [jax.ShapeDtypeStruct((n, hk), F32)] * 2
        + [jax.ShapeDtypeStruct((nh, n, LANES), F32)] * 2
        + [jax.ShapeDtypeStruct((n, 3, 3 * hk), F32)],
        scratch_shapes=[pltpu.VMEM((n, hk), F32)] * 2 + [pltpu.VMEM((DRAIN_SLOTS, n, hk), F32)],
        compiler_params=_params("arbitrary"),
        name="gdn_decode_in",
    )(zero, x, mod[0], *w_arrs, conv_state)


def _gdn_dec_state_kernel(qt_ref, kt_ref, v_ref, gz_ref, eg_ref, be_ref, ng_ref, s0_ref,
                          y_ref, s_ref, o_s):
    n = s0_ref.shape[0]
    for b in range(n):
        kc = kt_ref[:, b:b + 1]
        qc = qt_ref[:, b:b + 1]
        s0 = s0_ref[b]
        k_s0 = jnp.sum(kc * s0, axis=0, keepdims=True)
        q_s0 = jnp.sum(qc * s0, axis=0, keepdims=True)
        qk = jnp.sum(qc * kc, axis=0, keepdims=True)
        e_g = eg_ref[b:b + 1, :]
        v_new = be_ref[b:b + 1, :] * (v_ref[b:b + 1, :] - e_g * k_s0)
        o_s[b:b + 1, :] = e_g * q_s0 + qk * v_new
        s_ref[b] = e_g * s0 + kc * v_new
    o = o_s[...]
    ms = jnp.mean(o * o, axis=-1, keepdims=True)
    y_ref[...] = (o * lax.rsqrt(ms + EPS) * ng_ref[...] * gz_ref[...]).astype(BF16)


def _gdn_dec_state(qt, kt, v, gz, eg, be, norm_g, s0, j):
    _, n, nh, dk, dv = s0.shape
    head = lambda a, bb: pl.BlockSpec((None, a, bb), lambda h: (h, 0, 0))
    col = pl.BlockSpec((n, dv), lambda h: (0, h))
    ng_arr, ng_spec = norm_g
    return pl.pallas_call(
        _gdn_dec_state_kernel,
        grid=(nh,),
        in_specs=[head(dk, n), head(dk, n), col, col, head(n, LANES), head(n, LANES), ng_spec,
                  pl.BlockSpec((None, n, None, dk, dv), lambda h: (j, 0, h, 0, 0))],
        out_specs=[col, pl.BlockSpec((n, None, dk, dv), lambda h: (0, h, 0, 0))],
        out_shape=[jax.ShapeDtypeStruct((n, nh * dv), BF16), jax.ShapeDtypeStruct((n, nh, dk, dv), F32)],
        scratch_shapes=[pltpu.VMEM((n, dv), F32)],
        compiler_params=_params("parallel"),
        name="gdn_decode_state",
    )(qt, kt, v, gz, eg, be, ng_arr, s0)


def _ssd_proj(u, u_z, wz_ref, wx_refs, conv_fn, zs_ref, xs_ref, bm_ref, cm_ref, buf, zero, after_first=None):
    gn = bm_ref.shape[-1]
    nz = zs_ref.shape[-1] // WBLK
    nx = len(wx_refs)
    order = [blk for pair in zip([("x", j) for j in range(nx)], [("z", j) for j in range(nz)] + [None] * nx)
             for blk in pair if blk is not None]

    def consume(i, pre):
        kind, j = order[i]
        cs = slice(j * WBLK, (j + 1) * WBLK)
        if kind == "z":
            zs_ref[:, cs] = _silu(pre)
            return
        act = _silu(conv_fn(pre, cs))
        if j + 1 < nx:
            xs_ref[:, cs] = act
        else:
            bm_ref[...] = act[:, :gn]
            cm_ref[...] = act[:, gn:]

    def dot_of(kind, j):
        if kind == "z":
            return jnp.dot(u_z, wz_ref[:, j * WBLK:(j + 1) * WBLK], preferred_element_type=F32)
        return jnp.dot(u, wx_refs[j][...], preferred_element_type=F32)

    _pipelined([functools.partial(dot_of, *blk) for blk in order], consume, buf, zero, after_first)


def _ssd_in_kernel(z_ref, x_ref, mod_ref, wz_ref, wx0_ref, wx1_ref, wx2_ref, wdt_ref, wdtt_ref, cw_ref,
                   cb_ref, dtr_ref, alr_ref, dtc_ref, alc_ref,
                   zs_ref, xs_ref, bm_ref, cm_ref, da_ref, at_ref, dtt_ref, tail_ref, p_s):
    t = x_ref.shape[0]
    nh = at_ref.shape[0]

    @pl.when(pl.program_id(1) == 0)
    def _():
        tail_ref[...] = jnp.zeros_like(tail_ref)

    seq = pl.program_id(0)
    u = _modnorm(x_ref[...], _mrow(mod_ref, 1, seq), _mrow(mod_ref, 0, seq)).astype(BF16)

    def conv_fn(pre, cs):
        out = _conv_seq(pre, tail_ref[:, cs], cw_ref.at[:, cs], cb_ref[:, cs])
        tail_ref[:, cs] = pre[t - SUBLANES:, :]
        return out

    def decays():
        dt_c = jax.nn.softplus(jnp.dot(u, wdt_ref[...].astype(BF16), preferred_element_type=F32)
                               + dtr_ref[...])
        dt_r = jax.nn.softplus(lax.dot_general(wdtt_ref[...].astype(BF16), u, (((1,), (1,)), ((), ())),
                                               preferred_element_type=F32) + dtc_ref[...])
        acs_c = _sdot_r(_block_tri(t, SSD_CHUNK, True), dt_c * -jnp.exp(alr_ref[...]))
        acs_r = _sdot_l(dt_r * -jnp.exp(alc_ref[...]), _block_tri(t, SSD_CHUNK, False))
        lane = lax.broadcasted_iota(jnp.int32, dt_c.shape, 1)
        da_ref[...] = jnp.where(lane < nh, dt_c, pltpu.roll(acs_c, nh, 1))
        at_ref[...] = acs_r[:nh, :]
        dtt_ref[...] = dt_r[:nh, :]

    _ssd_proj(u, u, wz_ref, (wx0_ref, wx1_ref, wx2_ref), conv_fn, zs_ref, xs_ref, bm_ref, cm_ref, p_s, z_ref[0],
              decays)


def _ssd_in(x, mod, p):
    b, l, d = x.shape
    di, cd, nh = p["di"], p["cd"], p["nh"]
    gn = (cd - di) // 2
    t = ROW_TILE
    tok = lambda width: pl.BlockSpec((None, t, width), lambda i, c: (i, c, 0))
    w_arrs, w_specs = _unzip([p["wz"], *p["wx"], p["wdt"], p["wdtt"], p["cw"], p["cb"], p["dtr"],
                              p["alr"], p["dtc"], p["alc"]])
    rows_spec = pl.BlockSpec((None, nh, t), lambda i, c: (i, 0, c))
    zero, zero_spec = _zero_operand()
    return pl.pallas_call(
        _ssd_in_kernel,
        grid=(b, l // t),
        in_specs=[zero_spec, tok(d), _mod_operand(mod)[1]] + w_specs,
        out_specs=[tok(di), tok(di), tok(gn), tok(gn), tok(LANES), rows_spec, rows_spec,
                   pl.BlockSpec((None, SUBLANES, cd), lambda i, c: (i, 0, 0))],
        out_shape=[jax.ShapeDtypeStruct((b, l, di), F32), jax.ShapeDtypeStruct((b, l, di), F32),
                   jax.ShapeDtypeStruct((b, l, gn), F32), jax.ShapeDtypeStruct((b, l, gn), F32),
                   jax.ShapeDtypeStruct((b, l, LANES), F32),
                   jax.ShapeDtypeStruct((b, nh, l), F32), jax.ShapeDtypeStruct((b, nh, l), F32),
                   jax.ShapeDtypeStruct((b, SUBLANES, cd), F32)],
        scratch_shapes=[pltpu.VMEM((DRAIN_SLOTS, t, WBLK), F32)],
        compiler_params=_params("parallel", "arbitrary"),
        name="ssd_in",
    )(zero, x, mod[0], *w_arrs)


def _ssd_scan_kernel(zs_ref, xs_ref, bm_ref, cm_ref, da_ref, at_ref, dtt_ref, e2_ref, dsk_ref, ng_ref,
                     y_ref, hout_ref, ht_s):
    c = zs_ref.shape[0]
    di = zs_ref.shape[1]
    n = SSM_STATE
    pdim = SSM_HEADDIM
    gw = di // SSM_GROUPS
    hpg = gw // pdim
    nh = di // pdim
    ci = pl.program_id(1)

    @pl.when(ci == 0)
    def _():
        ht_s[...] = jnp.zeros_like(ht_s)

    da = da_ref[...]
    lane_a = lax.broadcasted_iota(jnp.int32, da.shape, 1)
    is_acs = (lane_a >= nh) & (lane_a < 2 * nh)
    acs = jnp.where(is_acs, da, 0.0)
    dt_on_acs = jnp.where(is_acs, pltpu.roll(da, nh, 1), 0.0)
    eacs_e = _sdot_l(jnp.where(is_acs, jnp.exp(acs), 0.0), e2_ref[...])
    dsdt_e = _sdot_l(jnp.exp(acs[c - 1:c, :] - acs) * dt_on_acs, e2_ref[...])
    xs = xs_ref[...]
    bms = [bm_ref[:, g * n:(g + 1) * n] for g in range(SSM_GROUPS)]
    cms = [cm_ref[:, g * n:(g + 1) * n] for g in range(SSM_GROUPS)]
    xd = xs * dsdt_e

    r = lax.broadcasted_iota(jnp.int32, (c, c), 0)
    cc = lax.broadcasted_iota(jnp.int32, (c, c), 1)
    causal = r >= cc
    lane = lax.broadcasted_iota(jnp.int32, (c, 2 * pdim), 1)

    groups = range(SSM_GROUPS)
    gsl = lambda g: slice(g * gw, (g + 1) * gw)
    nsl = lambda g: slice(g * n, (g + 1) * n)
    assert n == LANES
    cbs = [_bdot_nt(cms[g], bms[g]) for g in groups]
    hts = [ht_s[g] for g in groups]
    y_offs = [_bdot(cms[g], hts[g]) * eacs_e[:, gsl(g)] for g in groups]
    for g in groups:
        ht_s[g] = eacs_e[c - 1:c, gsl(g)] * hts[g] + _bdot(bms[g].T, xd[:, gsl(g)])
    for g in groups:
        y_parts = []
        for pr in range(hpg // 2):
            ms = []
            for hh in (2 * pr, 2 * pr + 1):
                h = g * hpg + hh
                seg = da[:, nh + h:nh + h + 1] - at_ref[h:h + 1, :]
                lm = jnp.where(causal, jnp.exp(jnp.where(causal, seg, 0.0)), 0.0)
                ms.append((cbs[g] * lm * dtt_ref[h:h + 1, :]).astype(BF16))
            ps = slice(g * gw + pr * 2 * pdim, g * gw + (pr + 1) * 2 * pdim)
            xp = xs[:, ps].astype(BF16)
            zero = jnp.zeros_like(xp)
            rhs = jnp.concatenate([jnp.where(lane < pdim, xp, zero), jnp.where(lane >= pdim, xp, zero)], axis=0)
            y_d = jnp.dot(jnp.concatenate(ms, axis=1), rhs, preferred_element_type=F32)
            os_ = slice(pr * 2 * pdim, (pr + 1) * 2 * pdim)
            y_parts.append(y_d + y_offs[g][:, os_] + dsk_ref[:, ps] * xs[:, ps])
        yz = jnp.concatenate(y_parts, axis=1) * zs_ref[:, gsl(g)]
        ms_ = jnp.mean(yz * yz, axis=-1, keepdims=True)
        y_ref[:, gsl(g)] = (yz * lax.rsqrt(ms_ + EPS) * ng_ref[:, gsl(g)]).astype(BF16)

    @pl.when(ci == pl.num_programs(1) - 1)
    def _():
        for g in groups:
            hout_ref[gsl(g), :] = ht_s[g].T


def _ssd_scan(zs, xs, bm, cm, da, at, dtt, p):
    b, l, di = zs.shape
    gn = bm.shape[-1]
    nh = p["nh"]
    gw = di // SSM_GROUPS
    t = SSD_CHUNK
    tok = lambda width: pl.BlockSpec((None, t, width), lambda i, c: (i, c, 0))
    rows_spec = pl.BlockSpec((None, nh, t), lambda i, c: (i, 0, c))
    w_arrs, w_specs = _unzip([p["e2"], p["dsk"], p["ng"]])
    return pl.pallas_call(
        _ssd_scan_kernel,
        grid=(b, l // t),
        in_specs=[tok(di), tok(di), tok(gn), tok(gn), tok(LANES), rows_spec, rows_spec] + w_specs,
        out_specs=[tok(di), pl.BlockSpec((None, di, SSM_STATE), lambda i, c: (i, 0, 0))],
        out_shape=[jax.ShapeDtypeStruct((b, l, di), BF16),
                   jax.ShapeDtypeStruct((b, di, SSM_STATE), F32)],
        scratch_shapes=[pltpu.VMEM((SSM_GROUPS, SSM_STATE, gw), F32)],
        compiler_params=_params("parallel", "arbitrary"),
        name="ssd_scan",
    )(zs, xs, bm, cm, da, at, dtt, *w_arrs)


def _ssd_dec_in_kernel(z_ref, x_ref, mod_ref, wz_ref, wx0_ref, wx1_ref, wx2_ref, wdt_ref, cw_ref, cb_ref,
                       dtr_ref, alr_ref, e1_ref, cs_ref,
                       zs_ref, xs_ref, bm_ref, cm_ref, xt_ref, ea_ref, xdt_ref, eae_ref, nb_ref, p_s):
    n = x_ref.shape[0]
    di = zs_ref.shape[-1]
    nh = di // SSM_HEADDIM
    u = _modnorm(x_ref[...], mod_ref[1], mod_ref[0]).astype(BF16)

    def conv_fn(pre, cs):
        return _conv_step(pre, cs_ref, nb_ref, cw_ref, cs, cb_ref[:, cs])

    _ssd_proj(u, u, wz_ref, (wx0_ref, wx1_ref, wx2_ref), conv_fn, zs_ref, xs_ref, bm_ref, cm_ref, p_s, z_ref[0])
    dt = jax.nn.softplus(jnp.dot(u, wdt_ref[...].astype(BF16), preferred_element_type=F32) + dtr_ref[...])
    e_a = jnp.exp(dt * -jnp.exp(alr_ref[...]))
    xdt_ref[...] = xs_ref[...] * _sdot_l(dt, e1_ref[...])
    eae_ref[...] = _sdot_l(e_a, e1_ref[...])
    for j in range(di // LANES):
        xt_ref[j * LANES:(j + 1) * LANES, :] = xdt_ref[:, j * LANES:(j + 1) * LANES].T
    for h in range(nh):
        ea_ref[h] = jnp.broadcast_to(e_a[:, h:h + 1], (n, LANES))


def _ssd_dec_in(x, mod, p, conv_state, j):
    n, d = x.shape
    di, cd, nh = p["di"], p["cd"], p["nh"]
    gn = (cd - di) // 2
    w_arrs, w_specs = _unzip([p["wz"], *p["wx"], p["wdt"], p["cw"], p["cb"], p["dtr"], p["alr"], p["e1"]])
    zero, zero_spec = _zero_operand()
    return pl.pallas_call(
        _ssd_dec_in_kernel,
        grid=(1,),
        in_specs=[zero_spec, _full((n, d)), _mod_operand(mod)[1]] + w_specs + [_state_spec(conv_state, j)],
        out_specs=[_full((n, di)), _full((n, di)), _full((n, gn)), _full((n, gn)), _full((di, n)),
                   _full((nh, n, LANES)), _full((n, di)), _full((n, di)), _full((n, 3, cd))],
        out_shape=[jax.ShapeDtypeStruct((n, di), F32), jax.ShapeDtypeStruct((n, di), F32),
                   jax.ShapeDtypeStruct((n, gn), F32), jax.ShapeDtypeStruct((n, gn), F32),
                   jax.ShapeDtypeStruct((di, n), F32),
                   jax.ShapeDtypeStruct((nh, n, LANES), F32),
                   jax.ShapeDtypeStruct((n, di), F32), jax.ShapeDtypeStruct((n, di), F32),
                   jax.ShapeDtypeStruct((n, 3, cd), F32)],
        scratch_shapes=[pltpu.VMEM((DRAIN_SLOTS, n, WBLK), F32)],
        compiler_params=_params("arbitrary"),
        name="ssd_decode_in",
    )(zero, x, mod[0], *w_arrs, conv_state)


def _ssd_dec_state_kernel(xt_ref, bm_ref, cm_ref, ea_ref, h0_ref, yo_ref, h_ref):
    n = h0_ref.shape[0]
    pdim = h0_ref.shape[2]
    for b in range(n):
        bb = bm_ref[b:b + 1, :]
        c8 = jnp.broadcast_to(cm_ref[b:b + 1, :], (SUBLANES, bb.shape[1]))
        outs = []
        for hh in range(2):
            h0 = h0_ref[b, hh]
            outs.append(_bdot_nt(c8, h0)[0:1, :])
            xc = xt_ref[hh * pdim:(hh + 1) * pdim, b:b + 1]
            h_ref[b, hh] = ea_ref[hh, b:b + 1, :] * h0 + xc * bb
        yo_ref[b:b + 1, :] = jnp.concatenate(outs, axis=1)


def _ssd_dec_state(xt, bm, cm, ea, h0, j):
    _, n, nh, pdim, ns = h0.shape
    hpg = nh // SSM_GROUPS
    return pl.pallas_call(
        _ssd_dec_state_kernel,
        grid=(nh // 2,),
        in_specs=[pl.BlockSpec((2 * pdim, n), lambda i: (i, 0)),
                  pl.BlockSpec((n, ns), lambda i: (0, (2 * i) // hpg)),
                  pl.BlockSpec((n, ns), lambda i: (0, (2 * i) // hpg)),
                  pl.BlockSpec((2, n, LANES), lambda i: (i, 0, 0)),
                  pl.BlockSpec((None, n, 2, pdim, ns), lambda i: (j, 0, i, 0, 0))],
        out_specs=[pl.BlockSpec((n, 2 * pdim), lambda i: (0, i)),
                   pl.BlockSpec((n, 2, pdim, ns), lambda i: (0, i, 0, 0))],
        out_shape=[jax.ShapeDtypeStruct((n, nh * pdim), F32), jax.ShapeDtypeStruct((n, nh, pdim, ns), F32)],
        compiler_params=_params("parallel"),
        name="ssd_decode_state",
    )(xt, bm, cm, ea, h0)


def _ssd_dec_out_kernel(yo_ref, xs_ref, bm_ref, cm_ref, zs_ref, eae_ref, xdt_ref, dsk_ref, ng_ref, y_ref):
    di = xs_ref.shape[1]
    gw = di // SSM_GROUPS
    n = SSM_STATE
    for g in range(SSM_GROUPS):
        gs = slice(g * gw, (g + 1) * gw)
        cb = jnp.sum(cm_ref[:, g * n:(g + 1) * n] * bm_ref[:, g * n:(g + 1) * n], axis=-1, keepdims=True)
        y = cb * xdt_ref[:, gs] + eae_ref[:, gs] * yo_ref[:, gs] + dsk_ref[:, gs] * xs_ref[:, gs]
        yz = y * zs_ref[:, gs]
        ms = jnp.mean(yz * yz, axis=-1, keepdims=True)
        y_ref[:, gs] = (yz * lax.rsqrt(ms + EPS) * ng_ref[:, gs]).astype(BF16)


def _ssd_dec_out(yo, xs, bm, cm, zs, eae, xdt, p):
    n, di = xs.shape
    gn = bm.shape[1]
    w_arrs, w_specs = _unzip([p["dsk"], p["ng"]])
    return pl.pallas_call(
        _ssd_dec_out_kernel,
        grid=(1,),
        in_specs=[_full((n, di)), _full((n, di)), _full((n, gn)), _full((n, gn)), _full((n, di)),
                  _full((n, di)), _full((n, di))] + w_specs,
        out_specs=_full((n, di)),
        out_shape=jax.ShapeDtypeStruct((n, di), BF16),
        compiler_params=_params("arbitrary"),
        name="ssd_decode_out",
    )(yo, xs, bm, cm, zs, eae, xdt, *w_arrs)


def _pad_lanes(v):
    return jnp.pad(v.astype(F32), (0, LANES - v.shape[0])).reshape(1, LANES)


def _pad_cols(w):
    return jnp.pad(w, ((0, 0), (0, LANES - w.shape[1])))


def _small(a):
    return _w(a, a.shape, (0,) * a.ndim)


def _rg_params(j, w_in, conv_w, conv_b, gate_w, gate_b, lam, w_out):
    nl, d, w2 = w_in.shape
    w = w2 // 2
    nblk = w // RG_BLOCK
    gate = (None, None, nblk, RG_BLOCK, RG_BLOCK)
    return dict(width=w,
                wy=_w(w_in, (None, d, w), (j, 0, 0)), wx=_w(w_in, (None, d, w), (j, 0, 1)),
                cw=_w(conv_w, (None, 4, w), (j, 0, 0)), cb=_w(conv_b.reshape(nl, 1, w), (None, 1, w), (j, 0, 0)),
                gwr=_w(gate_w, gate, (j, 0, 0, 0, 0)), gwi=_w(gate_w, gate, (j, 1, 0, 0, 0)),
                gb=_w(gate_b, (None, 2, w), (j, 0, 0)), lam=_w(lam.reshape(nl, 1, w), (None, 1, w), (j, 0, 0)),
                wo=_w(w_out, (None, w, d), (j, 0, 0)))


def _gdn_params(j, w_in, w_in_f32, conv_w, a_log, dt_bias, norm_g, w_out):
    nl, d, _ = w_in.shape
    qkv = conv_w.shape[-1]
    hk = qkv // 3
    wab = _pad_cols(w_in_f32[j, :, qkv + hk:])
    alr, dtr = _pad_lanes(a_log[j]), _pad_lanes(dt_bias[j])
    return dict(hk=hk,
                wqkv=_w(w_in, (None, d, qkv), (j, 0, 0)), wg=_w(w_in, (None, d, hk), (j, 0, qkv // hk)),
                wab=_small(wab), wabt=_small(wab.T), cw=_w(conv_w, (None, 4, qkv), (j, 0, 0)),
                alr=_small(alr), dtr=_small(dtr), alc=_small(alr.T), dtc=_small(dtr.T),
                ng=_w(norm_g.reshape(nl, 1, GDN_DV), (None, 1, GDN_DV), (j, 0, 0)),
                wo=_w(w_out, (None, hk, d), (j, 0, 0)))


def _ssd_params(j, w_in, w_in_f32, conv_w, conv_b, a_log, dt_bias, d_skip, norm_g, w_out):
    nl, d, _ = w_in.shape
    cd = conv_w.shape[-1]
    nh = a_log.shape[-1]
    di = nh * SSM_HEADDIM
    assert di % WBLK == 0 and (cd - di) == WBLK
    wdt = _pad_cols(w_in_f32[j, :, di + cd:])
    alr, dtr = _pad_lanes(a_log[j]), _pad_lanes(dt_bias[j])
    head_of = jnp.arange(di, dtype=jnp.int32) // SSM_HEADDIM
    rows = jnp.arange(LANES, dtype=jnp.int32)[:, None]
    e1 = (rows == head_of[None, :]).astype(BF16)
    e2 = (rows == head_of[None, :] + nh).astype(BF16)
    return dict(di=di, cd=cd, nh=nh,
                wz=_w(w_in, (None, d, di), (j, 0, 0)),
                wx=[_w(w_in, (None, d, WBLK), (j, 0, di // WBLK + i)) for i in range(cd // WBLK)],
                wdt=_small(wdt), wdtt=_small(wdt.T), cw=_w(conv_w, (None, 4, cd), (j, 0, 0)),
                cb=_w(conv_b.reshape(nl, 1, cd), (None, 1, cd), (j, 0, 0)),
                alr=_small(alr), dtr=_small(dtr), alc=_small(alr.T), dtc=_small(dtr.T),
                e1=_small(e1), e2=_small(e2),
                dsk=_small(jnp.repeat(d_skip[j], SSM_HEADDIM).reshape(1, di)),
                ng=_w(norm_g.reshape(nl, 1, di), (None, 1, di), (j, 0, 0)),
                wo=_w(w_out, (None, di, d), (j, 0, 0)))


def kernel(x_prompt, x_sample, state_rglru_conv, state_rglru_h, state_gdn_conv, state_gdn_S, state_ssd_conv, state_ssd_h, c_prompt, c_sample, w_mod, b_mod, w_mlp_up, w_mlp_down, final_norm_g, rg_w_in, rg_conv_w, rg_conv_b, rg_gate_w, rg_gate_b, rg_lambda, rg_w_out, gdn_w_in, gdn_conv_w, gdn_A_log, gdn_dt_bias, gdn_norm_g, gdn_w_out, ssd_w_in, ssd_conv_w, ssd_conv_b, ssd_A_log, ssd_dt_bias, ssd_D, ssd_norm_g, ssd_w_out):
    bp, l, d = x_prompt.shape
    ns = x_sample.shape[0]
    hid = w_mlp_up.shape[-1]
    assert x_sample.shape[1] == 1 and l % ROW_TILE == 0 and ns % SUBLANES == 0 and bp % GDN_SEQS == 0

    assert ns % bp == 0 and bp % SUBLANES == 0
    mod = _modulation(jnp.concatenate([c_sample, c_prompt], axis=0), w_mod, b_mod)
    mod_p = [(mod, i, bp, ns) for i in range(DEPTH)]
    mod_s = [(mod, i, ns, 0) for i in range(DEPTH)]
    fg = _small(final_norm_g.reshape(1, d))

    w_up, w_down = w_mlp_up, w_mlp_down
    rg_in, rg_gate, rg_out = rg_w_in.astype(BF16), rg_gate_w.astype(BF16), rg_w_out.astype(BF16)
    gdn_in, gdn_out = gdn_w_in.astype(BF16), gdn_w_out.astype(BF16)
    ssd_in, ssd_out = ssd_w_in.astype(BF16), ssd_w_out.astype(BF16)

    xp = x_prompt
    xs = x_sample.reshape(ns, d)
    tails = lambda t: t[:, SUBLANES - 3:, :]
    tails_interleaved = lambda t: t[:, SUBLANES - 1::SUBLANES, :]
    out = {k: [] for k in ("p_rg_conv", "p_rg_h", "p_gdn_conv", "p_gdn_S", "p_ssd_conv", "p_ssd_h",
                           "s_rg_conv", "s_rg_h", "s_gdn_conv", "s_gdn_S", "s_ssd_conv", "s_ssd_h")}
    for i in range(DEPTH):
        j = i // N_MIXERS
        kind = i % N_MIXERS
        if kind == 0:
            p = _rg_params(j, rg_in, rg_conv_w, rg_conv_b, rg_gate, rg_gate_b, rg_lambda, rg_out)
            yp, tail, h_last = _rg_prompt(xp, mod_p[i], p)
            out["p_rg_conv"].append(tails_interleaved(tail))
            out["p_rg_h"].append(h_last[:, 0, :])
            ysm, nb, h_new = _rg_decode(xs, mod_s[i], p, state_rglru_conv, state_rglru_h, j)
            out["s_rg_conv"].append(nb)
            out["s_rg_h"].append(h_new)
        elif kind == 1:
            p = _gdn_params(j, gdn_in, gdn_w_in, gdn_conv_w, gdn_A_log, gdn_dt_bias, gdn_norm_g, gdn_out)
            q, k, v, gz, gb, gr, tail = _gdn_in(xp, mod_p[i], p)
            yp, s_fin = _gdn_scan(q, k, v, gz, gb, gr, p["ng"])
            out["p_gdn_conv"].append(tails(tail))
            out["p_gdn_S"].append(s_fin)
            qt, kt, v1, gz1, eg, be, nb = _gdn_dec_in(xs, mod_s[i], p, state_gdn_conv, j)
            ysm, s_new = _gdn_dec_state(qt, kt, v1, gz1, eg, be, p["ng"], state_gdn_S, j)
            out["s_gdn_conv"].append(nb)
            out["s_gdn_S"].append(s_new)
        else:
            p = _ssd_params(j, ssd_in, ssd_w_in, ssd_conv_w, ssd_conv_b, ssd_A_log, ssd_dt_bias, ssd_D, ssd_norm_g,
                            ssd_out)
            zs, xc, bm, cm, da, at, dtt, tail = _ssd_in(xp, mod_p[i], p)
            yp, h_fin = _ssd_scan(zs, xc, bm, cm, da, at, dtt, p)
            out["p_ssd_conv"].append(tails(tail))
            out["p_ssd_h"].append(h_fin.reshape(bp, -1, SSM_HEADDIM, SSM_STATE))
            zs1, xc1, bm1, cm1, xt, ea, xdt, eae, nb = _ssd_dec_in(xs, mod_s[i], p, state_ssd_conv, j)
            yo, h_new = _ssd_dec_state(xt, bm1, cm1, ea, state_ssd_h, j)
            ysm = _ssd_dec_out(yo, xc1, bm1, cm1, zs1, eae, xdt, p)
            out["s_ssd_conv"].append(nb)
            out["s_ssd_h"].append(h_new)
        final = i == DEPTH - 1
        weights = [p["wo"], _w(w_up, (None, d, hid), (i, 0, 0)), _w(w_down, (None, hid, d), (i, 0, 0)), fg]
        y_rows = yp if yp.ndim == 4 else yp.reshape(bp * l, -1)
        xp = _outproj_mlp(xp.reshape(bp * l, d), y_rows, mod_p[i], weights,
                          tm=ROW_TILE, rows_per_mod=l, final=final).reshape(bp, l, d)
        xs = _outproj_mlp(xs, ysm, mod_s[i], weights, tm=ns, rows_per_mod=1, final=final)

    st = {k: jnp.stack(v) for k, v in out.items()}
    return (xp, xs.reshape(ns, 1, d),
            st["p_rg_conv"], st["p_rg_h"], st["p_gdn_conv"], st["p_gdn_S"], st["p_ssd_conv"], st["p_ssd_h"],
            st["s_rg_conv"], st["s_rg_h"], st["s_gdn_conv"], st["s_gdn_S"], st["s_ssd_conv"], st["s_ssd_h"])
```

```python
import functools

import jax
import jax.numpy as jnp
from jax import lax
from jax.experimental import pallas as pl
from jax.experimental.pallas import tpu as pltpu

F32 = jnp.float32
BF16 = jnp.bfloat16

DEPTH = 4
N_MIXERS = 3
EPS = 1e-6
RG_C = 8.0
RG_BLOCK = 256
GDN_DK = 128
GDN_DV = 128
SSM_HEADDIM = 64
SSM_STATE = 128
SSM_GROUPS = 4

SUBLANES = 8
LANES = 128

VMEM_LIMIT = 56 * 1024 * 1024
ROW_TILE = 512
GDN_CHUNK = 64
GDN_STEP = 128
GDN_SEQS = 1
SSD_CHUNK = 128
MLP_HCHUNK = 1024
WBLK = 1024
TAIL_ROWS = 3 * SUBLANES


def _params(*sem):
    return pltpu.CompilerParams(dimension_semantics=sem, vmem_limit_bytes=VMEM_LIMIT)


def _w(arr, blk, idx):
    return arr, pl.BlockSpec(blk, lambda *_: idx, pipeline_mode=pl.Buffered(1))


def _unzip(pairs):
    return [a for a, _ in pairs], [s for _, s in pairs]


def _bdot(a, b):
    return jnp.dot(a.astype(BF16), b.astype(BF16), preferred_element_type=F32)


def _bdot_nt(a, b):
    return lax.dot_general(a.astype(BF16), b.astype(BF16), (((1,), (1,)), ((), ())),
                           preferred_element_type=F32)


def _split(a):
    hi = a.astype(BF16)
    lo = (a - hi.astype(F32)).astype(BF16)
    return hi, lo


def _sdot_l(a, b_exact):
    hi, lo = _split(a)
    return (jnp.dot(hi, b_exact, preferred_element_type=F32)
            + jnp.dot(lo, b_exact, preferred_element_type=F32))


def _sdot_r(a_exact, b):
    hi, lo = _split(b)
    return (jnp.dot(a_exact, hi, preferred_element_type=F32)
            + jnp.dot(a_exact, lo, preferred_element_type=F32))


def _modnorm(x, sc, sh):
    ms = jnp.mean(x * x, axis=-1, keepdims=True)
    return x * lax.rsqrt(ms + EPS) * (1.0 + sc) + sh


def _sigmoid(x):
    return 0.5 * jnp.tanh(0.5 * x) + 0.5


def _silu(x):
    hx = 0.5 * x
    return hx * jnp.tanh(hx) + hx


def _conv_seq(xpre, tail, w_ref, bias):
    xcat = jnp.concatenate([tail, xpre], axis=0)
    acc = xpre * w_ref[3:4, :]
    for k in (1, 2, 3):
        shifted = pltpu.roll(xcat, k, 0)[SUBLANES:]
        acc = acc + shifted * w_ref[3 - k:4 - k, :]
    if bias is not None:
        acc = acc + bias
    return acc


def _interleave_rows(u_t):
    t = u_t.shape[0]
    r = lax.broadcasted_iota(jnp.int32, (t, t), 0)
    c = lax.broadcasted_iota(jnp.int32, (t, t), 1)
    perm = jnp.where(c == (r % SUBLANES) * (t // SUBLANES) + r // SUBLANES, 1.0, 0.0).astype(BF16)
    return jnp.dot(perm, u_t, preferred_element_type=F32).astype(BF16)


def _conv_interleaved(pre, tail_ref, cs, w_ref, bias):
    ng = pre.shape[0] // SUBLANES
    ntail = tail_ref.shape[0] // SUBLANES
    sub = lax.broadcasted_iota(jnp.int32, (SUBLANES, pre.shape[1]), 0)
    grp = lambda a, g: a[g * SUBLANES:(g + 1) * SUBLANES]
    acc = pre * w_ref[3:4, cs]
    if bias is not None:
        acc = acc + bias
    for k in (1, 2, 3):
        top = [jnp.where(sub == 0,
                         pltpu.roll(tail_ref[(ntail - k + i) * SUBLANES:(ntail - k + i + 1) * SUBLANES, cs], 1, 0),
                         pltpu.roll(grp(pre, ng - k + i), 1, 0)) for i in range(k)]
        shifted = jnp.concatenate(top + [pre[:(ng - k) * SUBLANES]], axis=0)
        acc = acc + shifted * w_ref[3 - k:4 - k, cs]
    tail_ref[:, cs] = pre[(ng - ntail) * SUBLANES:, :]
    return acc


def _conv_step(pre, cs_ref, nb_ref, w_ref, cs, bias):
    nb_ref[:, 0, cs] = cs_ref[:, 1, cs]
    nb_ref[:, 1, cs] = cs_ref[:, 2, cs]
    nb_ref[:, 2, cs] = pre
    out = (cs_ref[:, 0, cs] * w_ref[0:1, cs] + cs_ref[:, 1, cs] * w_ref[1:2, cs]
           + cs_ref[:, 2, cs] * w_ref[2:3, cs] + pre * w_ref[3:4, cs])
    return out if bias is None else out + bias


DRAIN_SLOTS = 2
RG_DRAIN_SLOTS = 3


def _zero_operand():
    return jnp.zeros((1,), jnp.int32), pl.BlockSpec(memory_space=pltpu.SMEM)


def _pipelined(dots, consume, buf, zero, after_first=None, side_after=0):
    n = len(dots)
    slots = buf.shape[0]
    ahead = slots - 1
    for j in range(min(ahead, n)):
        buf[zero + j % slots] = dots[j]()
    for j in range(n):
        if j + ahead < n:
            buf[zero + (j + ahead) % slots] = dots[j + ahead]()
        consume(j, buf[zero + j % slots])
        if j == side_after and after_first is not None:
            after_first()


def _block_tri(n, chunk, lower):
    r = lax.broadcasted_iota(jnp.int32, (n, n), 0)
    c = lax.broadcasted_iota(jnp.int32, (n, n), 1)
    same = (r // chunk) == (c // chunk)
    tri = (c <= r) if lower else (r <= c)
    return jnp.where(same & tri, 1.0, 0.0).astype(BF16)


def _mod_kernel(c_ref, w_ref, b_ref, o_ref):
    o_ref[...] = _bdot(_silu(c_ref[...]), w_ref[...]) + b_ref[...]


def _modulation(c_all, w_mod, b_mod):
    nb, d = c_all.shape
    n6 = w_mod.shape[-1]
    return pl.pallas_call(
        _mod_kernel,
        grid=(DEPTH, n6 // d),
        in_specs=[pl.BlockSpec((nb, d), lambda l, n: (0, 0)),
                  pl.BlockSpec((None, d, d), lambda l, n: (l, 0, n)),
                  pl.BlockSpec((None, 1, d), lambda l, n: (l, 0, n))],
        out_specs=pl.BlockSpec((None, None, nb, d), lambda l, n: (l, n, 0, 0)),
        out_shape=jax.ShapeDtypeStruct((DEPTH, n6 // d, nb, d), F32),
        compiler_params=_params("parallel", "parallel"),
        name="modulation",
    )(c_all, w_mod, b_mod.reshape(DEPTH, 1, n6))


def _mod_operand(mod):
    arr, layer, n_rows, first_row = mod
    return arr, pl.BlockSpec((None, arr.shape[1], n_rows, arr.shape[3]),
                             lambda *_: (layer, 0, first_row // n_rows, 0))


def _mrow(mod_ref, k, row):
    return mod_ref[k] if row is None else mod_ref[k, pl.ds(row, 1), :]


def _time_rows(blk_ref):
    ncb, t, _ = blk_ref.shape
    ng = t // SUBLANES
    return jnp.concatenate(
        [jnp.concatenate([blk_ref[cb, pl.ds(s, ng, stride=SUBLANES), :] for cb in range(ncb)], axis=1)
         for s in range(SUBLANES)], axis=0)


def _outproj_mlp_kernel(x_ref, y_ref, mod_ref, wo_ref, wu_ref, wd_ref, fg_ref, o_ref, *, final, tiles_per_seq):
    row = None if tiles_per_seq is None else pl.program_id(0) // tiles_per_seq
    y = y_ref[...] if len(y_ref.shape) == 2 else _time_rows(y_ref).astype(BF16)
    x1 = x_ref[...] + _mrow(mod_ref, 2, row) * jnp.dot(y, wo_ref[...], preferred_element_type=F32)
    u = _modnorm(x1, _mrow(mod_ref, 4, row), _mrow(mod_ref, 3, row)).astype(BF16)
    hidden = wu_ref.shape[1]
    acc = None
    for c in range(hidden // MLP_HCHUNK):
        cs = slice(c * MLP_HCHUNK, (c + 1) * MLP_HCHUNK)
        h = jnp.maximum(jnp.dot(u, wu_ref[:, cs].astype(BF16), preferred_element_type=F32), 0.0)
        p = jnp.dot((h * h).astype(BF16), wd_ref[cs, :].astype(BF16), preferred_element_type=F32)
        acc = p if acc is None else acc + p
    x2 = x1 + _mrow(mod_ref, 5, row) * acc
    if final:
        ms = jnp.mean(x2 * x2, axis=-1, keepdims=True)
        x2 = x2 * lax.rsqrt(ms + EPS) * fg_ref[...]
    o_ref[...] = x2


def _outproj_mlp(x, y, mod, weights, *, tm, rows_per_mod, final):
    n, d = x.shape
    per = None if rows_per_mod == 1 else rows_per_mod // tm
    assert per is not None or n == tm
    mod_arr, mod_spec = _mod_operand(mod)
    if y.ndim == 2:
        y_spec = pl.BlockSpec((tm, y.shape[1]), lambda i: (i, 0))
    else:
        assert tm == ROW_TILE and rows_per_mod == y.shape[2]
        y_spec = pl.BlockSpec((None, y.shape[1], tm, LANES), lambda i: (i // per, 0, i % per, 0))
    w_arrs, w_specs = _unzip(weights)
    return pl.pallas_call(
        functools.partial(_outproj_mlp_kernel, final=final, tiles_per_seq=per),
        grid=(n // tm,),
        in_specs=[pl.BlockSpec((tm, d), lambda i: (i, 0)), y_spec, mod_spec] + w_specs,
        out_specs=pl.BlockSpec((tm, d), lambda i: (i, 0)),
        out_shape=jax.ShapeDtypeStruct((n, d), F32),
        compiler_params=_params("parallel"),
        name="outproj_mlp",
    )(x, y, mod_arr, *w_arrs)


def _rg_gate_block(n, xbr, gwr_ref, gwi_ref, gb_ref, logsig, a_s, b_s):
    sl = slice(n * RG_BLOCK, (n + 1) * RG_BLOCK)
    xb = xbr.astype(BF16)
    gr = jnp.dot(xb, gwr_ref[n], preferred_element_type=F32)
    gi = jnp.dot(xb, gwi_ref[n], preferred_element_type=F32)
    log_a = (RG_C * logsig[:, sl]) * _sigmoid(gr + gb_ref[0:1, sl])
    a_s[:, sl] = jnp.exp(log_a)
    th = jnp.tanh(log_a)
    z = -2.0 * th / (1.0 - th)
    root = jnp.where(z > 0.0, z * lax.rsqrt(z), 0.0)
    b_s[:, sl] = root * _sigmoid(gi + gb_ref[1:2, sl]) * xbr


def _rg_gates(xbr, gwr_ref, gwi_ref, gb_ref, logsig, a_s, b_s):
    for n in range(xbr.shape[1] // RG_BLOCK):
        _rg_gate_block(n, xbr[:, n * RG_BLOCK:(n + 1) * RG_BLOCK], gwr_ref, gwi_ref, gb_ref, logsig, a_s, b_s)


def _rg_prompt_kernel(z_ref, x_ref, mod_ref, wy_ref, wx_ref, cw_ref, cb_ref, gwr_ref, gwi_ref, gb_ref,
                      lam_ref, y_ref, tail_ref, h_ref, a_s, b_s, y_s, p_s):
    t, w = a_s.shape
    ng = t // SUBLANES
    ncb = w // LANES

    @pl.when(pl.program_id(1) == 0)
    def _():
        tail_ref[...] = jnp.zeros_like(tail_ref)
        h_ref[...] = jnp.zeros_like(h_ref)

    seq = pl.program_id(0)
    u_t = _modnorm(x_ref[...], _mrow(mod_ref, 1, seq), _mrow(mod_ref, 0, seq)).astype(BF16)
    u = _interleave_rows(u_t)
    logsig = jax.nn.log_sigmoid(lam_ref[...])
    nblk = w // RG_BLOCK
    order = [(kind, n) for n in range(nblk) for kind in ("x", "y")]

    def consume(i, pre):
        kind, n = order[i]
        cs = slice(n * RG_BLOCK, (n + 1) * RG_BLOCK)
        if kind == "y":
            y_s[:, cs] = jax.nn.gelu(pre)
            return
        acc = _conv_interleaved(pre, tail_ref, cs, cw_ref, cb_ref[:, cs])
        _rg_gate_block(n, acc, gwr_ref, gwi_ref, gb_ref, logsig, a_s, b_s)

    def dot_of(kind, n):
        w_ref = wx_ref if kind == "x" else wy_ref
        return jnp.dot(u, w_ref[:, n * RG_BLOCK:(n + 1) * RG_BLOCK], preferred_element_type=F32)

    _pipelined([functools.partial(dot_of, *blk) for blk in order], consume, p_s, z_ref[0])

    def compose(gi, carry):
        ca, cbb = carry
        r0 = pl.multiple_of(gi * SUBLANES, SUBLANES)
        a = a_s[pl.ds(r0, SUBLANES), :]
        ca = a * ca
        cbb = a * cbb + b_s[pl.ds(r0, SUBLANES), :]
        a_s[pl.ds(r0, SUBLANES), :] = ca
        b_s[pl.ds(r0, SUBLANES), :] = cbb
        return ca, cbb

    a_end, b_end = lax.fori_loop(0, ng, compose, (jnp.ones((SUBLANES, w), F32), jnp.zeros((SUBLANES, w), F32)),
                                 unroll=4)
    h_in = jnp.zeros((SUBLANES, w), F32)
    for _ in range(SUBLANES):
        h_in = jnp.where(lax.broadcasted_iota(jnp.int32, (SUBLANES, w), 0) == 0, h_ref[...],
                         pltpu.roll(a_end * h_in + b_end, 1, 0))
    h_ref[...] = (a_end * h_in + b_end)[SUBLANES - 1:, :]
    hy = (a_s[...] * jnp.concatenate([h_in] * ng, axis=0) + b_s[...]) * y_s[...]
    for cb in range(ncb):
        y_ref[cb] = hy[:, cb * LANES:(cb + 1) * LANES]


def _rg_weights(p):
    return [p["wy"], p["wx"], p["cw"], p["cb"], p["gwr"], p["gwi"], p["gb"], p["lam"]]


def _rg_prompt(x, mod, p):
    b, l, d = x.shape
    w = p["width"]
    t = ROW_TILE
    w_arrs, w_specs = _unzip(_rg_weights(p))
    ncb = w // LANES
    ntail = (4 - 1) * SUBLANES
    zero, zero_spec = _zero_operand()
    return pl.pallas_call(
        _rg_prompt_kernel,
        grid=(b, l // t),
        in_specs=[zero_spec, pl.BlockSpec((None, t, d), lambda i, c: (i, c, 0)),
                  _mod_operand(mod)[1]] + w_specs,
        out_specs=[pl.BlockSpec((None, ncb, t, LANES), lambda i, c: (i, 0, c, 0)),
                   pl.BlockSpec((None, ntail, w), lambda i, c: (i, 0, 0)),
                   pl.BlockSpec((None, 1, w), lambda i, c: (i, 0, 0))],
        out_shape=[jax.ShapeDtypeStruct((b, ncb, l, LANES), F32),
                   jax.ShapeDtypeStruct((b, ntail, w), F32),
                   jax.ShapeDtypeStruct((b, 1, w), F32)],
        scratch_shapes=[pltpu.VMEM((t, w), F32)] * 3 + [pltpu.VMEM((RG_DRAIN_SLOTS, t, RG_BLOCK), F32)],
        compiler_params=_params("parallel", "arbitrary"),
        name="rglru_prompt",
    )(zero, x, mod[0], *w_arrs)


def _rg_decode_kernel(x_ref, mod_ref, wy_ref, wx_ref, cw_ref, cb_ref, gwr_ref, gwi_ref, gb_ref,
                      lam_ref, cs_ref, h0_ref, y_ref, nb_ref, h_ref, a_s, b_s):
    u = _modnorm(x_ref[...], mod_ref[1], mod_ref[0]).astype(BF16)
    xpre = jnp.dot(u, wx_ref[...], preferred_element_type=F32)
    xbr = _conv_step(xpre, cs_ref, nb_ref, cw_ref, slice(None), cb_ref[...])
    _rg_gates(xbr, gwr_ref, gwi_ref, gb_ref, jax.nn.log_sigmoid(lam_ref[...]), a_s, b_s)
    h = a_s[...] * h0_ref[...] + b_s[...]
    h_ref[...] = h
    y_br = jax.nn.gelu(jnp.dot(u, wy_ref[...], preferred_element_type=F32))
    y_ref[...] = (h * y_br).astype(BF16)


def _full(shape):
    return pl.BlockSpec(shape, lambda *_: (0,) * len(shape))


def _rg_decode(x, mod, p, conv_state, h0, j):
    n, d = x.shape
    w = p["width"]
    w_arrs, w_specs = _unzip(_rg_weights(p))
    return pl.pallas_call(
        _rg_decode_kernel,
        grid=(1,),
        in_specs=[_full((n, d)), _mod_operand(mod)[1]] + w_specs
        + [_state_spec(conv_state, j), _state_spec(h0, j)],
        out_specs=[_full((n, w)), _full((n, 3, w)), _full((n, w))],
        out_shape=[jax.ShapeDtypeStruct((n, w), BF16),
                   jax.ShapeDtypeStruct((n, 3, w), F32),
                   jax.ShapeDtypeStruct((n, w), F32)],
        scratch_shapes=[pltpu.VMEM((n, w), F32), pltpu.VMEM((n, w), F32)],
        compiler_params=_params("arbitrary"),
        name="rglru_decode",
    )(x, mod[0], *w_arrs, conv_state, h0)


def _gdn_qkvg(u, u_gate, wqkv_ref, wg_ref, conv_fn, q_ref, k_ref, v_ref, gz_ref, buf, zero, after_first=None):
    hk = gz_ref.shape[-1]
    nh = hk // GDN_DK
    dsts = ((q_ref, GDN_DK ** -0.5), (k_ref, 1.0), (v_ref, None))
    order = (0, None, 1, 2)

    def consume(i, pre):
        j = order[i]
        if j is None:
            gz_ref[...] = _silu(pre)
            return
        dst, scale = dsts[j]
        act = _silu(conv_fn(pre, j))
        for h in range(nh):
            xh = act[:, h * GDN_DK:(h + 1) * GDN_DK]
            if scale is not None:
                ss = jnp.sum(xh * xh, axis=-1, keepdims=True)
                xh = xh * (lax.rsqrt(ss + EPS) * scale)
            dst[:, h * GDN_DK:(h + 1) * GDN_DK] = xh

    def dot_of(j):
        if j is None:
            return jnp.dot(u_gate, wg_ref[...], preferred_element_type=F32)
        return jnp.dot(u, wqkv_ref[:, j * hk:(j + 1) * hk], preferred_element_type=F32)

    _pipelined([functools.partial(dot_of, j) for j in order], consume, buf, zero, after_first)


def _gdn_in_kernel(z_ref, x_ref, mod_ref, wqkv_ref, wg_ref, wab_ref, wabt_ref, cw_ref, alr_ref, dtr_ref,
                   alc_ref, dtc_ref, q_ref, k_ref, v_ref, gz_ref, gb_ref, gr_ref, tail_ref, p_s):
    t = x_ref.shape[0]
    hk = gz_ref.shape[-1]
    nh = hk // GDN_DK

    @pl.when(pl.program_id(1) == 0)
    def _():
        tail_ref[...] = jnp.zeros_like(tail_ref)

    seq = pl.program_id(0)
    u = _modnorm(x_ref[...], _mrow(mod_ref, 1, seq), _mrow(mod_ref, 0, seq)).astype(BF16)

    def conv_fn(pre, j):
        cs = slice(j * hk, (j + 1) * hk)
        out = _conv_seq(pre, tail_ref[:, cs], cw_ref.at[:, cs], None)
        tail_ref[:, cs] = pre[t - SUBLANES:, :]
        return out

    def decays():
        ab = jnp.dot(u, wab_ref[...].astype(BF16), preferred_element_type=F32)
        abt = lax.dot_general(wabt_ref[...].astype(BF16), u, (((1,), (1,)), ((), ())),
                              preferred_element_type=F32)
        g_col = -jnp.exp(alr_ref[...]) * jax.nn.softplus(ab + dtr_ref[...])
        g_row = -jnp.exp(alc_ref[...]) * jax.nn.softplus(abt + dtc_ref[...])
        gc = _sdot_r(_block_tri(t, GDN_CHUNK, True), g_col)
        gr = _sdot_l(g_row, _block_tri(t, GDN_CHUNK, False))
        lane = lax.broadcasted_iota(jnp.int32, ab.shape, 1)
        gb_ref[...] = jnp.where(lane < nh, gc, _sigmoid(ab))
        gr_ref[...] = gr[:SUBLANES, :]

    _gdn_qkvg(u, u, wqkv_ref, wg_ref, conv_fn, q_ref, k_ref, v_ref, gz_ref, p_s, z_ref[0], decays)


def _gdn_in(x, mod, p):
    b, l, d = x.shape
    hk = p["hk"]
    t = ROW_TILE
    tok = lambda width: pl.BlockSpec((None, t, width), lambda i, c: (i, c, 0))
    w_arrs, w_specs = _unzip([p["wqkv"], p["wg"], p["wab"], p["wabt"], p["cw"], p["alr"], p["dtr"],
                              p["alc"], p["dtc"]])
    zero, zero_spec = _zero_operand()
    return pl.pallas_call(
        _gdn_in_kernel,
        grid=(b, l // t),
        in_specs=[zero_spec, tok(d), _mod_operand(mod)[1]] + w_specs,
        out_specs=[tok(hk), tok(hk), tok(hk), tok(hk), tok(LANES),
                   pl.BlockSpec((None, SUBLANES, t), lambda i, c: (i, 0, c)),
                   pl.BlockSpec((None, SUBLANES, 3 * hk), lambda i, c: (i, 0, 0))],
        out_shape=[jax.ShapeDtypeStruct((b, l, hk), F32)] * 4
        + [jax.ShapeDtypeStruct((b, l, LANES), F32),
           jax.ShapeDtypeStruct((b, SUBLANES, l), F32),
           jax.ShapeDtypeStruct((b, SUBLANES, 3 * hk), F32)],
        scratch_shapes=[pltpu.VMEM((DRAIN_SLOTS, t, hk), F32)],
        compiler_params=_params("parallel", "arbitrary"),
        name="gdn_in",
    )(zero, x, mod[0], *w_arrs)


def _gdn_scan_kernel(q_ref, k_ref, v_ref, gz_ref, gb_ref, gr_ref, ng_ref, y_ref, s_ref):
    c = GDN_CHUNK
    nseq, nh = s_ref.shape[0], s_ref.shape[1]

    @pl.when(pl.program_id(1) == 0)
    def _():
        s_ref[...] = jnp.zeros_like(s_ref)

    r = lax.broadcasted_iota(jnp.int32, (c, c), 0)
    cc = lax.broadcasted_iota(jnp.int32, (c, c), 1)
    causal = r >= cc
    strict = r > cc
    eye = jnp.where(r == cc, 1.0, 0.0)
    merge = []
    blk = 1
    while blk < c:
        merge.append((r // (2 * blk) == cc // (2 * blk)) & ((r // blk) % 2 == 1) & ((cc // blk) % 2 == 0))
        blk *= 2

    nchunk = gz_ref.shape[1] // c
    trip = [(b, j, h) for b in range(nseq) for j in range(nchunk) for h in range(nh)]
    rows = lambda j: slice(j * c, (j + 1) * c)
    cols = lambda h: slice(h * GDN_DK, (h + 1) * GDN_DK)
    g_col = lambda b, j, h: gb_ref[b, rows(j), h:h + 1]
    beta_of = lambda b, j, h: gb_ref[b, rows(j), nh + h:nh + h + 1]
    g_row = lambda b, j, h: gr_ref[b, h:h + 1, rows(j)]
    blk_of = lambda ref, b, j, h: ref[b, rows(j), cols(h)]

    kks = [_bdot_nt(jnp.concatenate([blk_of(k_ref, *t) * beta_of(*t), blk_of(q_ref, *t)], axis=0),
                    blk_of(k_ref, *t)) for t in trip]
    a_s, qk_s = [], []
    for t, kk in zip(trip, kks):
        decay = jnp.where(causal, jnp.exp(jnp.where(causal, g_col(*t) - g_row(*t), 0.0)), 0.0)
        a_s.append(jnp.where(strict, kk[:c] * decay, 0.0))
        qk_s.append((kk[c:] * decay).astype(BF16))
    xs = [eye - jnp.where(merge[0], a, 0.0) for a in a_s]
    for m in merge[1:]:
        ts = [_bdot(jnp.where(m, a, 0.0), x) for a, x in zip(a_s, xs)]
        xs = [x - _bdot(x, t_) for x, t_ in zip(xs, ts)]
    uws = {}
    for t, x in zip(trip, xs):
        kb = blk_of(k_ref, *t) * beta_of(*t)
        rhs = jnp.concatenate([blk_of(v_ref, *t) * beta_of(*t), kb * jnp.exp(g_col(*t))], axis=1)
        uws[t] = _bdot(x, rhs)
    qks = dict(zip(trip, qk_s))

    for j in range(nchunk):
        cur = [(b, j, h) for b in range(nseq) for h in range(nh)]
        states = {t: s_ref[t[0], t[2]] for t in cur}
        ws_qs = {t: _bdot(jnp.concatenate([uws[t][:, GDN_DV:], blk_of(q_ref, *t) * jnp.exp(g_col(*t))],
                                          axis=0), states[t]) for t in cur}
        v_news = {t: uws[t][:, :GDN_DV] - ws_qs[t][:c] for t in cur}
        outs = {t: ws_qs[t][c:] + jnp.dot(qks[t], v_news[t].astype(BF16), preferred_element_type=F32)
                for t in cur}
        for t in cur:
            g_last = g_row(*t)[:, c - 1:c]
            k_dec = blk_of(k_ref, *t) * jnp.exp(g_last - g_col(*t))
            s_ref[t[0], t[2]] = jnp.exp(g_last) * states[t] + lax.dot_general(
                k_dec.astype(BF16), v_news[t].astype(BF16), (((0,), (0,)), ((), ())),
                preferred_element_type=F32)
        for t in cur:
            o = outs[t]
            ms = jnp.mean(o * o, axis=-1, keepdims=True)
            y_ref[t[0], rows(j), cols(t[2])] = (o * lax.rsqrt(ms + EPS) * ng_ref[...]
                                                * gz_ref[t[0], rows(j), cols(t[2])]).astype(BF16)


def _gdn_scan(q, k, v, gz, gb, gr, norm_g):
    b, l, hk = gz.shape
    nh = hk // GDN_DK
    t = GDN_STEP
    ns = GDN_SEQS
    tok = lambda width: pl.BlockSpec((ns, t, width), lambda i, c: (i, c, 0))
    ng_arr, ng_spec = norm_g
    return pl.pallas_call(
        _gdn_scan_kernel,
        grid=(b // ns, l // t),
        in_specs=[tok(hk), tok(hk), tok(hk), tok(hk), tok(LANES),
                  pl.BlockSpec((ns, SUBLANES, t), lambda i, c: (i, 0, c)), ng_spec],
        out_specs=[tok(hk), pl.BlockSpec((ns, nh, GDN_DK, GDN_DV), lambda i, c: (i, 0, 0, 0))],
        out_shape=[jax.ShapeDtypeStruct((b, l, hk), BF16),
                   jax.ShapeDtypeStruct((b, nh, GDN_DK, GDN_DV), F32)],
        compiler_params=_params("parallel", "arbitrary"),
        name="gdn_scan",
    )(q, k, v, gz, gb, gr, ng_arr)


def _gdn_dec_in_kernel(z_ref, x_ref, mod_ref, wqkv_ref, wg_ref, wab_ref, cw_ref, alr_ref, dtr_ref, cs_ref,
                       qt_ref, kt_ref, v_ref, gz_ref, eg_ref, be_ref, nb_ref, q_s, k_s, p_s):
    hk = v_ref.shape[-1]
    nh = hk // GDN_DK
    u = _modnorm(x_ref[...], mod_ref[1], mod_ref[0]).astype(BF16)

    def conv_fn(pre, j):
        return _conv_step(pre, cs_ref, nb_ref, cw_ref, slice(j * hk, (j + 1) * hk), None)

    _gdn_qkvg(u, u, wqkv_ref, wg_ref, conv_fn, q_s, k_s, v_ref, gz_ref, p_s, z_ref[0])
    ab = jnp.dot(u, wab_ref[...].astype(BF16), preferred_element_type=F32)
    e_g = jnp.exp(-jnp.exp(alr_ref[...]) * jax.nn.softplus(ab + dtr_ref[...]))
    beta = _sigmoid(ab)
    n = ab.shape[0]
    for h in range(nh):
        hs = slice(h * GDN_DK, (h + 1) * GDN_DK)
        qt_ref[h] = q_s[:, hs].T
        kt_ref[h] = k_s[:, hs].T
        eg_ref[h] = jnp.broadcast_to(e_g[:, h:h + 1], (n, LANES))
        be_ref[h] = jnp.broadcast_to(beta[:, nh + h:nh + h + 1], (n, LANES))


def _state_spec(state, j):
    return pl.BlockSpec((None,) + state.shape[1:], lambda *_: (j,) + (0,) * (state.ndim - 1))


def _gdn_dec_in(x, mod, p, conv_state, j):
    n, d = x.shape
    hk = p["hk"]
    nh = hk // GDN_DK
    w_arrs, w_specs = _unzip([p["wqkv"], p["wg"], p["wab"], p["cw"], p["alr"], p["dtr"]])
    zero, zero_spec = _zero_operand()
    return pl.pallas_call(
        _gdn_dec_in_kernel,
        grid=(1,),
        in_specs=[zero_spec, _full((n, d)), _mod_operand(mod)[1]] + w_specs + [_state_spec(conv_state, j)],
        out_specs=[_full((nh, GDN_DK, n)), _full((nh, GDN_DK, n)), _full((n, hk)), _full((n, hk)),
                   _full((nh, n, LANES)), _full((nh, n, LANES)), _full((n, 3, 3 * hk))],
        out_shape=[jax.ShapeDtypeStruct((nh, GDN_DK, n), F32)] * 2
        + [jax.ShapeDtypeStruct((n, hk), F32)] * 2
        + [jax.ShapeDtypeStruct((nh, n, LANES), F32)] * 2
        + [jax.ShapeDtypeStruct((n, 3, 3 * hk), F32)],
        scratch_shapes=[pltpu.VMEM((n, hk), F32)] * 2 + [pltpu.VMEM((DRAIN_SLOTS, n, hk), F32)],
        compiler_params=_params("arbitrary"),
        name="gdn_decode_in",
    )(zero, x, mod[0], *w_arrs, conv_state)


def _gdn_dec_state_kernel(qt_ref, kt_ref, v_ref, gz_ref, eg_ref, be_ref, ng_ref, s0_ref,
                          y_ref, s_ref, o_s):
    n = s0_ref.shape[0]
    for b in range(n):
        kc = kt_ref[:, b:b + 1]
        qc = qt_ref[:, b:b + 1]
        s0 = s0_ref[b]
        k_s0 = jnp.sum(kc * s0, axis=0, keepdims=True)
        q_s0 = jnp.sum(qc * s0, axis=0, keepdims=True)
        qk = jnp.sum(qc * kc, axis=0, keepdims=True)
        e_g = eg_ref[b:b + 1, :]
        v_new = be_ref[b:b + 1, :] * (v_ref[b:b + 1, :] - e_g * k_s0)
        o_s[b:b + 1, :] = e_g * q_s0 + qk * v_new
        s_ref[b] = e_g * s0 + kc * v_new
    o = o_s[...]
    ms = jnp.mean(o * o, axis=-1, keepdims=True)
    y_ref[...] = (o * lax.rsqrt(ms + EPS) * ng_ref[...] * gz_ref[...]).astype(BF16)


def _gdn_dec_state(qt, kt, v, gz, eg, be, norm_g, s0, j):
    _, n, nh, dk, dv = s0.shape
    head = lambda a, bb: pl.BlockSpec((None, a, bb), lambda h: (h, 0, 0))
    col = pl.BlockSpec((n, dv), lambda h: (0, h))
    ng_arr, ng_spec = norm_g
    return pl.pallas_call(
        _gdn_dec_state_kernel,
        grid=(nh,),
        in_specs=[head(dk, n), head(dk, n), col, col, head(n, LANES), head(n, LANES), ng_spec,
                  pl.BlockSpec((None, n, None, dk, dv), lambda h: (j, 0, h, 0, 0))],
        out_specs=[col, pl.BlockSpec((n, None, dk, dv), lambda h: (0, h, 0, 0))],
        out_shape=[jax.ShapeDtypeStruct((n, nh * dv), BF16), jax.ShapeDtypeStruct((n, nh, dk, dv), F32)],
        scratch_shapes=[pltpu.VMEM((n, dv), F32)],
        compiler_params=_params("parallel"),
        name="gdn_decode_state",
    )(qt, kt, v, gz, eg, be, ng_arr, s0)


def _ssd_proj(u, u_z, wz_ref, wx_refs, conv_fn, zs_ref, xs_ref, bm_ref, cm_ref, buf, zero, after_first=None):
    gn = bm_ref.shape[-1]
    nz = zs_ref.shape[-1] // WBLK
    nx = len(wx_refs)
    order = [blk for pair in zip([("x", j) for j in range(nx)], [("z", j) for j in range(nz)] + [None] * nx)
             for blk in pair if blk is not None]

    def consume(i, pre):
        kind, j = order[i]
        cs = slice(j * WBLK, (j + 1) * WBLK)
        if kind == "z":
            zs_ref[:, cs] = _silu(pre)
            return
        act = _silu(conv_fn(pre, cs))
        if j + 1 < nx:
            xs_ref[:, cs] = act
        else:
            bm_ref[...] = act[:, :gn]
            cm_ref[...] = act[:, gn:]

    def dot_of(kind, j):
        if kind == "z":
            return jnp.dot(u_z, wz_ref[:, j * WBLK:(j + 1) * WBLK], preferred_element_type=F32)
        return jnp.dot(u, wx_refs[j][...], preferred_element_type=F32)

    _pipelined([functools.partial(dot_of, *blk) for blk in order], consume, buf, zero, after_first,
               side_after=2)


def _ssd_in_kernel(z_ref, x_ref, mod_ref, wz_ref, wx0_ref, wx1_ref, wx2_ref, wdt_ref, wdtt_ref, cw_ref,
                   cb_ref, dtr_ref, alr_ref, dtc_ref, alc_ref,
                   zs_ref, xs_ref, bm_ref, cm_ref, da_ref, at_ref, dtt_ref, tail_ref, p_s):
    t = x_ref.shape[0]
    nh = at_ref.shape[0]

    @pl.when(pl.program_id(1) == 0)
    def _():
        tail_ref[...] = jnp.zeros_like(tail_ref)

    seq = pl.program_id(0)
    u = _modnorm(x_ref[...], _mrow(mod_ref, 1, seq), _mrow(mod_ref, 0, seq)).astype(BF16)

    def conv_fn(pre, cs):
        out = _conv_seq(pre, tail_ref[:, cs], cw_ref.at[:, cs], cb_ref[:, cs])
        tail_ref[:, cs] = pre[t - SUBLANES:, :]
        return out

    def decays():
        dt_c = jax.nn.softplus(jnp.dot(u, wdt_ref[...].astype(BF16), preferred_element_type=F32)
                               + dtr_ref[...])
        dt_r = jax.nn.softplus(lax.dot_general(wdtt_ref[...].astype(BF16), u, (((1,), (1,)), ((), ())),
                                               preferred_element_type=F32) + dtc_ref[...])
        acs_c = _sdot_r(_block_tri(t, SSD_CHUNK, True), dt_c * -jnp.exp(alr_ref[...]))
        acs_r = _sdot_l(dt_r * -jnp.exp(alc_ref[...]), _block_tri(t, SSD_CHUNK, False))
        lane = lax.broadcasted_iota(jnp.int32, dt_c.shape, 1)
        da_ref[...] = jnp.where(lane < nh, dt_c, pltpu.roll(acs_c, nh, 1))
        at_ref[...] = acs_r[:nh, :]
        dtt_ref[...] = dt_r[:nh, :]

    _ssd_proj(u, u, wz_ref, (wx0_ref, wx1_ref, wx2_ref), conv_fn, zs_ref, xs_ref, bm_ref, cm_ref, p_s, z_ref[0],
              decays)


def _ssd_in(x, mod, p):
    b, l, d = x.shape
    di, cd, nh = p["di"], p["cd"], p["nh"]
    gn = (cd - di) // 2
    t = ROW_TILE
    tok = lambda width: pl.BlockSpec((None, t, width), lambda i, c: (i, c, 0))
    w_arrs, w_specs = _unzip([p["wz"], *p["wx"], p["wdt"], p["wdtt"], p["cw"], p["cb"], p["dtr"],
                              p["alr"], p["dtc"], p["alc"]])
    rows_spec = pl.BlockSpec((None, nh, t), lambda i, c: (i, 0, c))
    zero, zero_spec = _zero_operand()
    return pl.pallas_call(
        _ssd_in_kernel,
        grid=(b, l // t),
        in_specs=[zero_spec, tok(d), _mod_operand(mod)[1]] + w_specs,
        out_specs=[tok(di), tok(di), tok(gn), tok(gn), tok(LANES), rows_spec, rows_spec,
                   pl.BlockSpec((None, SUBLANES, cd), lambda i, c: (i, 0, 0))],
        out_shape=[jax.ShapeDtypeStruct((b, l, di), F32), jax.ShapeDtypeStruct((b, l, di), F32),
                   jax.ShapeDtypeStruct((b, l, gn), F32), jax.ShapeDtypeStruct((b, l, gn), F32),
                   jax.ShapeDtypeStruct((b, l, LANES), F32),
                   jax.ShapeDtypeStruct((b, nh, l), F32), jax.ShapeDtypeStruct((b, nh, l), F32),
                   jax.ShapeDtypeStruct((b, SUBLANES, cd), F32)],
        scratch_shapes=[pltpu.VMEM((DRAIN_SLOTS, t, WBLK), F32)],
        compiler_params=_params("parallel", "arbitrary"),
        name="ssd_in",
    )(zero, x, mod[0], *w_arrs)


def _ssd_scan_kernel(zs_ref, xs_ref, bm_ref, cm_ref, da_ref, at_ref, dtt_ref, e2_ref, dsk_ref, ng_ref,
                     y_ref, hout_ref, ht_s):
    c = zs_ref.shape[0]
    di = zs_ref.shape[1]
    n = SSM_STATE
    pdim = SSM_HEADDIM
    gw = di // SSM_GROUPS
    hpg = gw // pdim
    nh = di // pdim
    ci = pl.program_id(1)

    @pl.when(ci == 0)
    def _():
        ht_s[...] = jnp.zeros_like(ht_s)

    da = da_ref[...]
    lane_a = lax.broadcasted_iota(jnp.int32, da.shape, 1)
    is_acs = (lane_a >= nh) & (lane_a < 2 * nh)
    acs = jnp.where(is_acs, da, 0.0)
    dt_on_acs = jnp.where(is_acs, pltpu.roll(da, nh, 1), 0.0)
    eacs_e = _sdot_l(jnp.where(is_acs, jnp.exp(acs), 0.0), e2_ref[...])
    dsdt_e = _sdot_l(jnp.exp(acs[c - 1:c, :] - acs) * dt_on_acs, e2_ref[...])
    xs = xs_ref[...]
    bms = [bm_ref[:, g * n:(g + 1) * n] for g in range(SSM_GROUPS)]
    cms = [cm_ref[:, g * n:(g + 1) * n] for g in range(SSM_GROUPS)]
    xd = xs * dsdt_e

    r = lax.broadcasted_iota(jnp.int32, (c, c), 0)
    cc = lax.broadcasted_iota(jnp.int32, (c, c), 1)
    causal = r >= cc
    lane = lax.broadcasted_iota(jnp.int32, (c, 2 * pdim), 1)

    groups = range(SSM_GROUPS)
    gsl = lambda g: slice(g * gw, (g + 1) * gw)
    nsl = lambda g: slice(g * n, (g + 1) * n)
    assert n == LANES
    cbs = [_bdot_nt(cms[g], bms[g]) for g in groups]
    hts = [ht_s[g] for g in groups]
    y_offs = [_bdot(cms[g], hts[g]) * eacs_e[:, gsl(g)] for g in groups]
    for g in groups:
        ht_s[g] = eacs_e[c - 1:c, gsl(g)] * hts[g] + _bdot(bms[g].T, xd[:, gsl(g)])
    for g in groups:
        y_parts = []
        for pr in range(hpg // 2):
            ms = []
            for hh in (2 * pr, 2 * pr + 1):
                h = g * hpg + hh
                seg = da[:, nh + h:nh + h + 1] - at_ref[h:h + 1, :]
                lm = jnp.where(causal, jnp.exp(jnp.where(causal, seg, 0.0)), 0.0)
                ms.append((cbs[g] * lm * dtt_ref[h:h + 1, :]).astype(BF16))
            ps = slice(g * gw + pr * 2 * pdim, g * gw + (pr + 1) * 2 * pdim)
            xp = xs[:, ps].astype(BF16)
            zero = jnp.zeros_like(xp)
            rhs = jnp.concatenate([jnp.where(lane < pdim, xp, zero), jnp.where(lane >= pdim, xp, zero)], axis=0)
            y_d = jnp.dot(jnp.concatenate(ms, axis=1), rhs, preferred_element_type=F32)
            os_ = slice(pr * 2 * pdim, (pr + 1) * 2 * pdim)
            y_parts.append(y_d + y_offs[g][:, os_] + dsk_ref[:, ps] * xs[:, ps])
        yz = jnp.concatenate(y_parts, axis=1) * zs_ref[:, gsl(g)]
        ms_ = jnp.mean(yz * yz, axis=-1, keepdims=True)
        y_ref[:, gsl(g)] = (yz * lax.rsqrt(ms_ + EPS) * ng_ref[:, gsl(g)]).astype(BF16)

    @pl.when(ci == pl.num_programs(1) - 1)
    def _():
        for g in groups:
            hout_ref[gsl(g), :] = ht_s[g].T


def _ssd_scan(zs, xs, bm, cm, da, at, dtt, p):
    b, l, di = zs.shape
    gn = bm.shape[-1]
    nh = p["nh"]
    gw = di // SSM_GROUPS
    t = SSD_CHUNK
    tok = lambda width: pl.BlockSpec((None, t, width), lambda i, c: (i, c, 0))
    rows_spec = pl.BlockSpec((None, nh, t), lambda i, c: (i, 0, c))
    w_arrs, w_specs = _unzip([p["e2"], p["dsk"], p["ng"]])
    return pl.pallas_call(
        _ssd_scan_kernel,
        grid=(b, l // t),
        in_specs=[tok(di), tok(di), tok(gn), tok(gn), tok(LANES), rows_spec, rows_spec] + w_specs,
        out_specs=[tok(di), pl.BlockSpec((None, di, SSM_STATE), lambda i, c: (i, 0, 0))],
        out_shape=[jax.ShapeDtypeStruct((b, l, di), BF16),
                   jax.ShapeDtypeStruct((b, di, SSM_STATE), F32)],
        scratch_shapes=[pltpu.VMEM((SSM_GROUPS, SSM_STATE, gw), F32)],
        compiler_params=_params("parallel", "arbitrary"),
        name="ssd_scan",
    )(zs, xs, bm, cm, da, at, dtt, *w_arrs)


def _ssd_dec_in_kernel(z_ref, x_ref, mod_ref, wz_ref, wx0_ref, wx1_ref, wx2_ref, wdt_ref, cw_ref, cb_ref,
                       dtr_ref, alr_ref, e1_ref, cs_ref,
                       zs_ref, xs_ref, bm_ref, cm_ref, xt_ref, ea_ref, xdt_ref, eae_ref, nb_ref, p_s):
    n = x_ref.shape[0]
    di = zs_ref.shape[-1]
    nh = di // SSM_HEADDIM
    u = _modnorm(x_ref[...], mod_ref[1], mod_ref[0]).astype(BF16)

    def conv_fn(pre, cs):
        return _conv_step(pre, cs_ref, nb_ref, cw_ref, cs, cb_ref[:, cs])

    _ssd_proj(u, u, wz_ref, (wx0_ref, wx1_ref, wx2_ref), conv_fn, zs_ref, xs_ref, bm_ref, cm_ref, p_s, z_ref[0])
    dt = jax.nn.softplus(jnp.dot(u, wdt_ref[...].astype(BF16), preferred_element_type=F32) + dtr_ref[...])
    e_a = jnp.exp(dt * -jnp.exp(alr_ref[...]))
    xdt_ref[...] = xs_ref[...] * _sdot_l(dt, e1_ref[...])
    eae_ref[...] = _sdot_l(e_a, e1_ref[...])
    for j in range(di // LANES):
        xt_ref[j * LANES:(j + 1) * LANES, :] = xdt_ref[:, j * LANES:(j + 1) * LANES].T
    for h in range(nh):
        ea_ref[h] = jnp.broadcast_to(e_a[:, h:h + 1], (n, LANES))


def _ssd_dec_in(x, mod, p, conv_state, j):
    n, d = x.shape
    di, cd, nh = p["di"], p["cd"], p["nh"]
    gn = (cd - di) // 2
    w_arrs, w_specs = _unzip([p["wz"], *p["wx"], p["wdt"], p["cw"], p["cb"], p["dtr"], p["alr"], p["e1"]])
    zero, zero_spec = _zero_operand()
    return pl.pallas_call(
        _ssd_dec_in_kernel,
        grid=(1,),
        in_specs=[zero_spec, _full((n, d)), _mod_operand(mod)[1]] + w_specs + [_state_spec(conv_state, j)],
        out_specs=[_full((n, di)), _full((n, di)), _full((n, gn)), _full((n, gn)), _full((di, n)),
                   _full((nh, n, LANES)), _full((n, di)), _full((n, di)), _full((n, 3, cd))],
        out_shape=[jax.ShapeDtypeStruct((n, di), F32), jax.ShapeDtypeStruct((n, di), F32),
                   jax.ShapeDtypeStruct((n, gn), F32), jax.ShapeDtypeStruct((n, gn), F32),
                   jax.ShapeDtypeStruct((di, n), F32),
                   jax.ShapeDtypeStruct((nh, n, LANES), F32),
                   jax.ShapeDtypeStruct((n, di), F32), jax.ShapeDtypeStruct((n, di), F32),
                   jax.ShapeDtypeStruct((n, 3, cd), F32)],
        scratch_shapes=[pltpu.VMEM((DRAIN_SLOTS, n, WBLK), F32)],
        compiler_params=_params("arbitrary"),
        name="ssd_decode_in",
    )(zero, x, mod[0], *w_arrs, conv_state)


def _ssd_dec_state_kernel(xt_ref, bm_ref, cm_ref, ea_ref, h0_ref, yo_ref, h_ref):
    n = h0_ref.shape[0]
    pdim = h0_ref.shape[2]
    for b in range(n):
        bb = bm_ref[b:b + 1, :]
        c8 = jnp.broadcast_to(cm_ref[b:b + 1, :], (SUBLANES, bb.shape[1]))
        outs = []
        for hh in range(2):
            h0 = h0_ref[b, hh]
            outs.append(_bdot_nt(c8, h0)[0:1, :])
            xc = xt_ref[hh * pdim:(hh + 1) * pdim, b:b + 1]
            h_ref[b, hh] = ea_ref[hh, b:b + 1, :] * h0 + xc * bb
        yo_ref[b:b + 1, :] = jnp.concatenate(outs, axis=1)


def _ssd_dec_state(xt, bm, cm, ea, h0, j):
    _, n, nh, pdim, ns = h0.shape
    hpg = nh // SSM_GROUPS
    return pl.pallas_call(
        _ssd_dec_state_kernel,
        grid=(nh // 2,),
        in_specs=[pl.BlockSpec((2 * pdim, n), lambda i: (i, 0)),
                  pl.BlockSpec((n, ns), lambda i: (0, (2 * i) // hpg)),
                  pl.BlockSpec((n, ns), lambda i: (0, (2 * i) // hpg)),
                  pl.BlockSpec((2, n, LANES), lambda i: (i, 0, 0)),
                  pl.BlockSpec((None, n, 2, pdim, ns), lambda i: (j, 0, i, 0, 0))],
        out_specs=[pl.BlockSpec((n, 2 * pdim), lambda i: (0, i)),
                   pl.BlockSpec((n, 2, pdim, ns), lambda i: (0, i, 0, 0))],
        out_shape=[jax.ShapeDtypeStruct((n, nh * pdim), F32), jax.ShapeDtypeStruct((n, nh, pdim, ns), F32)],
        compiler_params=_params("parallel"),
        name="ssd_decode_state",
    )(xt, bm, cm, ea, h0)


def _ssd_dec_out_kernel(yo_ref, xs_ref, bm_ref, cm_ref, zs_ref, eae_ref, xdt_ref, dsk_ref, ng_ref, y_ref):
    di = xs_ref.shape[1]
    gw = di // SSM_GROUPS
    n = SSM_STATE
    for g in range(SSM_GROUPS):
        gs = slice(g * gw, (g + 1) * gw)
        cb = jnp.sum(cm_ref[:, g * n:(g + 1) * n] * bm_ref[:, g * n:(g + 1) * n], axis=-1, keepdims=True)
        y = cb * xdt_ref[:, gs] + eae_ref[:, gs] * yo_ref[:, gs] + dsk_ref[:, gs] * xs_ref[:, gs]
        yz = y * zs_ref[:, gs]
        ms = jnp.mean(yz * yz, axis=-1, keepdims=True)
        y_ref[:, gs] = (yz * lax.rsqrt(ms + EPS) * ng_ref[:, gs]).astype(BF16)


def _ssd_dec_out(yo, xs, bm, cm, zs, eae, xdt, p):
    n, di = xs.shape
    gn = bm.shape[1]
    w_arrs, w_specs = _unzip([p["dsk"], p["ng"]])
    return pl.pallas_call(
        _ssd_dec_out_kernel,
        grid=(1,),
        in_specs=[_full((n, di)), _full((n, di)), _full((n, gn)), _full((n, gn)), _full((n, di)),
                  _full((n, di)), _full((n, di))] + w_specs,
        out_specs=_full((n, di)),
        out_shape=jax.ShapeDtypeStruct((n, di), BF16),
        compiler_params=_params("arbitrary"),
        name="ssd_decode_out",
    )(yo, xs, bm, cm, zs, eae, xdt, *w_arrs)


def _pad_lanes(v):
    return jnp.pad(v.astype(F32), (0, LANES - v.shape[0])).reshape(1, LANES)


def _pad_cols(w):
    return jnp.pad(w, ((0, 0), (0, LANES - w.shape[1])))


def _small(a):
    return _w(a, a.shape, (0,) * a.ndim)


def _rg_params(j, w_in, conv_w, conv_b, gate_w, gate_b, lam, w_out):
    nl, d, w2 = w_in.shape
    w = w2 // 2
    nblk = w // RG_BLOCK
    gate = (None, None, nblk, RG_BLOCK, RG_BLOCK)
    return dict(width=w,
                wy=_w(w_in, (None, d, w), (j, 0, 0)), wx=_w(w_in, (None, d, w), (j, 0, 1)),
                cw=_w(conv_w, (None, 4, w), (j, 0, 0)), cb=_w(conv_b.reshape(nl, 1, w), (None, 1, w), (j, 0, 0)),
                gwr=_w(gate_w, gate, (j, 0, 0, 0, 0)), gwi=_w(gate_w, gate, (j, 1, 0, 0, 0)),
                gb=_w(gate_b, (None, 2, w), (j, 0, 0)), lam=_w(lam.reshape(nl, 1, w), (None, 1, w), (j, 0, 0)),
                wo=_w(w_out, (None, w, d), (j, 0, 0)))


def _gdn_params(j, w_in, w_in_f32, conv_w, a_log, dt_bias, norm_g, w_out):
    nl, d, _ = w_in.shape
    qkv = conv_w.shape[-1]
    hk = qkv // 3
    wab = _pad_cols(w_in_f32[j, :, qkv + hk:])
    alr, dtr = _pad_lanes(a_log[j]), _pad_lanes(dt_bias[j])
    return dict(hk=hk,
                wqkv=_w(w_in, (None, d, qkv), (j, 0, 0)), wg=_w(w_in, (None, d, hk), (j, 0, qkv // hk)),
                wab=_small(wab), wabt=_small(wab.T), cw=_w(conv_w, (None, 4, qkv), (j, 0, 0)),
                alr=_small(alr), dtr=_small(dtr), alc=_small(alr.T), dtc=_small(dtr.T),
                ng=_w(norm_g.reshape(nl, 1, GDN_DV), (None, 1, GDN_DV), (j, 0, 0)),
                wo=_w(w_out, (None, hk, d), (j, 0, 0)))


def _ssd_params(j, w_in, w_in_f32, conv_w, conv_b, a_log, dt_bias, d_skip, norm_g, w_out):
    nl, d, _ = w_in.shape
    cd = conv_w.shape[-1]
    nh = a_log.shape[-1]
    di = nh * SSM_HEADDIM
    assert di % WBLK == 0 and (cd - di) == WBLK
    wdt = _pad_cols(w_in_f32[j, :, di + cd:])
    alr, dtr = _pad_lanes(a_log[j]), _pad_lanes(dt_bias[j])
    head_of = jnp.arange(di, dtype=jnp.int32) // SSM_HEADDIM
    rows = jnp.arange(LANES, dtype=jnp.int32)[:, None]
    e1 = (rows == head_of[None, :]).astype(BF16)
    e2 = (rows == head_of[None, :] + nh).astype(BF16)
    return dict(di=di, cd=cd, nh=nh,
                wz=_w(w_in, (None, d, di), (j, 0, 0)),
                wx=[_w(w_in, (None, d, WBLK), (j, 0, di // WBLK + i)) for i in range(cd // WBLK)],
                wdt=_small(wdt), wdtt=_small(wdt.T), cw=_w(conv_w, (None, 4, cd), (j, 0, 0)),
                cb=_w(conv_b.reshape(nl, 1, cd), (None, 1, cd), (j, 0, 0)),
                alr=_small(alr), dtr=_small(dtr), alc=_small(alr.T), dtc=_small(dtr.T),
                e1=_small(e1), e2=_small(e2),
                dsk=_small(jnp.repeat(d_skip[j], SSM_HEADDIM).reshape(1, di)),
                ng=_w(norm_g.reshape(nl, 1, di), (None, 1, di), (j, 0, 0)),
                wo=_w(w_out, (None, di, d), (j, 0, 0)))


def kernel(x_prompt, x_sample, state_rglru_conv, state_rglru_h, state_gdn_conv, state_gdn_S, state_ssd_conv, state_ssd_h, c_prompt, c_sample, w_mod, b_mod, w_mlp_up, w_mlp_down, final_norm_g, rg_w_in, rg_conv_w, rg_conv_b, rg_gate_w, rg_gate_b, rg_lambda, rg_w_out, gdn_w_in, gdn_conv_w, gdn_A_log, gdn_dt_bias, gdn_norm_g, gdn_w_out, ssd_w_in, ssd_conv_w, ssd_conv_b, ssd_A_log, ssd_dt_bias, ssd_D, ssd_norm_g, ssd_w_out):
    bp, l, d = x_prompt.shape
    ns = x_sample.shape[0]
    hid = w_mlp_up.shape[-1]
    assert x_sample.shape[1] == 1 and l % ROW_TILE == 0 and ns % SUBLANES == 0 and bp % GDN_SEQS == 0

    assert ns % bp == 0 and bp % SUBLANES == 0
    mod = _modulation(jnp.concatenate([c_sample, c_prompt], axis=0), w_mod, b_mod)
    mod_p = [(mod, i, bp, ns) for i in range(DEPTH)]
    mod_s = [(mod, i, ns, 0) for i in range(DEPTH)]
    fg = _small(final_norm_g.reshape(1, d))

    w_up, w_down = w_mlp_up, w_mlp_down
    rg_in, rg_gate, rg_out = rg_w_in.astype(BF16), rg_gate_w.astype(BF16), rg_w_out.astype(BF16)
    gdn_in, gdn_out = gdn_w_in.astype(BF16), gdn_w_out.astype(BF16)
    ssd_in, ssd_out = ssd_w_in.astype(BF16), ssd_w_out.astype(BF16)

    xp = x_prompt
    xs = x_sample.reshape(ns, d)
    tails = lambda t: t[:, SUBLANES - 3:, :]
    tails_interleaved = lambda t: t[:, SUBLANES - 1::SUBLANES, :]
    out = {k: [] for k in ("p_rg_conv", "p_rg_h", "p_gdn_conv", "p_gdn_S", "p_ssd_conv", "p_ssd_h",
                           "s_rg_conv", "s_rg_h", "s_gdn_conv", "s_gdn_S", "s_ssd_conv", "s_ssd_h")}
    for i in range(DEPTH):
        j = i // N_MIXERS
        kind = i % N_MIXERS
        if kind == 0:
            p = _rg_params(j, rg_in, rg_conv_w, rg_conv_b, rg_gate, rg_gate_b, rg_lambda, rg_out)
            yp, tail, h_last = _rg_prompt(xp, mod_p[i], p)
            out["p_rg_conv"].append(tails_interleaved(tail))
            out["p_rg_h"].append(h_last[:, 0, :])
            ysm, nb, h_new = _rg_decode(xs, mod_s[i], p, state_rglru_conv, state_rglru_h, j)
            out["s_rg_conv"].append(nb)
            out["s_rg_h"].append(h_new)
        elif kind == 1:
            p = _gdn_params(j, gdn_in, gdn_w_in, gdn_conv_w, gdn_A_log, gdn_dt_bias, gdn_norm_g, gdn_out)
            q, k, v, gz, gb, gr, tail = _gdn_in(xp, mod_p[i], p)
            yp, s_fin = _gdn_scan(q, k, v, gz, gb, gr, p["ng"])
            out["p_gdn_conv"].append(tails(tail))
            out["p_gdn_S"].append(s_fin)
            qt, kt, v1, gz1, eg, be, nb = _gdn_dec_in(xs, mod_s[i], p, state_gdn_conv, j)
            ysm, s_new = _gdn_dec_state(qt, kt, v1, gz1, eg, be, p["ng"], state_gdn_S, j)
            out["s_gdn_conv"].append(nb)
            out["s_gdn_S"].append(s_new)
        else:
            p = _ssd_params(j, ssd_in, ssd_w_in, ssd_conv_w, ssd_conv_b, ssd_A_log, ssd_dt_bias, ssd_D, ssd_norm_g,
                            ssd_out)
            zs, xc, bm, cm, da, at, dtt, tail = _ssd_in(xp, mod_p[i], p)
            yp, h_fin = _ssd_scan(zs, xc, bm, cm, da, at, dtt, p)
            out["p_ssd_conv"].append(tails(tail))
            out["p_ssd_h"].append(h_fin.reshape(bp, -1, SSM_HEADDIM, SSM_STATE))
            zs1, xc1, bm1, cm1, xt, ea, xdt, eae, nb = _ssd_dec_in(xs, mod_s[i], p, state_ssd_conv, j)
            yo, h_new = _ssd_dec_state(xt, bm1, cm1, ea, state_ssd_h, j)
            ysm = _ssd_dec_out(yo, xc1, bm1, cm1, zs1, eae, xdt, p)
            out["s_ssd_conv"].append(nb)
            out["s_ssd_h"].append(h_new)
        final = i == DEPTH - 1
        weights = [p["wo"], _w(w_up, (None, d, hid), (i, 0, 0)), _w(w_down, (None, hid, d), (i, 0, 0)), fg]
        y_rows = yp if yp.ndim == 4 else yp.reshape(bp * l, -1)
        xp = _outproj_mlp(xp.reshape(bp * l, d), y_rows, mod_p[i], weights,
                          tm=ROW_TILE, rows_per_mod=l, final=final).reshape(bp, l, d)
        xs = _outproj_mlp(xs, ysm, mod_s[i], weights, tm=ns, rows_per_mod=1, final=final)

    st = {k: jnp.stack(v) for k, v in out.items()}
    return (xp, xs.reshape(ns, 1, d),
            st["p_rg_conv"], st["p_rg_h"], st["p_gdn_conv"], st["p_gdn_S"], st["p_ssd_conv"], st["p_ssd_h"],
            st["s_rg_conv"], st["s_rg_h"], st["s_gdn_conv"], st["s_gdn_S"], st["s_ssd_conv"], st["s_ssd_h"])
```

```python
import functools

import jax
import jax.numpy as jnp
from jax import lax
from jax.experimental import pallas as pl
from jax.experimental.pallas import tpu as pltpu

F32 = jnp.float32
BF16 = jnp.bfloat16

DEPTH = 4
N_MIXERS = 3
EPS = 1e-6
RG_C = 8.0
RG_BLOCK = 256
GDN_DK = 128
GDN_DV = 128
SSM_HEADDIM = 64
SSM_STATE = 128
SSM_GROUPS = 4

SUBLANES = 8
LANES = 128

VMEM_LIMIT = 56 * 1024 * 1024
ROW_TILE = 512
GDN_CHUNK = 64
GDN_STEP = 256
GDN_SEQS = 1
SSD_CHUNK = 128
MLP_HCHUNK = 1024
WBLK = 1024
TAIL_ROWS = 3 * SUBLANES


def _params(*sem):
    return pltpu.CompilerParams(dimension_semantics=sem, vmem_limit_bytes=VMEM_LIMIT)


def _w(arr, blk, idx):
    return arr, pl.BlockSpec(blk, lambda *_: idx, pipeline_mode=pl.Buffered(1))


def _unzip(pairs):
    return [a for a, _ in pairs], [s for _, s in pairs]


def _bdot(a, b):
    return jnp.dot(a.astype(BF16), b.astype(BF16), preferred_element_type=F32)


def _bdot_nt(a, b):
    return lax.dot_general(a.astype(BF16), b.astype(BF16), (((1,), (1,)), ((), ())),
                           preferred_element_type=F32)


def _split(a):
    hi = a.astype(BF16)
    lo = (a - hi.astype(F32)).astype(BF16)
    return hi, lo


def _sdot_l(a, b_exact):
    hi, lo = _split(a)
    return (jnp.dot(hi, b_exact, preferred_element_type=F32)
            + jnp.dot(lo, b_exact, preferred_element_type=F32))


def _sdot_r(a_exact, b):
    hi, lo = _split(b)
    return (jnp.dot(a_exact, hi, preferred_element_type=F32)
            + jnp.dot(a_exact, lo, preferred_element_type=F32))


def _modnorm(x, sc, sh):
    ms = jnp.mean(x * x, axis=-1, keepdims=True)
    return x * lax.rsqrt(ms + EPS) * (1.0 + sc) + sh


def _sigmoid(x):
    return 0.5 * jnp.tanh(0.5 * x) + 0.5


def _silu(x):
    hx = 0.5 * x
    return hx * jnp.tanh(hx) + hx


def _conv_seq(xpre, tail, w_ref, bias):
    xcat = jnp.concatenate([tail, xpre], axis=0)
    acc = xpre * w_ref[3:4, :]
    for k in (1, 2, 3):
        shifted = pltpu.roll(xcat, k, 0)[SUBLANES:]
        acc = acc + shifted * w_ref[3 - k:4 - k, :]
    if bias is not None:
        acc = acc + bias
    return acc


def _interleave_rows(u_t):
    t = u_t.shape[0]
    r = lax.broadcasted_iota(jnp.int32, (t, t), 0)
    c = lax.broadcasted_iota(jnp.int32, (t, t), 1)
    perm = jnp.where(c == (r % SUBLANES) * (t // SUBLANES) + r // SUBLANES, 1.0, 0.0).astype(BF16)
    return jnp.dot(perm, u_t, preferred_element_type=F32).astype(BF16)


def _conv_interleaved(pre, tail_ref, cs, w_ref, bias):
    ng = pre.shape[0] // SUBLANES
    ntail = tail_ref.shape[0] // SUBLANES
    sub = lax.broadcasted_iota(jnp.int32, (SUBLANES, pre.shape[1]), 0)
    grp = lambda a, g: a[g * SUBLANES:(g + 1) * SUBLANES]
    acc = pre * w_ref[3:4, cs]
    if bias is not None:
        acc = acc + bias
    for k in (1, 2, 3):
        top = [jnp.where(sub == 0,
                         pltpu.roll(tail_ref[(ntail - k + i) * SUBLANES:(ntail - k + i + 1) * SUBLANES, cs], 1, 0),
                         pltpu.roll(grp(pre, ng - k + i), 1, 0)) for i in range(k)]
        shifted = jnp.concatenate(top + [pre[:(ng - k) * SUBLANES]], axis=0)
        acc = acc + shifted * w_ref[3 - k:4 - k, cs]
    tail_ref[:, cs] = pre[(ng - ntail) * SUBLANES:, :]
    return acc


def _conv_step(pre, cs_ref, nb_ref, w_ref, cs, bias):
    nb_ref[:, 0, cs] = cs_ref[:, 1, cs]
    nb_ref[:, 1, cs] = cs_ref[:, 2, cs]
    nb_ref[:, 2, cs] = pre
    out = (cs_ref[:, 0, cs] * w_ref[0:1, cs] + cs_ref[:, 1, cs] * w_ref[1:2, cs]
           + cs_ref[:, 2, cs] * w_ref[2:3, cs] + pre * w_ref[3:4, cs])
    return out if bias is None else out + bias


DRAIN_SLOTS = 2
RG_DRAIN_SLOTS = 3


def _zero_operand():
    return jnp.zeros((1,), jnp.int32), pl.BlockSpec(memory_space=pltpu.SMEM)


def _pipelined(dots, consume, buf, zero, after_first=None, side_after=0):
    n = len(dots)
    slots = buf.shape[0]
    ahead = slots - 1
    for j in range(min(ahead, n)):
        buf[zero + j % slots] = dots[j]()
    for j in range(n):
        if j + ahead < n:
            buf[zero + (j + ahead) % slots] = dots[j + ahead]()
        consume(j, buf[zero + j % slots])
        if j == side_after and after_first is not None:
            after_first()


def _block_tri(n, chunk, lower):
    r = lax.broadcasted_iota(jnp.int32, (n, n), 0)
    c = lax.broadcasted_iota(jnp.int32, (n, n), 1)
    same = (r // chunk) == (c // chunk)
    tri = (c <= r) if lower else (r <= c)
    return jnp.where(same & tri, 1.0, 0.0).astype(BF16)


def _mod_kernel(c_ref, w_ref, b_ref, o_ref):
    o_ref[...] = _bdot(_silu(c_ref[...]), w_ref[...]) + b_ref[...]


def _modulation(c_all, w_mod, b_mod):
    nb, d = c_all.shape
    n6 = w_mod.shape[-1]
    return pl.pallas_call(
        _mod_kernel,
        grid=(DEPTH, n6 // d),
        in_specs=[pl.BlockSpec((nb, d), lambda l, n: (0, 0)),
                  pl.BlockSpec((None, d, d), lambda l, n: (l, 0, n)),
                  pl.BlockSpec((None, 1, d), lambda l, n: (l, 0, n))],
        out_specs=pl.BlockSpec((None, None, nb, d), lambda l, n: (l, n, 0, 0)),
        out_shape=jax.ShapeDtypeStruct((DEPTH, n6 // d, nb, d), F32),
        compiler_params=_params("parallel", "parallel"),
        name="modulation",
    )(c_all, w_mod, b_mod.reshape(DEPTH, 1, n6))


def _mod_operand(mod):
    arr, layer, n_rows, first_row = mod
    return arr, pl.BlockSpec((None, arr.shape[1], n_rows, arr.shape[3]),
                             lambda *_: (layer, 0, first_row // n_rows, 0))


def _mrow(mod_ref, k, row):
    return mod_ref[k] if row is None else mod_ref[k, pl.ds(row, 1), :]


def _time_rows(blk_ref):
    ncb, t, _ = blk_ref.shape
    ng = t // SUBLANES
    return jnp.concatenate(
        [jnp.concatenate([blk_ref[cb, pl.ds(s, ng, stride=SUBLANES), :] for cb in range(ncb)], axis=1)
         for s in range(SUBLANES)], axis=0)


def _outproj_mlp_kernel(x_ref, y_ref, mod_ref, wo_ref, wu_ref, wd_ref, fg_ref, o_ref, *, final, tiles_per_seq):
    row = None if tiles_per_seq is None else pl.program_id(0) // tiles_per_seq
    y = y_ref[...] if len(y_ref.shape) == 2 else _time_rows(y_ref).astype(BF16)
    x1 = x_ref[...] + _mrow(mod_ref, 2, row) * jnp.dot(y, wo_ref[...], preferred_element_type=F32)
    u = _modnorm(x1, _mrow(mod_ref, 4, row), _mrow(mod_ref, 3, row)).astype(BF16)
    hidden = wu_ref.shape[1]
    acc = None
    for c in range(hidden // MLP_HCHUNK):
        cs = slice(c * MLP_HCHUNK, (c + 1) * MLP_HCHUNK)
        h = jnp.maximum(jnp.dot(u, wu_ref[:, cs].astype(BF16), preferred_element_type=F32), 0.0)
        p = jnp.dot((h * h).astype(BF16), wd_ref[cs, :].astype(BF16), preferred_element_type=F32)
        acc = p if acc is None else acc + p
    x2 = x1 + _mrow(mod_ref, 5, row) * acc
    if final:
        ms = jnp.mean(x2 * x2, axis=-1, keepdims=True)
        x2 = x2 * lax.rsqrt(ms + EPS) * fg_ref[...]
    o_ref[...] = x2


def _outproj_mlp(x, y, mod, weights, *, tm, rows_per_mod, final):
    n, d = x.shape
    per = None if rows_per_mod == 1 else rows_per_mod // tm
    assert per is not None or n == tm
    mod_arr, mod_spec = _mod_operand(mod)
    if y.ndim == 2:
        y_spec = pl.BlockSpec((tm, y.shape[1]), lambda i: (i, 0))
    else:
        assert tm == ROW_TILE and rows_per_mod == y.shape[2]
        y_spec = pl.BlockSpec((None, y.shape[1], tm, LANES), lambda i: (i // per, 0, i % per, 0))
    w_arrs, w_specs = _unzip(weights)
    return pl.pallas_call(
        functools.partial(_outproj_mlp_kernel, final=final, tiles_per_seq=per),
        grid=(n // tm,),
        in_specs=[pl.BlockSpec((tm, d), lambda i: (i, 0)), y_spec, mod_spec] + w_specs,
        out_specs=pl.BlockSpec((tm, d), lambda i: (i, 0)),
        out_shape=jax.ShapeDtypeStruct((n, d), F32),
        compiler_params=_params("parallel"),
        name="outproj_mlp",
    )(x, y, mod_arr, *w_arrs)


def _rg_gate_block(n, xbr, gwr_ref, gwi_ref, gb_ref, logsig, a_s, b_s):
    sl = slice(n * RG_BLOCK, (n + 1) * RG_BLOCK)
    xb = xbr.astype(BF16)
    gr = jnp.dot(xb, gwr_ref[n], preferred_element_type=F32)
    gi = jnp.dot(xb, gwi_ref[n], preferred_element_type=F32)
    log_a = (RG_C * logsig[:, sl]) * _sigmoid(gr + gb_ref[0:1, sl])
    a_s[:, sl] = jnp.exp(log_a)
    th = jnp.tanh(log_a)
    z = -2.0 * th / (1.0 - th)
    root = jnp.where(z > 0.0, z * lax.rsqrt(z), 0.0)
    b_s[:, sl] = root * _sigmoid(gi + gb_ref[1:2, sl]) * xbr


def _rg_gates(xbr, gwr_ref, gwi_ref, gb_ref, logsig, a_s, b_s):
    for n in range(xbr.shape[1] // RG_BLOCK):
        _rg_gate_block(n, xbr[:, n * RG_BLOCK:(n + 1) * RG_BLOCK], gwr_ref, gwi_ref, gb_ref, logsig, a_s, b_s)


def _rg_prompt_kernel(z_ref, x_ref, mod_ref, wy_ref, wx_ref, cw_ref, cb_ref, gwr_ref, gwi_ref, gb_ref,
                      lam_ref, y_ref, tail_ref, h_ref, a_s, b_s, y_s, p_s):
    t, w = a_s.shape
    ng = t // SUBLANES
    ncb = w // LANES

    @pl.when(pl.program_id(1) == 0)
    def _():
        tail_ref[...] = jnp.zeros_like(tail_ref)
        h_ref[...] = jnp.zeros_like(h_ref)

    seq = pl.program_id(0)
    u_t = _modnorm(x_ref[...], _mrow(mod_ref, 1, seq), _mrow(mod_ref, 0, seq)).astype(BF16)
    u = _interleave_rows(u_t)
    logsig = jax.nn.log_sigmoid(lam_ref[...])
    nblk = w // RG_BLOCK
    order = [(kind, n) for n in range(nblk) for kind in ("x", "y")]

    def consume(i, pre):
        kind, n = order[i]
        cs = slice(n * RG_BLOCK, (n + 1) * RG_BLOCK)
        if kind == "y":
            y_s[:, cs] = jax.nn.gelu(pre)
            return
        acc = _conv_interleaved(pre, tail_ref, cs, cw_ref, cb_ref[:, cs])
        _rg_gate_block(n, acc, gwr_ref, gwi_ref, gb_ref, logsig, a_s, b_s)

    def dot_of(kind, n):
        w_ref = wx_ref if kind == "x" else wy_ref
        return jnp.dot(u, w_ref[:, n * RG_BLOCK:(n + 1) * RG_BLOCK], preferred_element_type=F32)

    _pipelined([functools.partial(dot_of, *blk) for blk in order], consume, p_s, z_ref[0])

    def compose(gi, carry):
        ca, cbb = carry
        r0 = pl.multiple_of(gi * SUBLANES, SUBLANES)
        a = a_s[pl.ds(r0, SUBLANES), :]
        ca = a * ca
        cbb = a * cbb + b_s[pl.ds(r0, SUBLANES), :]
        a_s[pl.ds(r0, SUBLANES), :] = ca
        b_s[pl.ds(r0, SUBLANES), :] = cbb
        return ca, cbb

    a_end, b_end = lax.fori_loop(0, ng, compose, (jnp.ones((SUBLANES, w), F32), jnp.zeros((SUBLANES, w), F32)),
                                 unroll=4)
    h_in = jnp.zeros((SUBLANES, w), F32)
    for _ in range(SUBLANES):
        h_in = jnp.where(lax.broadcasted_iota(jnp.int32, (SUBLANES, w), 0) == 0, h_ref[...],
                         pltpu.roll(a_end * h_in + b_end, 1, 0))
    h_ref[...] = (a_end * h_in + b_end)[SUBLANES - 1:, :]
    hy = (a_s[...] * jnp.concatenate([h_in] * ng, axis=0) + b_s[...]) * y_s[...]
    for cb in range(ncb):
        y_ref[cb] = hy[:, cb * LANES:(cb + 1) * LANES]


def _rg_weights(p):
    return [p["wy"], p["wx"], p["cw"], p["cb"], p["gwr"], p["gwi"], p["gb"], p["lam"]]


def _rg_prompt(x, mod, p):
    b, l, d = x.shape
    w = p["width"]
    t = ROW_TILE
    w_arrs, w_specs = _unzip(_rg_weights(p))
    ncb = w // LANES
    ntail = (4 - 1) * SUBLANES
    zero, zero_spec = _zero_operand()
    return pl.pallas_call(
        _rg_prompt_kernel,
        grid=(b, l // t),
        in_specs=[zero_spec, pl.BlockSpec((None, t, d), lambda i, c: (i, c, 0)),
                  _mod_operand(mod)[1]] + w_specs,
        out_specs=[pl.BlockSpec((None, ncb, t, LANES), lambda i, c: (i, 0, c, 0)),
                   pl.BlockSpec((None, ntail, w), lambda i, c: (i, 0, 0)),
                   pl.BlockSpec((None, 1, w), lambda i, c: (i, 0, 0))],
        out_shape=[jax.ShapeDtypeStruct((b, ncb, l, LANES), F32),
                   jax.ShapeDtypeStruct((b, ntail, w), F32),
                   jax.ShapeDtypeStruct((b, 1, w), F32)],
        scratch_shapes=[pltpu.VMEM((t, w), F32)] * 3 + [pltpu.VMEM((RG_DRAIN_SLOTS, t, RG_BLOCK), F32)],
        compiler_params=_params("parallel", "arbitrary"),
        name="rglru_prompt",
    )(zero, x, mod[0], *w_arrs)


def _rg_decode_kernel(x_ref, mod_ref, wy_ref, wx_ref, cw_ref, cb_ref, gwr_ref, gwi_ref, gb_ref,
                      lam_ref, cs_ref, h0_ref, y_ref, nb_ref, h_ref, a_s, b_s):
    u = _modnorm(x_ref[...], mod_ref[1], mod_ref[0]).astype(BF16)
    xpre = jnp.dot(u, wx_ref[...], preferred_element_type=F32)
    xbr = _conv_step(xpre, cs_ref, nb_ref, cw_ref, slice(None), cb_ref[...])
    _rg_gates(xbr, gwr_ref, gwi_ref, gb_ref, jax.nn.log_sigmoid(lam_ref[...]), a_s, b_s)
    h = a_s[...] * h0_ref[...] + b_s[...]
    h_ref[...] = h
    y_br = jax.nn.gelu(jnp.dot(u, wy_ref[...], preferred_element_type=F32))
    y_ref[...] = (h * y_br).astype(BF16)


def _full(shape):
    return pl.BlockSpec(shape, lambda *_: (0,) * len(shape))


def _rg_decode(x, mod, p, conv_state, h0, j):
    n, d = x.shape
    w = p["width"]
    w_arrs, w_specs = _unzip(_rg_weights(p))
    return pl.pallas_call(
        _rg_decode_kernel,
        grid=(1,),
        in_specs=[_full((n, d)), _mod_operand(mod)[1]] + w_specs
        + [_state_spec(conv_state, j), _state_spec(h0, j)],
        out_specs=[_full((n, w)), _full((n, 3, w)), _full((n, w))],
        out_shape=[jax.ShapeDtypeStruct((n, w), BF16),
                   jax.ShapeDtypeStruct((n, 3, w), F32),
                   jax.ShapeDtypeStruct((n, w), F32)],
        scratch_shapes=[pltpu.VMEM((n, w), F32), pltpu.VMEM((n, w), F32)],
        compiler_params=_params("arbitrary"),
        name="rglru_decode",
    )(x, mod[0], *w_arrs, conv_state, h0)


def _gdn_qkvg(u, u_gate, wqkv_ref, wg_ref, conv_fn, q_ref, k_ref, v_ref, gz_ref, buf, zero, after_first=None):
    hk = gz_ref.shape[-1]
    nh = hk // GDN_DK
    dsts = ((q_ref, GDN_DK ** -0.5), (k_ref, 1.0), (v_ref, None))
    order = (0, None, 1, 2)

    def consume(i, pre):
        j = order[i]
        if j is None:
            gz_ref[...] = _silu(pre)
            return
        dst, scale = dsts[j]
        act = _silu(conv_fn(pre, j))
        for h in range(nh):
            xh = act[:, h * GDN_DK:(h + 1) * GDN_DK]
            if scale is not None:
                ss = jnp.sum(xh * xh, axis=-1, keepdims=True)
                xh = xh * (lax.rsqrt(ss + EPS) * scale)
            dst[:, h * GDN_DK:(h + 1) * GDN_DK] = xh

    def dot_of(j):
        if j is None:
            return jnp.dot(u_gate, wg_ref[...], preferred_element_type=F32)
        return jnp.dot(u, wqkv_ref[:, j * hk:(j + 1) * hk], preferred_element_type=F32)

    _pipelined([functools.partial(dot_of, j) for j in order], consume, buf, zero, after_first)


def _gdn_in_kernel(z_ref, x_ref, mod_ref, wqkv_ref, wg_ref, wab_ref, wabt_ref, cw_ref, alr_ref, dtr_ref,
                   alc_ref, dtc_ref, q_ref, k_ref, v_ref, gz_ref, gb_ref, gr_ref, tail_ref, p_s):
    t = x_ref.shape[0]
    hk = gz_ref.shape[-1]
    nh = hk // GDN_DK

    @pl.when(pl.program_id(1) == 0)
    def _():
        tail_ref[...] = jnp.zeros_like(tail_ref)

    seq = pl.program_id(0)
    u = _modnorm(x_ref[...], _mrow(mod_ref, 1, seq), _mrow(mod_ref, 0, seq)).astype(BF16)

    def conv_fn(pre, j):
        cs = slice(j * hk, (j + 1) * hk)
        out = _conv_seq(pre, tail_ref[:, cs], cw_ref.at[:, cs], None)
        tail_ref[:, cs] = pre[t - SUBLANES:, :]
        return out

    def decays():
        ab = jnp.dot(u, wab_ref[...].astype(BF16), preferred_element_type=F32)
        abt = lax.dot_general(wabt_ref[...].astype(BF16), u, (((1,), (1,)), ((), ())),
                              preferred_element_type=F32)
        g_col = -jnp.exp(alr_ref[...]) * jax.nn.softplus(ab + dtr_ref[...])
        g_row = -jnp.exp(alc_ref[...]) * jax.nn.softplus(abt + dtc_ref[...])
        gc = _sdot_r(_block_tri(t, GDN_CHUNK, True), g_col)
        gr = _sdot_l(g_row, _block_tri(t, GDN_CHUNK, False))
        lane = lax.broadcasted_iota(jnp.int32, ab.shape, 1)
        gb_ref[...] = jnp.where(lane < nh, gc, _sigmoid(ab))
        gr_ref[...] = gr[:SUBLANES, :]

    _gdn_qkvg(u, u, wqkv_ref, wg_ref, conv_fn, q_ref, k_ref, v_ref, gz_ref, p_s, z_ref[0], decays)


def _gdn_in(x, mod, p):
    b, l, d = x.shape
    hk = p["hk"]
    t = ROW_TILE
    tok = lambda width: pl.BlockSpec((None, t, width), lambda i, c: (i, c, 0))
    w_arrs, w_specs = _unzip([p["wqkv"], p["wg"], p["wab"], p["wabt"], p["cw"], p["alr"], p["dtr"],
                              p["alc"], p["dtc"]])
    zero, zero_spec = _zero_operand()
    return pl.pallas_call(
        _gdn_in_kernel,
        grid=(b, l // t),
        in_specs=[zero_spec, tok(d), _mod_operand(mod)[1]] + w_specs,
        out_specs=[tok(hk), tok(hk), tok(hk), tok(hk), tok(LANES),
                   pl.BlockSpec((None, SUBLANES, t), lambda i, c: (i, 0, c)),
                   pl.BlockSpec((None, SUBLANES, 3 * hk), lambda i, c: (i, 0, 0))],
        out_shape=[jax.ShapeDtypeStruct((b, l, hk), F32)] * 4
        + [jax.ShapeDtypeStruct((b, l, LANES), F32),
           jax.ShapeDtypeStruct((b, SUBLANES, l), F32),
           jax.ShapeDtypeStruct((b, SUBLANES, 3 * hk), F32)],
        scratch_shapes=[pltpu.VMEM((DRAIN_SLOTS, t, hk), F32)],
        compiler_params=_params("parallel", "arbitrary"),
        name="gdn_in",
    )(zero, x, mod[0], *w_arrs)


def _gdn_scan_kernel(q_ref, k_ref, v_ref, gz_ref, gb_ref, gr_ref, ng_ref, y_ref, s_ref):
    c = GDN_CHUNK
    nseq, nh = s_ref.shape[0], s_ref.shape[1]

    @pl.when(pl.program_id(1) == 0)
    def _():
        s_ref[...] = jnp.zeros_like(s_ref)

    r = lax.broadcasted_iota(jnp.int32, (c, c), 0)
    cc = lax.broadcasted_iota(jnp.int32, (c, c), 1)
    causal = r >= cc
    strict = r > cc
    eye = jnp.where(r == cc, 1.0, 0.0)
    merge = []
    blk = 1
    while blk < c:
        merge.append((r // (2 * blk) == cc // (2 * blk)) & ((r // blk) % 2 == 1) & ((cc // blk) % 2 == 0))
        blk *= 2

    nchunk = gz_ref.shape[1] // c
    trip = [(b, j, h) for b in range(nseq) for j in range(nchunk) for h in range(nh)]
    rows = lambda j: slice(j * c, (j + 1) * c)
    cols = lambda h: slice(h * GDN_DK, (h + 1) * GDN_DK)
    g_col = lambda b, j, h: gb_ref[b, rows(j), h:h + 1]
    beta_of = lambda b, j, h: gb_ref[b, rows(j), nh + h:nh + h + 1]
    g_row = lambda b, j, h: gr_ref[b, h:h + 1, rows(j)]
    blk_of = lambda ref, b, j, h: ref[b, rows(j), cols(h)]

    kks = [_bdot_nt(jnp.concatenate([blk_of(k_ref, *t) * beta_of(*t), blk_of(q_ref, *t)], axis=0),
                    blk_of(k_ref, *t)) for t in trip]
    a_s, qk_s = [], []
    for t, kk in zip(trip, kks):
        decay = jnp.where(causal, jnp.exp(jnp.where(causal, g_col(*t) - g_row(*t), 0.0)), 0.0)
        a_s.append(jnp.where(strict, kk[:c] * decay, 0.0))
        qk_s.append((kk[c:] * decay).astype(BF16))
    xs = [eye - jnp.where(merge[0], a, 0.0) for a in a_s]
    for m in merge[1:]:
        ts = [_bdot(jnp.where(m, a, 0.0), x) for a, x in zip(a_s, xs)]
        xs = [x - _bdot(x, t_) for x, t_ in zip(xs, ts)]
    uws = {}
    for t, x in zip(trip, xs):
        kb = blk_of(k_ref, *t) * beta_of(*t)
        rhs = jnp.concatenate([blk_of(v_ref, *t) * beta_of(*t), kb * jnp.exp(g_col(*t))], axis=1)
        uws[t] = _bdot(x, rhs)
    qks = dict(zip(trip, qk_s))

    for j in range(nchunk):
        cur = [(b, j, h) for b in range(nseq) for h in range(nh)]
        states = {t: s_ref[t[0], t[2]] for t in cur}
        ws_qs = {t: _bdot(jnp.concatenate([uws[t][:, GDN_DV:], blk_of(q_ref, *t) * jnp.exp(g_col(*t))],
                                          axis=0), states[t]) for t in cur}
        v_news = {t: uws[t][:, :GDN_DV] - ws_qs[t][:c] for t in cur}
        outs = {t: ws_qs[t][c:] + jnp.dot(qks[t], v_news[t].astype(BF16), preferred_element_type=F32)
                for t in cur}
        for t in cur:
            g_last = g_row(*t)[:, c - 1:c]
            k_dec = blk_of(k_ref, *t) * jnp.exp(g_last - g_col(*t))
            s_ref[t[0], t[2]] = jnp.exp(g_last) * states[t] + lax.dot_general(
                k_dec.astype(BF16), v_news[t].astype(BF16), (((0,), (0,)), ((), ())),
                preferred_element_type=F32)
        for t in cur:
            o = outs[t]
            ms = jnp.mean(o * o, axis=-1, keepdims=True)
            y_ref[t[0], rows(j), cols(t[2])] = (o * lax.rsqrt(ms + EPS) * ng_ref[...]
                                                * gz_ref[t[0], rows(j), cols(t[2])]).astype(BF16)


def _gdn_scan(q, k, v, gz, gb, gr, norm_g):
    b, l, hk = gz.shape
    nh = hk // GDN_DK
    t = GDN_STEP
    ns = GDN_SEQS
    tok = lambda width: pl.BlockSpec((ns, t, width), lambda i, c: (i, c, 0))
    ng_arr, ng_spec = norm_g
    return pl.pallas_call(
        _gdn_scan_kernel,
        grid=(b // ns, l // t),
        in_specs=[tok(hk), tok(hk), tok(hk), tok(hk), tok(LANES),
                  pl.BlockSpec((ns, SUBLANES, t), lambda i, c: (i, 0, c)), ng_spec],
        out_specs=[tok(hk), pl.BlockSpec((ns, nh, GDN_DK, GDN_DV), lambda i, c: (i, 0, 0, 0))],
        out_shape=[jax.ShapeDtypeStruct((b, l, hk), BF16),
                   jax.ShapeDtypeStruct((b, nh, GDN_DK, GDN_DV), F32)],
        compiler_params=_params("parallel", "arbitrary"),
        name="gdn_scan",
    )(q, k, v, gz, gb, gr, ng_arr)


def _gdn_dec_in_kernel(z_ref, x_ref, mod_ref, wqkv_ref, wg_ref, wab_ref, cw_ref, alr_ref, dtr_ref, cs_ref,
                       qt_ref, kt_ref, v_ref, gz_ref, eg_ref, be_ref, nb_ref, q_s, k_s, p_s):
    hk = v_ref.shape[-1]
    nh = hk // GDN_DK
    u = _modnorm(x_ref[...], mod_ref[1], mod_ref[0]).astype(BF16)

    def conv_fn(pre, j):
        return _conv_step(pre, cs_ref, nb_ref, cw_ref, slice(j * hk, (j + 1) * hk), None)

    _gdn_qkvg(u, u, wqkv_ref, wg_ref, conv_fn, q_s, k_s, v_ref, gz_ref, p_s, z_ref[0])
    ab = jnp.dot(u, wab_ref[...].astype(BF16), preferred_element_type=F32)
    e_g = jnp.exp(-jnp.exp(alr_ref[...]) * jax.nn.softplus(ab + dtr_ref[...]))
    beta = _sigmoid(ab)
    n = ab.shape[0]
    for h in range(nh):
        hs = slice(h * GDN_DK, (h + 1) * GDN_DK)
        qt_ref[h] = q_s[:, hs].T
        kt_ref[h] = k_s[:, hs].T
        eg_ref[h] = jnp.broadcast_to(e_g[:, h:h + 1], (n, LANES))
        be_ref[h] = jnp.broadcast_to(beta[:, nh + h:nh + h + 1], (n, LANES))


def _state_spec(state, j):
    return pl.BlockSpec((None,) + state.shape[1:], lambda *_: (j,) + (0,) * (state.ndim - 1))


def _gdn_dec_in(x, mod, p, conv_state, j):
    n, d = x.shape
    hk = p["hk"]
    nh = hk // GDN_DK
    w_arrs, w_specs = _unzip([p["wqkv"], p["wg"], p["wab"], p["cw"], p["alr"], p["dtr"]])
    zero, zero_spec = _zero_operand()
    return pl.pallas_call(
        _gdn_dec_in_kernel,
        grid=(1,),
        in_specs=[zero_spec, _full((n, d)), _mod_operand(mod)[1]] + w_specs + [_state_spec(conv_state, j)],
        out_specs=[_full((nh, GDN_DK, n)), _full((nh, GDN_DK, n)), _full((n, hk)), _full((n, hk)),
                   _full((nh, n, LANES)), _full((nh, n, LANES)), _full((n, 3, 3 * hk))],
        out_shape=[jax.ShapeDtypeStruct((nh, GDN_DK, n), F32)] * 2
        + [jax.ShapeDtypeStruct((n, hk), F32)] * 2
        + [jax.ShapeDtypeStruct((nh, n, LANES), F32)] * 2
        + [jax.ShapeDtypeStruct((n, 3, 3 * hk), F32)],
        scratch_shapes=[pltpu.VMEM((n, hk), F32)] * 2 + [pltpu.VMEM((DRAIN_SLOTS, n, hk), F32)],
        compiler_params=_params("arbitrary"),
        name="gdn_decode_in",
    )(zero, x, mod[0], *w_arrs, conv_state)


def _gdn_dec_state_kernel(qt_ref, kt_ref, v_ref, gz_ref, eg_ref, be_ref, ng_ref, s0_ref,
                          y_ref, s_ref, o_s):
    n = s0_ref.shape[0]
    for b in range(n):
        kc = kt_ref[:, b:b + 1]
        qc = qt_ref[:, b:b + 1]
        s0 = s0_ref[b]
        k_s0 = jnp.sum(kc * s0, axis=0, keepdims=True)
        q_s0 = jnp.sum(qc * s0, axis=0, keepdims=True)
        qk = jnp.sum(qc * kc, axis=0, keepdims=True)
        e_g = eg_ref[b:b + 1, :]
        v_new = be_ref[b:b + 1, :] * (v_ref[b:b + 1, :] - e_g * k_s0)
        o_s[b:b + 1, :] = e_g * q_s0 + qk * v_new
        s_ref[b] = e_g * s0 + kc * v_new
    o = o_s[...]
    ms = jnp.mean(o * o, axis=-1, keepdims=True)
    y_ref[...] = (o * lax.rsqrt(ms + EPS) * ng_ref[...] * gz_ref[...]).astype(BF16)


def _gdn_dec_state(qt, kt, v, gz, eg, be, norm_g, s0, j):
    _, n, nh, dk, dv = s0.shape
    head = lambda a, bb: pl.BlockSpec((None, a, bb), lambda h: (h, 0, 0))
    col = pl.BlockSpec((n, dv), lambda h: (0, h))
    ng_arr, ng_spec = norm_g
    return pl.pallas_call(
        _gdn_dec_state_kernel,
        grid=(nh,),
        in_specs=[head(dk, n), head(dk, n), col, col, head(n, LANES), head(n, LANES), ng_spec,
                  pl.BlockSpec((None, n, None, dk, dv), lambda h: (j, 0, h, 0, 0))],
        out_specs=[col, pl.BlockSpec((n, None, dk, dv), lambda h: (0, h, 0, 0))],
        out_shape=[jax.ShapeDtypeStruct((n, nh * dv), BF16), jax.ShapeDtypeStruct((n, nh, dk, dv), F32)],
        scratch_shapes=[pltpu.VMEM((n, dv), F32)],
        compiler_params=_params("parallel"),
        name="gdn_decode_state",
    )(qt, kt, v, gz, eg, be, ng_arr, s0)


def _ssd_proj(u, u_z, wz_ref, wx_refs, conv_fn, zs_ref, xs_ref, bm_ref, cm_ref, buf, zero, after_first=None):
    gn = bm_ref.shape[-1]
    nz = zs_ref.shape[-1] // WBLK
    nx = len(wx_refs)
    order = [blk for pair in zip([("x", j) for j in range(nx)], [("z", j) for j in range(nz)] + [None] * nx)
             for blk in pair if blk is not None]

    def consume(i, pre):
        kind, j = order[i]
        cs = slice(j * WBLK, (j + 1) * WBLK)
        if kind == "z":
            zs_ref[:, cs] = _silu(pre)
            return
        act = _silu(conv_fn(pre, cs))
        if j + 1 < nx:
            xs_ref[:, cs] = act
        else:
            bm_ref[...] = act[:, :gn]
            cm_ref[...] = act[:, gn:]

    def dot_of(kind, j):
        if kind == "z":
            return jnp.dot(u_z, wz_ref[:, j * WBLK:(j + 1) * WBLK], preferred_element_type=F32)
        return jnp.dot(u, wx_refs[j][...], preferred_element_type=F32)

    _pipelined([functools.partial(dot_of, *blk) for blk in order], consume, buf, zero, after_first,
               side_after=2)


def _ssd_in_kernel(z_ref, x_ref, mod_ref, wz_ref, wx0_ref, wx1_ref, wx2_ref, wdt_ref, wdtt_ref, cw_ref,
                   cb_ref, dtr_ref, alr_ref, dtc_ref, alc_ref,
                   zs_ref, xs_ref, bm_ref, cm_ref, da_ref, at_ref, dtt_ref, tail_ref, p_s):
    t = x_ref.shape[0]
    nh = at_ref.shape[0]

    @pl.when(pl.program_id(1) == 0)
    def _():
        tail_ref[...] = jnp.zeros_like(tail_ref)

    seq = pl.program_id(0)
    u = _modnorm(x_ref[...], _mrow(mod_ref, 1, seq), _mrow(mod_ref, 0, seq)).astype(BF16)

    def conv_fn(pre, cs):
        out = _conv_seq(pre, tail_ref[:, cs], cw_ref.at[:, cs], cb_ref[:, cs])
        tail_ref[:, cs] = pre[t - SUBLANES:, :]
        return out

    def decays():
        dt_c = jax.nn.softplus(jnp.dot(u, wdt_ref[...].astype(BF16), preferred_element_type=F32)
                               + dtr_ref[...])
        dt_r = jax.nn.softplus(lax.dot_general(wdtt_ref[...].astype(BF16), u, (((1,), (1,)), ((), ())),
                                               preferred_element_type=F32) + dtc_ref[...])
        acs_c = _sdot_r(_block_tri(t, SSD_CHUNK, True), dt_c * -jnp.exp(alr_ref[...]))
        acs_r = _sdot_l(dt_r * -jnp.exp(alc_ref[...]), _block_tri(t, SSD_CHUNK, False))
        lane = lax.broadcasted_iota(jnp.int32, dt_c.shape, 1)
        da_ref[...] = jnp.where(lane < nh, dt_c, pltpu.roll(acs_c, nh, 1))
        at_ref[...] = acs_r[:nh, :]
        dtt_ref[...] = dt_r[:nh, :]

    _ssd_proj(u, u, wz_ref, (wx0_ref, wx1_ref, wx2_ref), conv_fn, zs_ref, xs_ref, bm_ref, cm_ref, p_s, z_ref[0],
              decays)


def _ssd_in(x, mod, p):
    b, l, d = x.shape
    di, cd, nh = p["di"], p["cd"], p["nh"]
    gn = (cd - di) // 2
    t = ROW_TILE
    tok = lambda width: pl.BlockSpec((None, t, width), lambda i, c: (i, c, 0))
    w_arrs, w_specs = _unzip([p["wz"], *p["wx"], p["wdt"], p["wdtt"], p["cw"], p["cb"], p["dtr"],
                              p["alr"], p["dtc"], p["alc"]])
    rows_spec = pl.BlockSpec((None, nh, t), lambda i, c: (i, 0, c))
    zero, zero_spec = _zero_operand()
    return pl.pallas_call(
        _ssd_in_kernel,
        grid=(b, l // t),
        in_specs=[zero_spec, tok(d), _mod_operand(mod)[1]] + w_specs,
        out_specs=[tok(di), tok(di), tok(gn), tok(gn), tok(LANES), rows_spec, rows_spec,
                   pl.BlockSpec((None, SUBLANES, cd), lambda i, c: (i, 0, 0))],
        out_shape=[jax.ShapeDtypeStruct((b, l, di), F32), jax.ShapeDtypeStruct((b, l, di), F32),
                   jax.ShapeDtypeStruct((b, l, gn), F32), jax.ShapeDtypeStruct((b, l, gn), F32),
                   jax.ShapeDtypeStruct((b, l, LANES), F32),
                   jax.ShapeDtypeStruct((b, nh, l), F32), jax.ShapeDtypeStruct((b, nh, l), F32),
                   jax.ShapeDtypeStruct((b, SUBLANES, cd), F32)],
        scratch_shapes=[pltpu.VMEM((DRAIN_SLOTS, t, WBLK), F32)],
        compiler_params=_params("parallel", "arbitrary"),
        name="ssd_in",
    )(zero, x, mod[0], *w_arrs)


def _ssd_scan_kernel(zs_ref, xs_ref, bm_ref, cm_ref, da_ref, at_ref, dtt_ref, e2_ref, dsk_ref, ng_ref,
                     y_ref, hout_ref, ht_s):
    c = zs_ref.shape[0]
    di = zs_ref.shape[1]
    n = SSM_STATE
    pdim = SSM_HEADDIM
    gw = di // SSM_GROUPS
    hpg = gw // pdim
    nh = di // pdim
    ci = pl.program_id(1)

    @pl.when(ci == 0)
    def _():
        ht_s[...] = jnp.zeros_like(ht_s)

    da = da_ref[...]
    lane_a = lax.broadcasted_iota(jnp.int32, da.shape, 1)
    is_acs = (lane_a >= nh) & (lane_a < 2 * nh)
    acs = jnp.where(is_acs, da, 0.0)
    dt_on_acs = jnp.where(is_acs, pltpu.roll(da, nh, 1), 0.0)
    eacs_e = _sdot_l(jnp.where(is_acs, jnp.exp(acs), 0.0), e2_ref[...])
    dsdt_e = _sdot_l(jnp.exp(acs[c - 1:c, :] - acs) * dt_on_acs, e2_ref[...])
    xs = xs_ref[...]
    bms = [bm_ref[:, g * n:(g + 1) * n] for g in range(SSM_GROUPS)]
    cms = [cm_ref[:, g * n:(g + 1) * n] for g in range(SSM_GROUPS)]
    xd = xs * dsdt_e

    r = lax.broadcasted_iota(jnp.int32, (c, c), 0)
    cc = lax.broadcasted_iota(jnp.int32, (c, c), 1)
    causal = r >= cc
    lane = lax.broadcasted_iota(jnp.int32, (c, 2 * pdim), 1)

    groups = range(SSM_GROUPS)
    gsl = lambda g: slice(g * gw, (g + 1) * gw)
    nsl = lambda g: slice(g * n, (g + 1) * n)
    assert n == LANES
    cbs = [_bdot_nt(cms[g], bms[g]) for g in groups]
    hts = [ht_s[g] for g in groups]
    y_offs = [_bdot(cms[g], hts[g]) * eacs_e[:, gsl(g)] for g in groups]
    for g in groups:
        ht_s[g] = eacs_e[c - 1:c, gsl(g)] * hts[g] + _bdot(bms[g].T, xd[:, gsl(g)])
    for g in groups:
        y_parts = []
        for pr in range(hpg // 2):
            ms = []
            for hh in (2 * pr, 2 * pr + 1):
                h = g * hpg + hh
                seg = da[:, nh + h:nh + h + 1] - at_ref[h:h + 1, :]
                lm = jnp.where(causal, jnp.exp(jnp.where(causal, seg, 0.0)), 0.0)
                ms.append((cbs[g] * lm * dtt_ref[h:h + 1, :]).astype(BF16))
            ps = slice(g * gw + pr * 2 * pdim, g * gw + (pr + 1) * 2 * pdim)
            xp = xs[:, ps].astype(BF16)
            zero = jnp.zeros_like(xp)
            rhs = jnp.concatenate([jnp.where(lane < pdim, xp, zero), jnp.where(lane >= pdim, xp, zero)], axis=0)
            y_d = jnp.dot(jnp.concatenate(ms, axis=1), rhs, preferred_element_type=F32)
            os_ = slice(pr * 2 * pdim, (pr + 1) * 2 * pdim)
            y_parts.append(y_d + y_offs[g][:, os_] + dsk_ref[:, ps] * xs[:, ps])
        yz = jnp.concatenate(y_parts, axis=1) * zs_ref[:, gsl(g)]
        ms_ = jnp.mean(yz * yz, axis=-1, keepdims=True)
        y_ref[:, gsl(g)] = (yz * lax.rsqrt(ms_ + EPS) * ng_ref[:, gsl(g)]).astype(BF16)

    @pl.when(ci == pl.num_programs(1) - 1)
    def _():
        for g in groups:
            hout_ref[gsl(g), :] = ht_s[g].T


def _ssd_scan(zs, xs, bm, cm, da, at, dtt, p):
    b, l, di = zs.shape
    gn = bm.shape[-1]
    nh = p["nh"]
    gw = di // SSM_GROUPS
    t = SSD_CHUNK
    tok = lambda width: pl.BlockSpec((None, t, width), lambda i, c: (i, c, 0))
    rows_spec = pl.BlockSpec((None, nh, t), lambda i, c: (i, 0, c))
    w_arrs, w_specs = _unzip([p["e2"], p["dsk"], p["ng"]])
    return pl.pallas_call(
        _ssd_scan_kernel,
        grid=(b, l // t),
        in_specs=[tok(di), tok(di), tok(gn), tok(gn), tok(LANES), rows_spec, rows_spec] + w_specs,
        out_specs=[tok(di), pl.BlockSpec((None, di, SSM_STATE), lambda i, c: (i, 0, 0))],
        out_shape=[jax.ShapeDtypeStruct((b, l, di), BF16),
                   jax.ShapeDtypeStruct((b, di, SSM_STATE), F32)],
        scratch_shapes=[pltpu.VMEM((SSM_GROUPS, SSM_STATE, gw), F32)],
        compiler_params=_params("parallel", "arbitrary"),
        name="ssd_scan",
    )(zs, xs, bm, cm, da, at, dtt, *w_arrs)


def _ssd_dec_in_kernel(z_ref, x_ref, mod_ref, wz_ref, wx0_ref, wx1_ref, wx2_ref, wdt_ref, cw_ref, cb_ref,
                       dtr_ref, alr_ref, e1_ref, cs_ref,
                       zs_ref, xs_ref, bm_ref, cm_ref, xt_ref, ea_ref, xdt_ref, eae_ref, nb_ref, p_s):
    n = x_ref.shape[0]
    di = zs_ref.shape[-1]
    nh = di // SSM_HEADDIM
    u = _modnorm(x_ref[...], mod_ref[1], mod_ref[0]).astype(BF16)

    def conv_fn(pre, cs):
        return _conv_step(pre, cs_ref, nb_ref, cw_ref, cs, cb_ref[:, cs])

    _ssd_proj(u, u, wz_ref, (wx0_ref, wx1_ref, wx2_ref), conv_fn, zs_ref, xs_ref, bm_ref, cm_ref, p_s, z_ref[0])
    dt = jax.nn.softplus(jnp.dot(u, wdt_ref[...].astype(BF16), preferred_element_type=F32) + dtr_ref[...])
    e_a = jnp.exp(dt * -jnp.exp(alr_ref[...]))
    xdt_ref[...] = xs_ref[...] * _sdot_l(dt, e1_ref[...])
    eae_ref[...] = _sdot_l(e_a, e1_ref[...])
    for j in range(di // LANES):
        xt_ref[j * LANES:(j + 1) * LANES, :] = xdt_ref[:, j * LANES:(j + 1) * LANES].T
    for h in range(nh):
        ea_ref[h] = jnp.broadcast_to(e_a[:, h:h + 1], (n, LANES))


def _ssd_dec_in(x, mod, p, conv_state, j):
    n, d = x.shape
    di, cd, nh = p["di"], p["cd"], p["nh"]
    gn = (cd - di) // 2
    w_arrs, w_specs = _unzip([p["wz"], *p["wx"], p["wdt"], p["cw"], p["cb"], p["dtr"], p["alr"], p["e1"]])
    zero, zero_spec = _zero_operand()
    return pl.pallas_call(
        _ssd_dec_in_kernel,
        grid=(1,),
        in_specs=[zero_spec, _full((n, d)), _mod_operand(mod)[1]] + w_specs + [_state_spec(conv_state, j)],
        out_specs=[_full((n, di)), _full((n, di)), _full((n, gn)), _full((n, gn)), _full((di, n)),
                   _full((nh, n, LANES)), _full((n, di)), _full((n, di)), _full((n, 3, cd))],
        out_shape=[jax.ShapeDtypeStruct((n, di), F32), jax.ShapeDtypeStruct((n, di), F32),
                   jax.ShapeDtypeStruct((n, gn), F32), jax.ShapeDtypeStruct((n, gn), F32),
                   jax.ShapeDtypeStruct((di, n), F32),
                   jax.ShapeDtypeStruct((nh, n, LANES), F32),
                   jax.ShapeDtypeStruct((n, di), F32), jax.ShapeDtypeStruct((n, di), F32),
                   jax.ShapeDtypeStruct((n, 3, cd), F32)],
        scratch_shapes=[pltpu.VMEM((DRAIN_SLOTS, n, WBLK), F32)],
        compiler_params=_params("arbitrary"),
        name="ssd_decode_in",
    )(zero, x, mod[0], *w_arrs, conv_state)


def _ssd_dec_state_kernel(xt_ref, bm_ref, cm_ref, ea_ref, h0_ref, yo_ref, h_ref):
    n = h0_ref.shape[0]
    pdim = h0_ref.shape[2]
    for b in range(n):
        bb = bm_ref[b:b + 1, :]
        c8 = jnp.broadcast_to(cm_ref[b:b + 1, :], (SUBLANES, bb.shape[1]))
        outs = []
        for hh in range(2):
            h0 = h0_ref[b, hh]
            outs.append(_bdot_nt(c8, h0)[0:1, :])
            xc = xt_ref[hh * pdim:(hh + 1) * pdim, b:b + 1]
            h_ref[b, hh] = ea_ref[hh, b:b + 1, :] * h0 + xc * bb
        yo_ref[b:b + 1, :] = jnp.concatenate(outs, axis=1)


def _ssd_dec_state(xt, bm, cm, ea, h0, j):
    _, n, nh, pdim, ns = h0.shape
    hpg = nh // SSM_GROUPS
    return pl.pallas_call(
        _ssd_dec_state_kernel,
        grid=(nh // 2,),
        in_specs=[pl.BlockSpec((2 * pdim, n), lambda i: (i, 0)),
                  pl.BlockSpec((n, ns), lambda i: (0, (2 * i) // hpg)),
                  pl.BlockSpec((n, ns), lambda i: (0, (2 * i) // hpg)),
                  pl.BlockSpec((2, n, LANES), lambda i: (i, 0, 0)),
                  pl.BlockSpec((None, n, 2, pdim, ns), lambda i: (j, 0, i, 0, 0))],
        out_specs=[pl.BlockSpec((n, 2 * pdim), lambda i: (0, i)),
                   pl.BlockSpec((n, 2, pdim, ns), lambda i: (0, i, 0, 0))],
        out_shape=[jax.ShapeDtypeStruct((n, nh * pdim), F32), jax.ShapeDtypeStruct((n, nh, pdim, ns), F32)],
        compiler_params=_params("parallel"),
        name="ssd_decode_state",
    )(xt, bm, cm, ea, h0)


def _ssd_dec_out_kernel(yo_ref, xs_ref, bm_ref, cm_ref, zs_ref, eae_ref, xdt_ref, dsk_ref, ng_ref, y_ref):
    di = xs_ref.shape[1]
    gw = di // SSM_GROUPS
    n = SSM_STATE
    for g in range(SSM_GROUPS):
        gs = slice(g * gw, (g + 1) * gw)
        cb = jnp.sum(cm_ref[:, g * n:(g + 1) * n] * bm_ref[:, g * n:(g + 1) * n], axis=-1, keepdims=True)
        y = cb * xdt_ref[:, gs] + eae_ref[:, gs] * yo_ref[:, gs] + dsk_ref[:, gs] * xs_ref[:, gs]
        yz = y * zs_ref[:, gs]
        ms = jnp.mean(yz * yz, axis=-1, keepdims=True)
        y_ref[:, gs] = (yz * lax.rsqrt(ms + EPS) * ng_ref[:, gs]).astype(BF16)


def _ssd_dec_out(yo, xs, bm, cm, zs, eae, xdt, p):
    n, di = xs.shape
    gn = bm.shape[1]
    w_arrs, w_specs = _unzip([p["dsk"], p["ng"]])
    return pl.pallas_call(
        _ssd_dec_out_kernel,
        grid=(1,),
        in_specs=[_full((n, di)), _full((n, di)), _full((n, gn)), _full((n, gn)), _full((n, di)),
                  _full((n, di)), _full((n, di))] + w_specs,
        out_specs=_full((n, di)),
        out_shape=jax.ShapeDtypeStruct((n, di), BF16),
        compiler_params=_params("arbitrary"),
        name="ssd_decode_out",
    )(yo, xs, bm, cm, zs, eae, xdt, *w_arrs)


def _pad_lanes(v):
    return jnp.pad(v.astype(F32), (0, LANES - v.shape[0])).reshape(1, LANES)


def _pad_cols(w):
    return jnp.pad(w, ((0, 0), (0, LANES - w.shape[1])))


def _small(a):
    return _w(a, a.shape, (0,) * a.ndim)


def _rg_params(j, w_in, conv_w, conv_b, gate_w, gate_b, lam, w_out):
    nl, d, w2 = w_in.shape
    w = w2 // 2
    nblk = w // RG_BLOCK
    gate = (None, None, nblk, RG_BLOCK, RG_BLOCK)
    return dict(width=w,
                wy=_w(w_in, (None, d, w), (j, 0, 0)), wx=_w(w_in, (None, d, w), (j, 0, 1)),
                cw=_w(conv_w, (None, 4, w), (j, 0, 0)), cb=_w(conv_b.reshape(nl, 1, w), (None, 1, w), (j, 0, 0)),
                gwr=_w(gate_w, gate, (j, 0, 0, 0, 0)), gwi=_w(gate_w, gate, (j, 1, 0, 0, 0)),
                gb=_w(gate_b, (None, 2, w), (j, 0, 0)), lam=_w(lam.reshape(nl, 1, w), (None, 1, w), (j, 0, 0)),
                wo=_w(w_out, (None, w, d), (j, 0, 0)))


def _gdn_params(j, w_in, w_in_f32, conv_w, a_log, dt_bias, norm_g, w_out):
    nl, d, _ = w_in.shape
    qkv = conv_w.shape[-1]
    hk = qkv // 3
    wab = _pad_cols(w_in_f32[j, :, qkv + hk:])
    alr, dtr = _pad_lanes(a_log[j]), _pad_lanes(dt_bias[j])
    return dict(hk=hk,
                wqkv=_w(w_in, (None, d, qkv), (j, 0, 0)), wg=_w(w_in, (None, d, hk), (j, 0, qkv // hk)),
                wab=_small(wab), wabt=_small(wab.T), cw=_w(conv_w, (None, 4, qkv), (j, 0, 0)),
                alr=_small(alr), dtr=_small(dtr), alc=_small(alr.T), dtc=_small(dtr.T),
                ng=_w(norm_g.reshape(nl, 1, GDN_DV), (None, 1, GDN_DV), (j, 0, 0)),
                wo=_w(w_out, (None, hk, d), (j, 0, 0)))


def _ssd_params(j, w_in, w_in_f32, conv_w, conv_b, a_log, dt_bias, d_skip, norm_g, w_out):
    nl, d, _ = w_in.shape
    cd = conv_w.shape[-1]
    nh = a_log.shape[-1]
    di = nh * SSM_HEADDIM
    assert di % WBLK == 0 and (cd - di) == WBLK
    wdt = _pad_cols(w_in_f32[j, :, di + cd:])
    alr, dtr = _pad_lanes(a_log[j]), _pad_lanes(dt_bias[j])
    head_of = jnp.arange(di, dtype=jnp.int32) // SSM_HEADDIM
    rows = jnp.arange(LANES, dtype=jnp.int32)[:, None]
    e1 = (rows == head_of[None, :]).astype(BF16)
    e2 = (rows == head_of[None, :] + nh).astype(BF16)
    return dict(di=di, cd=cd, nh=nh,
                wz=_w(w_in, (None, d, di), (j, 0, 0)),
                wx=[_w(w_in, (None, d, WBLK), (j, 0, di // WBLK + i)) for i in range(cd // WBLK)],
                wdt=_small(wdt), wdtt=_small(wdt.T), cw=_w(conv_w, (None, 4, cd), (j, 0, 0)),
                cb=_w(conv_b.reshape(nl, 1, cd), (None, 1, cd), (j, 0, 0)),
                alr=_small(alr), dtr=_small(dtr), alc=_small(alr.T), dtc=_small(dtr.T),
                e1=_small(e1), e2=_small(e2),
                dsk=_small(jnp.repeat(d_skip[j], SSM_HEADDIM).reshape(1, di)),
                ng=_w(norm_g.reshape(nl, 1, di), (None, 1, di), (j, 0, 0)),
                wo=_w(w_out, (None, di, d), (j, 0, 0)))


def kernel(x_prompt, x_sample, state_rglru_conv, state_rglru_h, state_gdn_conv, state_gdn_S, state_ssd_conv, state_ssd_h, c_prompt, c_sample, w_mod, b_mod, w_mlp_up, w_mlp_down, final_norm_g, rg_w_in, rg_conv_w, rg_conv_b, rg_gate_w, rg_gate_b, rg_lambda, rg_w_out, gdn_w_in, gdn_conv_w, gdn_A_log, gdn_dt_bias, gdn_norm_g, gdn_w_out, ssd_w_in, ssd_conv_w, ssd_conv_b, ssd_A_log, ssd_dt_bias, ssd_D, ssd_norm_g, ssd_w_out):
    bp, l, d = x_prompt.shape
    ns = x_sample.shape[0]
    hid = w_mlp_up.shape[-1]
    assert x_sample.shape[1] == 1 and l % ROW_TILE == 0 and ns % SUBLANES == 0 and bp % GDN_SEQS == 0

    assert ns % bp == 0 and bp % SUBLANES == 0
    mod = _modulation(jnp.concatenate([c_sample, c_prompt], axis=0), w_mod, b_mod)
    mod_p = [(mod, i, bp, ns) for i in range(DEPTH)]
    mod_s = [(mod, i, ns, 0) for i in range(DEPTH)]
    fg = _small(final_norm_g.reshape(1, d))

    w_up, w_down = w_mlp_up, w_mlp_down
    rg_in, rg_gate, rg_out = rg_w_in.astype(BF16), rg_gate_w.astype(BF16), rg_w_out.astype(BF16)
    gdn_in, gdn_out = gdn_w_in.astype(BF16), gdn_w_out.astype(BF16)
    ssd_in, ssd_out = ssd_w_in.astype(BF16), ssd_w_out.astype(BF16)

    xp = x_prompt
    xs = x_sample.reshape(ns, d)
    tails = lambda t: t[:, SUBLANES - 3:, :]
    tails_interleaved = lambda t: t[:, SUBLANES - 1::SUBLANES, :]
    out = {k: [] for k in ("p_rg_conv", "p_rg_h", "p_gdn_conv", "p_gdn_S", "p_ssd_conv", "p_ssd_h",
                           "s_rg_conv", "s_rg_h", "s_gdn_conv", "s_gdn_S", "s_ssd_conv", "s_ssd_h")}
    for i in range(DEPTH):
        j = i // N_MIXERS
        kind = i % N_MIXERS
        if kind == 0:
            p = _rg_params(j, rg_in, rg_conv_w, rg_conv_b, rg_gate, rg_gate_b, rg_lambda, rg_out)
            yp, tail, h_last = _rg_prompt(xp, mod_p[i], p)
            out["p_rg_conv"].append(tails_interleaved(tail))
            out["p_rg_h"].append(h_last[:, 0, :])
            ysm, nb, h_new = _rg_decode(xs, mod_s[i], p, state_rglru_conv, state_rglru_h, j)
            out["s_rg_conv"].append(nb)
            out["s_rg_h"].append(h_new)
        elif kind == 1:
            p = _gdn_params(j, gdn_in, gdn_w_in, gdn_conv_w, gdn_A_log, gdn_dt_bias, gdn_norm_g, gdn_out)
            q, k, v, gz, gb, gr, tail = _gdn_in(xp, mod_p[i], p)
            yp, s_fin = _gdn_scan(q, k, v, gz, gb, gr, p["ng"])
            out["p_gdn_conv"].append(tails(tail))
            out["p_gdn_S"].append(s_fin)
            qt, kt, v1, gz1, eg, be, nb = _gdn_dec_in(xs, mod_s[i], p, state_gdn_conv, j)
            ysm, s_new = _gdn_dec_state(qt, kt, v1, gz1, eg, be, p["ng"], state_gdn_S, j)
            out["s_gdn_conv"].append(nb)
            out["s_gdn_S"].append(s_new)
        else:
            p = _ssd_params(j, ssd_in, ssd_w_in, ssd_conv_w, ssd_conv_b, ssd_A_log, ssd_dt_bias, ssd_D, ssd_norm_g,
                            ssd_out)
            zs, xc, bm, cm, da, at, dtt, tail = _ssd_in(xp, mod_p[i], p)
            yp, h_fin = _ssd_scan(zs, xc, bm, cm, da, at, dtt, p)
            out["p_ssd_conv"].append(tails(tail))
            out["p_ssd_h"].append(h_fin.reshape(bp, -1, SSM_HEADDIM, SSM_STATE))
            zs1, xc1, bm1, cm1, xt, ea, xdt, eae, nb = _ssd_dec_in(xs, mod_s[i], p, state_ssd_conv, j)
            yo, h_new = _ssd_dec_state(xt, bm1, cm1, ea, state_ssd_h, j)
            ysm = _ssd_dec_out(yo, xc1, bm1, cm1, zs1, eae, xdt, p)
            out["s_ssd_conv"].append(nb)
            out["s_ssd_h"].append(h_new)
        final = i == DEPTH - 1
        weights = [p["wo"], _w(w_up, (None, d, hid), (i, 0, 0)), _w(w_down, (None, hid, d), (i, 0, 0)), fg]
        y_rows = yp if yp.ndim == 4 else yp.reshape(bp * l, -1)
        xp = _outproj_mlp(xp.reshape(bp * l, d), y_rows, mod_p[i], weights,
                          tm=ROW_TILE, rows_per_mod=l, final=final).reshape(bp, l, d)
        xs = _outproj_mlp(xs, ysm, mod_s[i], weights, tm=ns, rows_per_mod=1, final=final)

    st = {k: jnp.stack(v) for k, v in out.items()}
    return (xp, xs.reshape(ns, 1, d),
            st["p_rg_conv"], st["p_rg_h"], st["p_gdn_conv"], st["p_gdn_S"], st["p_ssd_conv"], st["p_ssd_h"],
            st["s_rg_conv"], st["s_rg_h"], st["s_gdn_conv"], st["s_gdn_S"], st["s_ssd_conv"], st["s_ssd_h"])
```

```python
import functools

import jax
import jax.numpy as jnp
from jax import lax
from jax.experimental import pallas as pl
from jax.experimental.pallas import tpu as pltpu

F32 = jnp.float32
BF16 = jnp.bfloat16

DEPTH = 4
N_MIXERS = 3
EPS = 1e-6
RG_C = 8.0
RG_BLOCK = 256
GDN_DK = 128
GDN_DV = 128
SSM_HEADDIM = 64
SSM_STATE = 128
SSM_GROUPS = 4

SUBLANES = 8
LANES = 128

VMEM_LIMIT = 56 * 1024 * 1024
ROW_TILE = 512
GDN_CHUNK = 64
GDN_STEP = 256
GDN_SEQS = 1
SSD_CHUNK = 256
MLP_HCHUNK = 1024
WBLK = 1024
TAIL_ROWS = 3 * SUBLANES


def _params(*sem):
    return pltpu.CompilerParams(dimension_semantics=sem, vmem_limit_bytes=VMEM_LIMIT)


def _w(arr, blk, idx):
    return arr, pl.BlockSpec(blk, lambda *_: idx, pipeline_mode=pl.Buffered(1))


def _unzip(pairs):
    return [a for a, _ in pairs], [s for _, s in pairs]


def _bdot(a, b):
    return jnp.dot(a.astype(BF16), b.astype(BF16), preferred_element_type=F32)


def _bdot_nt(a, b):
    return lax.dot_general(a.astype(BF16), b.astype(BF16), (((1,), (1,)), ((), ())),
                           preferred_element_type=F32)


def _split(a):
    hi = a.astype(BF16)
    lo = (a - hi.astype(F32)).astype(BF16)
    return hi, lo


def _sdot_l(a, b_exact):
    hi, lo = _split(a)
    return (jnp.dot(hi, b_exact, preferred_element_type=F32)
            + jnp.dot(lo, b_exact, preferred_element_type=F32))


def _sdot_r(a_exact, b):
    hi, lo = _split(b)
    return (jnp.dot(a_exact, hi, preferred_element_type=F32)
            + jnp.dot(a_exact, lo, preferred_element_type=F32))


def _modnorm(x, sc, sh):
    ms = jnp.mean(x * x, axis=-1, keepdims=True)
    return x * lax.rsqrt(ms + EPS) * (1.0 + sc) + sh


def _sigmoid(x):
    return 0.5 * jnp.tanh(0.5 * x) + 0.5


def _silu(x):
    hx = 0.5 * x
    return hx * jnp.tanh(hx) + hx


def _conv_seq(xpre, tail, w_ref, bias):
    xcat = jnp.concatenate([tail, xpre], axis=0)
    acc = xpre * w_ref[3:4, :]
    for k in (1, 2, 3):
        shifted = pltpu.roll(xcat, k, 0)[SUBLANES:]
        acc = acc + shifted * w_ref[3 - k:4 - k, :]
    if bias is not None:
        acc = acc + bias
    return acc


def _interleave_rows(u_t):
    t = u_t.shape[0]
    r = lax.broadcasted_iota(jnp.int32, (t, t), 0)
    c = lax.broadcasted_iota(jnp.int32, (t, t), 1)
    perm = jnp.where(c == (r % SUBLANES) * (t // SUBLANES) + r // SUBLANES, 1.0, 0.0).astype(BF16)
    return jnp.dot(perm, u_t, preferred_element_type=F32).astype(BF16)


def _conv_interleaved(pre, tail_ref, cs, w_ref, bias):
    ng = pre.shape[0] // SUBLANES
    ntail = tail_ref.shape[0] // SUBLANES
    sub = lax.broadcasted_iota(jnp.int32, (SUBLANES, pre.shape[1]), 0)
    grp = lambda a, g: a[g * SUBLANES:(g + 1) * SUBLANES]
    acc = pre * w_ref[3:4, cs]
    if bias is not None:
        acc = acc + bias
    for k in (1, 2, 3):
        top = [jnp.where(sub == 0,
                         pltpu.roll(tail_ref[(ntail - k + i) * SUBLANES:(ntail - k + i + 1) * SUBLANES, cs], 1, 0),
                         pltpu.roll(grp(pre, ng - k + i), 1, 0)) for i in range(k)]
        shifted = jnp.concatenate(top + [pre[:(ng - k) * SUBLANES]], axis=0)
        acc = acc + shifted * w_ref[3 - k:4 - k, cs]
    tail_ref[:, cs] = pre[(ng - ntail) * SUBLANES:, :]
    return acc


def _conv_step(pre, cs_ref, nb_ref, w_ref, cs, bias):
    nb_ref[:, 0, cs] = cs_ref[:, 1, cs]
    nb_ref[:, 1, cs] = cs_ref[:, 2, cs]
    nb_ref[:, 2, cs] = pre
    out = (cs_ref[:, 0, cs] * w_ref[0:1, cs] + cs_ref[:, 1, cs] * w_ref[1:2, cs]
           + cs_ref[:, 2, cs] * w_ref[2:3, cs] + pre * w_ref[3:4, cs])
    return out if bias is None else out + bias


DRAIN_SLOTS = 2
RG_DRAIN_SLOTS = 3


def _zero_operand():
    return jnp.zeros((1,), jnp.int32), pl.BlockSpec(memory_space=pltpu.SMEM)


def _pipelined(dots, consume, buf, zero, after_first=None, side_after=0):
    n = len(dots)
    slots = buf.shape[0]
    ahead = slots - 1
    for j in range(min(ahead, n)):
        buf[zero + j % slots] = dots[j]()
    for j in range(n):
        if j + ahead < n:
            buf[zero + (j + ahead) % slots] = dots[j + ahead]()
        consume(j, buf[zero + j % slots])
        if j == side_after and after_first is not None:
            after_first()


def _block_tri(n, chunk, lower):
    r = lax.broadcasted_iota(jnp.int32, (n, n), 0)
    c = lax.broadcasted_iota(jnp.int32, (n, n), 1)
    same = (r // chunk) == (c // chunk)
    tri = (c <= r) if lower else (r <= c)
    return jnp.where(same & tri, 1.0, 0.0).astype(BF16)


def _mod_kernel(c_ref, w_ref, b_ref, o_ref):
    o_ref[...] = _bdot(_silu(c_ref[...]), w_ref[...]) + b_ref[...]


def _modulation(c_all, w_mod, b_mod):
    nb, d = c_all.shape
    n6 = w_mod.shape[-1]
    return pl.pallas_call(
        _mod_kernel,
        grid=(DEPTH, n6 // d),
        in_specs=[pl.BlockSpec((nb, d), lambda l, n: (0, 0)),
                  pl.BlockSpec((None, d, d), lambda l, n: (l, 0, n)),
                  pl.BlockSpec((None, 1, d), lambda l, n: (l, 0, n))],
        out_specs=pl.BlockSpec((None, None, nb, d), lambda l, n: (l, n, 0, 0)),
        out_shape=jax.ShapeDtypeStruct((DEPTH, n6 // d, nb, d), F32),
        compiler_params=_params("parallel", "parallel"),
        name="modulation",
    )(c_all, w_mod, b_mod.reshape(DEPTH, 1, n6))


def _mod_operand(mod):
    arr, layer, n_rows, first_row = mod
    return arr, pl.BlockSpec((None, arr.shape[1], n_rows, arr.shape[3]),
                             lambda *_: (layer, 0, first_row // n_rows, 0))


def _mrow(mod_ref, k, row):
    return mod_ref[k] if row is None else mod_ref[k, pl.ds(row, 1), :]


def _time_rows(blk_ref):
    ncb, t, _ = blk_ref.shape
    ng = t // SUBLANES
    return jnp.concatenate(
        [jnp.concatenate([blk_ref[cb, pl.ds(s, ng, stride=SUBLANES), :] for cb in range(ncb)], axis=1)
         for s in range(SUBLANES)], axis=0)


def _outproj_mlp_kernel(x_ref, y_ref, mod_ref, wo_ref, wu_ref, wd_ref, fg_ref, o_ref, *, final, tiles_per_seq):
    row = None if tiles_per_seq is None else pl.program_id(0) // tiles_per_seq
    y = y_ref[...] if len(y_ref.shape) == 2 else _time_rows(y_ref).astype(BF16)
    x1 = x_ref[...] + _mrow(mod_ref, 2, row) * jnp.dot(y, wo_ref[...], preferred_element_type=F32)
    u = _modnorm(x1, _mrow(mod_ref, 4, row), _mrow(mod_ref, 3, row)).astype(BF16)
    hidden = wu_ref.shape[1]
    acc = None
    for c in range(hidden // MLP_HCHUNK):
        cs = slice(c * MLP_HCHUNK, (c + 1) * MLP_HCHUNK)
        h = jnp.maximum(jnp.dot(u, wu_ref[:, cs].astype(BF16), preferred_element_type=F32), 0.0)
        p = jnp.dot((h * h).astype(BF16), wd_ref[cs, :].astype(BF16), preferred_element_type=F32)
        acc = p if acc is None else acc + p
    x2 = x1 + _mrow(mod_ref, 5, row) * acc
    if final:
        ms = jnp.mean(x2 * x2, axis=-1, keepdims=True)
        x2 = x2 * lax.rsqrt(ms + EPS) * fg_ref[...]
    o_ref[...] = x2


def _outproj_mlp(x, y, mod, weights, *, tm, rows_per_mod, final):
    n, d = x.shape
    per = None if rows_per_mod == 1 else rows_per_mod // tm
    assert per is not None or n == tm
    mod_arr, mod_spec = _mod_operand(mod)
    if y.ndim == 2:
        y_spec = pl.BlockSpec((tm, y.shape[1]), lambda i: (i, 0))
    else:
        assert tm == ROW_TILE and rows_per_mod == y.shape[2]
        y_spec = pl.BlockSpec((None, y.shape[1], tm, LANES), lambda i: (i // per, 0, i % per, 0))
    w_arrs, w_specs = _unzip(weights)
    return pl.pallas_call(
        functools.partial(_outproj_mlp_kernel, final=final, tiles_per_seq=per),
        grid=(n // tm,),
        in_specs=[pl.BlockSpec((tm, d), lambda i: (i, 0)), y_spec, mod_spec] + w_specs,
        out_specs=pl.BlockSpec((tm, d), lambda i: (i, 0)),
        out_shape=jax.ShapeDtypeStruct((n, d), F32),
        compiler_params=_params("parallel"),
        name="outproj_mlp",
    )(x, y, mod_arr, *w_arrs)


def _rg_gate_block(n, xbr, gwr_ref, gwi_ref, gb_ref, logsig, a_s, b_s):
    sl = slice(n * RG_BLOCK, (n + 1) * RG_BLOCK)
    xb = xbr.astype(BF16)
    gr = jnp.dot(xb, gwr_ref[n], preferred_element_type=F32)
    gi = jnp.dot(xb, gwi_ref[n], preferred_element_type=F32)
    log_a = (RG_C * logsig[:, sl]) * _sigmoid(gr + gb_ref[0:1, sl])
    a_s[:, sl] = jnp.exp(log_a)
    th = jnp.tanh(log_a)
    z = -2.0 * th / (1.0 - th)
    root = jnp.where(z > 0.0, z * lax.rsqrt(z), 0.0)
    b_s[:, sl] = root * _sigmoid(gi + gb_ref[1:2, sl]) * xbr


def _rg_gates(xbr, gwr_ref, gwi_ref, gb_ref, logsig, a_s, b_s):
    for n in range(xbr.shape[1] // RG_BLOCK):
        _rg_gate_block(n, xbr[:, n * RG_BLOCK:(n + 1) * RG_BLOCK], gwr_ref, gwi_ref, gb_ref, logsig, a_s, b_s)


def _rg_prompt_kernel(z_ref, x_ref, mod_ref, wy_ref, wx_ref, cw_ref, cb_ref, gwr_ref, gwi_ref, gb_ref,
                      lam_ref, y_ref, tail_ref, h_ref, a_s, b_s, y_s, p_s):
    t, w = a_s.shape
    ng = t // SUBLANES
    ncb = w // LANES

    @pl.when(pl.program_id(1) == 0)
    def _():
        tail_ref[...] = jnp.zeros_like(tail_ref)
        h_ref[...] = jnp.zeros_like(h_ref)

    seq = pl.program_id(0)
    u_t = _modnorm(x_ref[...], _mrow(mod_ref, 1, seq), _mrow(mod_ref, 0, seq)).astype(BF16)
    u = _interleave_rows(u_t)
    logsig = jax.nn.log_sigmoid(lam_ref[...])
    nblk = w // RG_BLOCK
    order = [(kind, n) for n in range(nblk) for kind in ("x", "y")]

    def consume(i, pre):
        kind, n = order[i]
        cs = slice(n * RG_BLOCK, (n + 1) * RG_BLOCK)
        if kind == "y":
            y_s[:, cs] = jax.nn.gelu(pre)
            return
        acc = _conv_interleaved(pre, tail_ref, cs, cw_ref, cb_ref[:, cs])
        _rg_gate_block(n, acc, gwr_ref, gwi_ref, gb_ref, logsig, a_s, b_s)

    def dot_of(kind, n):
        w_ref = wx_ref if kind == "x" else wy_ref
        return jnp.dot(u, w_ref[:, n * RG_BLOCK:(n + 1) * RG_BLOCK], preferred_element_type=F32)

    _pipelined([functools.partial(dot_of, *blk) for blk in order], consume, p_s, z_ref[0])

    def compose(gi, carry):
        ca, cbb = carry
        r0 = pl.multiple_of(gi * SUBLANES, SUBLANES)
        a = a_s[pl.ds(r0, SUBLANES), :]
        ca = a * ca
        cbb = a * cbb + b_s[pl.ds(r0, SUBLANES), :]
        a_s[pl.ds(r0, SUBLANES), :] = ca
        b_s[pl.ds(r0, SUBLANES), :] = cbb
        return ca, cbb

    a_end, b_end = lax.fori_loop(0, ng, compose, (jnp.ones((SUBLANES, w), F32), jnp.zeros((SUBLANES, w), F32)),
                                 unroll=4)
    h_in = jnp.zeros((SUBLANES, w), F32)
    for _ in range(SUBLANES):
        h_in = jnp.where(lax.broadcasted_iota(jnp.int32, (SUBLANES, w), 0) == 0, h_ref[...],
                         pltpu.roll(a_end * h_in + b_end, 1, 0))
    h_ref[...] = (a_end * h_in + b_end)[SUBLANES - 1:, :]
    hy = (a_s[...] * jnp.concatenate([h_in] * ng, axis=0) + b_s[...]) * y_s[...]
    for cb in range(ncb):
        y_ref[cb] = hy[:, cb * LANES:(cb + 1) * LANES]


def _rg_weights(p):
    return [p["wy"], p["wx"], p["cw"], p["cb"], p["gwr"], p["gwi"], p["gb"], p["lam"]]


def _rg_prompt(x, mod, p):
    b, l, d = x.shape
    w = p["width"]
    t = ROW_TILE
    w_arrs, w_specs = _unzip(_rg_weights(p))
    ncb = w // LANES
    ntail = (4 - 1) * SUBLANES
    zero, zero_spec = _zero_operand()
    return pl.pallas_call(
        _rg_prompt_kernel,
        grid=(b, l // t),
        in_specs=[zero_spec, pl.BlockSpec((None, t, d), lambda i, c: (i, c, 0)),
                  _mod_operand(mod)[1]] + w_specs,
        out_specs=[pl.BlockSpec((None, ncb, t, LANES), lambda i, c: (i, 0, c, 0)),
                   pl.BlockSpec((None, ntail, w), lambda i, c: (i, 0, 0)),
                   pl.BlockSpec((None, 1, w), lambda i, c: (i, 0, 0))],
        out_shape=[jax.ShapeDtypeStruct((b, ncb, l, LANES), F32),
                   jax.ShapeDtypeStruct((b, ntail, w), F32),
                   jax.ShapeDtypeStruct((b, 1, w), F32)],
        scratch_shapes=[pltpu.VMEM((t, w), F32)] * 3 + [pltpu.VMEM((RG_DRAIN_SLOTS, t, RG_BLOCK), F32)],
        compiler_params=_params("parallel", "arbitrary"),
        name="rglru_prompt",
    )(zero, x, mod[0], *w_arrs)


def _rg_decode_kernel(x_ref, mod_ref, wy_ref, wx_ref, cw_ref, cb_ref, gwr_ref, gwi_ref, gb_ref,
                      lam_ref, cs_ref, h0_ref, y_ref, nb_ref, h_ref, a_s, b_s):
    u = _modnorm(x_ref[...], mod_ref[1], mod_ref[0]).astype(BF16)
    xpre = jnp.dot(u, wx_ref[...], preferred_element_type=F32)
    xbr = _conv_step(xpre, cs_ref, nb_ref, cw_ref, slice(None), cb_ref[...])
    _rg_gates(xbr, gwr_ref, gwi_ref, gb_ref, jax.nn.log_sigmoid(lam_ref[...]), a_s, b_s)
    h = a_s[...] * h0_ref[...] + b_s[...]
    h_ref[...] = h
    y_br = jax.nn.gelu(jnp.dot(u, wy_ref[...], preferred_element_type=F32))
    y_ref[...] = (h * y_br).astype(BF16)


def _full(shape):
    return pl.BlockSpec(shape, lambda *_: (0,) * len(shape))


def _rg_decode(x, mod, p, conv_state, h0, j):
    n, d = x.shape
    w = p["width"]
    w_arrs, w_specs = _unzip(_rg_weights(p))
    return pl.pallas_call(
        _rg_decode_kernel,
        grid=(1,),
        in_specs=[_full((n, d)), _mod_operand(mod)[1]] + w_specs
        + [_state_spec(conv_state, j), _state_spec(h0, j)],
        out_specs=[_full((n, w)), _full((n, 3, w)), _full((n, w))],
        out_shape=[jax.ShapeDtypeStruct((n, w), BF16),
                   jax.ShapeDtypeStruct((n, 3, w), F32),
                   jax.ShapeDtypeStruct((n, w), F32)],
        scratch_shapes=[pltpu.VMEM((n, w), F32), pltpu.VMEM((n, w), F32)],
        compiler_params=_params("arbitrary"),
        name="rglru_decode",
    )(x, mod[0], *w_arrs, conv_state, h0)


def _gdn_qkvg(u, u_gate, wqkv_ref, wg_ref, conv_fn, q_ref, k_ref, v_ref, gz_ref, buf, zero, after_first=None):
    hk = gz_ref.shape[-1]
    nh = hk // GDN_DK
    dsts = ((q_ref, GDN_DK ** -0.5), (k_ref, 1.0), (v_ref, None))
    order = (0, None, 1, 2)

    def consume(i, pre):
        j = order[i]
        if j is None:
            gz_ref[...] = _silu(pre)
            return
        dst, scale = dsts[j]
        act = _silu(conv_fn(pre, j))
        for h in range(nh):
            xh = act[:, h * GDN_DK:(h + 1) * GDN_DK]
            if scale is not None:
                ss = jnp.sum(xh * xh, axis=-1, keepdims=True)
                xh = xh * (lax.rsqrt(ss + EPS) * scale)
            dst[:, h * GDN_DK:(h + 1) * GDN_DK] = xh

    def dot_of(j):
        if j is None:
            return jnp.dot(u_gate, wg_ref[...], preferred_element_type=F32)
        return jnp.dot(u, wqkv_ref[:, j * hk:(j + 1) * hk], preferred_element_type=F32)

    _pipelined([functools.partial(dot_of, j) for j in order], consume, buf, zero, after_first)


def _gdn_in_kernel(z_ref, x_ref, mod_ref, wqkv_ref, wg_ref, wab_ref, wabt_ref, cw_ref, alr_ref, dtr_ref,
                   alc_ref, dtc_ref, q_ref, k_ref, v_ref, gz_ref, gb_ref, gr_ref, tail_ref, p_s):
    t = x_ref.shape[0]
    hk = gz_ref.shape[-1]
    nh = hk // GDN_DK

    @pl.when(pl.program_id(1) == 0)
    def _():
        tail_ref[...] = jnp.zeros_like(tail_ref)

    seq = pl.program_id(0)
    u = _modnorm(x_ref[...], _mrow(mod_ref, 1, seq), _mrow(mod_ref, 0, seq)).astype(BF16)

    def conv_fn(pre, j):
        cs = slice(j * hk, (j + 1) * hk)
        out = _conv_seq(pre, tail_ref[:, cs], cw_ref.at[:, cs], None)
        tail_ref[:, cs] = pre[t - SUBLANES:, :]
        return out

    def decays():
        ab = jnp.dot(u, wab_ref[...].astype(BF16), preferred_element_type=F32)
        abt = lax.dot_general(wabt_ref[...].astype(BF16), u, (((1,), (1,)), ((), ())),
                              preferred_element_type=F32)
        g_col = -jnp.exp(alr_ref[...]) * jax.nn.softplus(ab + dtr_ref[...])
        g_row = -jnp.exp(alc_ref[...]) * jax.nn.softplus(abt + dtc_ref[...])
        gc = _sdot_r(_block_tri(t, GDN_CHUNK, True), g_col)
        gr = _sdot_l(g_row, _block_tri(t, GDN_CHUNK, False))
        lane = lax.broadcasted_iota(jnp.int32, ab.shape, 1)
        gb_ref[...] = jnp.where(lane < nh, gc, _sigmoid(ab))
        gr_ref[...] = gr[:SUBLANES, :]

    _gdn_qkvg(u, u, wqkv_ref, wg_ref, conv_fn, q_ref, k_ref, v_ref, gz_ref, p_s, z_ref[0], decays)


def _gdn_in(x, mod, p):
    b, l, d = x.shape
    hk = p["hk"]
    t = ROW_TILE
    tok = lambda width: pl.BlockSpec((None, t, width), lambda i, c: (i, c, 0))
    w_arrs, w_specs = _unzip([p["wqkv"], p["wg"], p["wab"], p["wabt"], p["cw"], p["alr"], p["dtr"],
                              p["alc"], p["dtc"]])
    zero, zero_spec = _zero_operand()
    return pl.pallas_call(
        _gdn_in_kernel,
        grid=(b, l // t),
        in_specs=[zero_spec, tok(d), _mod_operand(mod)[1]] + w_specs,
        out_specs=[tok(hk), tok(hk), tok(hk), tok(hk), tok(LANES),
                   pl.BlockSpec((None, SUBLANES, t), lambda i, c: (i, 0, c)),
                   pl.BlockSpec((None, SUBLANES, 3 * hk), lambda i, c: (i, 0, 0))],
        out_shape=[jax.ShapeDtypeStruct((b, l, hk), F32)] * 4
        + [jax.ShapeDtypeStruct((b, l, LANES), F32),
           jax.ShapeDtypeStruct((b, SUBLANES, l), F32),
           jax.ShapeDtypeStruct((b, SUBLANES, 3 * hk), F32)],
        scratch_shapes=[pltpu.VMEM((DRAIN_SLOTS, t, hk), F32)],
        compiler_params=_params("parallel", "arbitrary"),
        name="gdn_in",
    )(zero, x, mod[0], *w_arrs)


def _gdn_scan_kernel(q_ref, k_ref, v_ref, gz_ref, gb_ref, gr_ref, ng_ref, y_ref, s_ref):
    c = GDN_CHUNK
    nseq, nh = s_ref.shape[0], s_ref.shape[1]

    @pl.when(pl.program_id(1) == 0)
    def _():
        s_ref[...] = jnp.zeros_like(s_ref)

    r = lax.broadcasted_iota(jnp.int32, (c, c), 0)
    cc = lax.broadcasted_iota(jnp.int32, (c, c), 1)
    causal = r >= cc
    strict = r > cc
    eye = jnp.where(r == cc, 1.0, 0.0)
    merge = []
    blk = 1
    while blk < c:
        merge.append((r // (2 * blk) == cc // (2 * blk)) & ((r // blk) % 2 == 1) & ((cc // blk) % 2 == 0))
        blk *= 2

    nchunk = gz_ref.shape[1] // c
    trip = [(b, j, h) for b in range(nseq) for j in range(nchunk) for h in range(nh)]
    rows = lambda j: slice(j * c, (j + 1) * c)
    cols = lambda h: slice(h * GDN_DK, (h + 1) * GDN_DK)
    g_col = lambda b, j, h: gb_ref[b, rows(j), h:h + 1]
    beta_of = lambda b, j, h: gb_ref[b, rows(j), nh + h:nh + h + 1]
    g_row = lambda b, j, h: gr_ref[b, h:h + 1, rows(j)]
    blk_of = lambda ref, b, j, h: ref[b, rows(j), cols(h)]

    kks = [_bdot_nt(jnp.concatenate([blk_of(k_ref, *t) * beta_of(*t), blk_of(q_ref, *t)], axis=0),
                    blk_of(k_ref, *t)) for t in trip]
    a_s, qk_s = [], []
    for t, kk in zip(trip, kks):
        decay = jnp.where(causal, jnp.exp(jnp.where(causal, g_col(*t) - g_row(*t), 0.0)), 0.0)
        a_s.append(jnp.where(strict, kk[:c] * decay, 0.0))
        qk_s.append((kk[c:] * decay).astype(BF16))
    xs = [eye - jnp.where(merge[0], a, 0.0) for a in a_s]
    for m in merge[1:]:
        ts = [_bdot(jnp.where(m, a, 0.0), x) for a, x in zip(a_s, xs)]
        xs = [x - _bdot(x, t_) for x, t_ in zip(xs, ts)]
    uws = {}
    for t, x in zip(trip, xs):
        kb = blk_of(k_ref, *t) * beta_of(*t)
        rhs = jnp.concatenate([blk_of(v_ref, *t) * beta_of(*t), kb * jnp.exp(g_col(*t))], axis=1)
        uws[t] = _bdot(x, rhs)
    qks = dict(zip(trip, qk_s))

    for j in range(nchunk):
        cur = [(b, j, h) for b in range(nseq) for h in range(nh)]
        states = {t: s_ref[t[0], t[2]] for t in cur}
        ws_qs = {t: _bdot(jnp.concatenate([uws[t][:, GDN_DV:], blk_of(q_ref, *t) * jnp.exp(g_col(*t))],
                                          axis=0), states[t]) for t in cur}
        v_news = {t: uws[t][:, :GDN_DV] - ws_qs[t][:c] for t in cur}
        outs = {t: ws_qs[t][c:] + jnp.dot(qks[t], v_news[t].astype(BF16), preferred_element_type=F32)
                for t in cur}
        for t in cur:
            g_last = g_row(*t)[:, c - 1:c]
            k_dec = blk_of(k_ref, *t) * jnp.exp(g_last - g_col(*t))
            s_ref[t[0], t[2]] = jnp.exp(g_last) * states[t] + lax.dot_general(
                k_dec.astype(BF16), v_news[t].astype(BF16), (((0,), (0,)), ((), ())),
                preferred_element_type=F32)
        for t in cur:
            o = outs[t]
            ms = jnp.mean(o * o, axis=-1, keepdims=True)
            y_ref[t[0], rows(j), cols(t[2])] = (o * lax.rsqrt(ms + EPS) * ng_ref[...]
                                                * gz_ref[t[0], rows(j), cols(t[2])]).astype(BF16)


def _gdn_scan(q, k, v, gz, gb, gr, norm_g):
    b, l, hk = gz.shape
    nh = hk // GDN_DK
    t = GDN_STEP
    ns = GDN_SEQS
    tok = lambda width: pl.BlockSpec((ns, t, width), lambda i, c: (i, c, 0))
    ng_arr, ng_spec = norm_g
    return pl.pallas_call(
        _gdn_scan_kernel,
        grid=(b // ns, l // t),
        in_specs=[tok(hk), tok(hk), tok(hk), tok(hk), tok(LANES),
                  pl.BlockSpec((ns, SUBLANES, t), lambda i, c: (i, 0, c)), ng_spec],
        out_specs=[tok(hk), pl.BlockSpec((ns, nh, GDN_DK, GDN_DV), lambda i, c: (i, 0, 0, 0))],
        out_shape=[jax.ShapeDtypeStruct((b, l, hk), BF16),
                   jax.ShapeDtypeStruct((b, nh, GDN_DK, GDN_DV), F32)],
        compiler_params=_params("parallel", "arbitrary"),
        name="gdn_scan",
    )(q, k, v, gz, gb, gr, ng_arr)


def _gdn_dec_in_kernel(z_ref, x_ref, mod_ref, wqkv_ref, wg_ref, wab_ref, cw_ref, alr_ref, dtr_ref, cs_ref,
                       qt_ref, kt_ref, v_ref, gz_ref, eg_ref, be_ref, nb_ref, q_s, k_s, p_s):
    hk = v_ref.shape[-1]
    nh = hk // GDN_DK
    u = _modnorm(x_ref[...], mod_ref[1], mod_ref[0]).astype(BF16)

    def conv_fn(pre, j):
        return _conv_step(pre, cs_ref, nb_ref, cw_ref, slice(j * hk, (j + 1) * hk), None)

    _gdn_qkvg(u, u, wqkv_ref, wg_ref, conv_fn, q_s, k_s, v_ref, gz_ref, p_s, z_ref[0])
    ab = jnp.dot(u, wab_ref[...].astype(BF16), preferred_element_type=F32)
    e_g = jnp.exp(-jnp.exp(alr_ref[...]) * jax.nn.softplus(ab + dtr_ref[...]))
    beta = _sigmoid(ab)
    n = ab.shape[0]
    for h in range(nh):
        hs = slice(h * GDN_DK, (h + 1) * GDN_DK)
        qt_ref[h] = q_s[:, hs].T
        kt_ref[h] = k_s[:, hs].T
        eg_ref[h] = jnp.broadcast_to(e_g[:, h:h + 1], (n, LANES))
        be_ref[h] = jnp.broadcast_to(beta[:, nh + h:nh + h + 1], (n, LANES))


def _state_spec(state, j):
    return pl.BlockSpec((None,) + state.shape[1:], lambda *_: (j,) + (0,) * (state.ndim - 1))


def _gdn_dec_in(x, mod, p, conv_state, j):
    n, d = x.shape
    hk = p["hk"]
    nh = hk // GDN_DK
    w_arrs, w_specs = _unzip([p["wqkv"], p["wg"], p["wab"], p["cw"], p["alr"], p["dtr"]])
    zero, zero_spec = _zero_operand()
    return pl.pallas_call(
        _gdn_dec_in_kernel,
        grid=(1,),
        in_specs=[zero_spec, _full((n, d)), _mod_operand(mod)[1]] + w_specs + [_state_spec(conv_state, j)],
        out_specs=[_full((nh, GDN_DK, n)), _full((nh, GDN_DK, n)), _full((n, hk)), _full((n, hk)),
                   _full((nh, n, LANES)), _full((nh, n, LANES)), _full((n, 3, 3 * hk))],
        out_shape=[jax.ShapeDtypeStruct((nh, GDN_DK, n), F32)] * 2
        + [jax.ShapeDtypeStruct((n, hk), F32)] * 2
        + [jax.ShapeDtypeStruct((nh, n, LANES), F32)] * 2
        + [jax.ShapeDtypeStruct((n, 3, 3 * hk), F32)],
        scratch_shapes=[pltpu.VMEM((n, hk), F32)] * 2 + [pltpu.VMEM((DRAIN_SLOTS, n, hk), F32)],
        compiler_params=_params("arbitrary"),
        name="gdn_decode_in",
    )(zero, x, mod[0], *w_arrs, conv_state)


def _gdn_dec_state_kernel(qt_ref, kt_ref, v_ref, gz_ref, eg_ref, be_ref, ng_ref, s0_ref,
                          y_ref, s_ref, o_s):
    n = s0_ref.shape[0]
    for b in range(n):
        kc = kt_ref[:, b:b + 1]
        qc = qt_ref[:, b:b + 1]
        s0 = s0_ref[b]
        k_s0 = jnp.sum(kc * s0, axis=0, keepdims=True)
        q_s0 = jnp.sum(qc * s0, axis=0, keepdims=True)
        qk = jnp.sum(qc * kc, axis=0, keepdims=True)
        e_g = eg_ref[b:b + 1, :]
        v_new = be_ref[b:b + 1, :] * (v_ref[b:b + 1, :] - e_g * k_s0)
        o_s[b:b + 1, :] = e_g * q_s0 + qk * v_new
        s_ref[b] = e_g * s0 + kc * v_new
    o = o_s[...]
    ms = jnp.mean(o * o, axis=-1, keepdims=True)
    y_ref[...] = (o * lax.rsqrt(ms + EPS) * ng_ref[...] * gz_ref[...]).astype(BF16)


def _gdn_dec_state(qt, kt, v, gz, eg, be, norm_g, s0, j):
    _, n, nh, dk, dv = s0.shape
    head = lambda a, bb: pl.BlockSpec((None, a, bb), lambda h: (h, 0, 0))
    col = pl.BlockSpec((n, dv), lambda h: (0, h))
    ng_arr, ng_spec = norm_g
    return pl.pallas_call(
        _gdn_dec_state_kernel,
        grid=(nh,),
        in_specs=[head(dk, n), head(dk, n), col, col, head(n, LANES), head(n, LANES), ng_spec,
                  pl.BlockSpec((None, n, None, dk, dv), lambda h: (j, 0, h, 0, 0))],
        out_specs=[col, pl.BlockSpec((n, None, dk, dv), lambda h: (0, h, 0, 0))],
        out_shape=[jax.ShapeDtypeStruct((n, nh * dv), BF16), jax.ShapeDtypeStruct((n, nh, dk, dv), F32)],
        scratch_shapes=[pltpu.VMEM((n, dv), F32)],
        compiler_params=_params("parallel"),
        name="gdn_decode_state",
    )(qt, kt, v, gz, eg, be, ng_arr, s0)


def _ssd_proj(u, u_z, wz_ref, wx_refs, conv_fn, zs_ref, xs_ref, bm_ref, cm_ref, buf, zero, after_first=None):
    gn = bm_ref.shape[-1]
    nz = zs_ref.shape[-1] // WBLK
    nx = len(wx_refs)
    order = [blk for pair in zip([("x", j) for j in range(nx)], [("z", j) for j in range(nz)] + [None] * nx)
             for blk in pair if blk is not None]

    def consume(i, pre):
        kind, j = order[i]
        cs = slice(j * WBLK, (j + 1) * WBLK)
        if kind == "z":
            zs_ref[:, cs] = _silu(pre)
            return
        act = _silu(conv_fn(pre, cs))
        if j + 1 < nx:
            xs_ref[:, cs] = act
        else:
            bm_ref[...] = act[:, :gn]
            cm_ref[...] = act[:, gn:]

    def dot_of(kind, j):
        if kind == "z":
            return jnp.dot(u_z, wz_ref[:, j * WBLK:(j + 1) * WBLK], preferred_element_type=F32)
        return jnp.dot(u, wx_refs[j][...], preferred_element_type=F32)

    _pipelined([functools.partial(dot_of, *blk) for blk in order], consume, buf, zero, after_first,
               side_after=2)


def _ssd_in_kernel(z_ref, x_ref, mod_ref, wz_ref, wx0_ref, wx1_ref, wx2_ref, wdt_ref, wdtt_ref, cw_ref,
                   cb_ref, dtr_ref, alr_ref, dtc_ref, alc_ref,
                   zs_ref, xs_ref, bm_ref, cm_ref, da_ref, at_ref, dtt_ref, tail_ref, p_s):
    t = x_ref.shape[0]
    nh = at_ref.shape[0]

    @pl.when(pl.program_id(1) == 0)
    def _():
        tail_ref[...] = jnp.zeros_like(tail_ref)

    seq = pl.program_id(0)
    u = _modnorm(x_ref[...], _mrow(mod_ref, 1, seq), _mrow(mod_ref, 0, seq)).astype(BF16)

    def conv_fn(pre, cs):
        out = _conv_seq(pre, tail_ref[:, cs], cw_ref.at[:, cs], cb_ref[:, cs])
        tail_ref[:, cs] = pre[t - SUBLANES:, :]
        return out

    def decays():
        dt_c = jax.nn.softplus(jnp.dot(u, wdt_ref[...].astype(BF16), preferred_element_type=F32)
                               + dtr_ref[...])
        dt_r = jax.nn.softplus(lax.dot_general(wdtt_ref[...].astype(BF16), u, (((1,), (1,)), ((), ())),
                                               preferred_element_type=F32) + dtc_ref[...])
        acs_c = _sdot_r(_block_tri(t, SSD_CHUNK, True), dt_c * -jnp.exp(alr_ref[...]))
        acs_r = _sdot_l(dt_r * -jnp.exp(alc_ref[...]), _block_tri(t, SSD_CHUNK, False))
        lane = lax.broadcasted_iota(jnp.int32, dt_c.shape, 1)
        da_ref[...] = jnp.where(lane < nh, dt_c, pltpu.roll(acs_c, nh, 1))
        at_ref[...] = acs_r[:nh, :]
        dtt_ref[...] = dt_r[:nh, :]

    _ssd_proj(u, u, wz_ref, (wx0_ref, wx1_ref, wx2_ref), conv_fn, zs_ref, xs_ref, bm_ref, cm_ref, p_s, z_ref[0],
              decays)


def _ssd_in(x, mod, p):
    b, l, d = x.shape
    di, cd, nh = p["di"], p["cd"], p["nh"]
    gn = (cd - di) // 2
    t = ROW_TILE
    tok = lambda width: pl.BlockSpec((None, t, width), lambda i, c: (i, c, 0))
    w_arrs, w_specs = _unzip([p["wz"], *p["wx"], p["wdt"], p["wdtt"], p["cw"], p["cb"], p["dtr"],
                              p["alr"], p["dtc"], p["alc"]])
    rows_spec = pl.BlockSpec((None, nh, t), lambda i, c: (i, 0, c))
    zero, zero_spec = _zero_operand()
    return pl.pallas_call(
        _ssd_in_kernel,
        grid=(b, l // t),
        in_specs=[zero_spec, tok(d), _mod_operand(mod)[1]] + w_specs,
        out_specs=[tok(di), tok(di), tok(gn), tok(gn), tok(LANES), rows_spec, rows_spec,
                   pl.BlockSpec((None, SUBLANES, cd), lambda i, c: (i, 0, 0))],
        out_shape=[jax.ShapeDtypeStruct((b, l, di), F32), jax.ShapeDtypeStruct((b, l, di), F32),
                   jax.ShapeDtypeStruct((b, l, gn), F32), jax.ShapeDtypeStruct((b, l, gn), F32),
                   jax.ShapeDtypeStruct((b, l, LANES), F32),
                   jax.ShapeDtypeStruct((b, nh, l), F32), jax.ShapeDtypeStruct((b, nh, l), F32),
                   jax.ShapeDtypeStruct((b, SUBLANES, cd), F32)],
        scratch_shapes=[pltpu.VMEM((DRAIN_SLOTS, t, WBLK), F32)],
        compiler_params=_params("parallel", "arbitrary"),
        name="ssd_in",
    )(zero, x, mod[0], *w_arrs)


def _ssd_scan_kernel(zs_ref, xs_ref, bm_ref, cm_ref, da_ref, at_ref, dtt_ref, e2_ref, dsk_ref, ng_ref,
                     y_ref, hout_ref, ht_s):
    c = zs_ref.shape[0]
    di = zs_ref.shape[1]
    n = SSM_STATE
    pdim = SSM_HEADDIM
    gw = di // SSM_GROUPS
    hpg = gw // pdim
    nh = di // pdim
    ci = pl.program_id(1)

    @pl.when(ci == 0)
    def _():
        ht_s[...] = jnp.zeros_like(ht_s)

    da = da_ref[...]
    lane_a = lax.broadcasted_iota(jnp.int32, da.shape, 1)
    is_acs = (lane_a >= nh) & (lane_a < 2 * nh)
    acs = jnp.where(is_acs, da, 0.0)
    dt_on_acs = jnp.where(is_acs, pltpu.roll(da, nh, 1), 0.0)
    eacs_e = _sdot_l(jnp.where(is_acs, jnp.exp(acs), 0.0), e2_ref[...])
    dsdt_e = _sdot_l(jnp.exp(acs[c - 1:c, :] - acs) * dt_on_acs, e2_ref[...])
    xs = xs_ref[...]
    bms = [bm_ref[:, g * n:(g + 1) * n] for g in range(SSM_GROUPS)]
    cms = [cm_ref[:, g * n:(g + 1) * n] for g in range(SSM_GROUPS)]
    xd = xs * dsdt_e

    r = lax.broadcasted_iota(jnp.int32, (c, c), 0)
    cc = lax.broadcasted_iota(jnp.int32, (c, c), 1)
    causal = r >= cc
    lane = lax.broadcasted_iota(jnp.int32, (c, 2 * pdim), 1)

    groups = range(SSM_GROUPS)
    gsl = lambda g: slice(g * gw, (g + 1) * gw)
    nsl = lambda g: slice(g * n, (g + 1) * n)
    assert n == LANES
    cbs = [_bdot_nt(cms[g], bms[g]) for g in groups]
    hts = [ht_s[g] for g in groups]
    y_offs = [_bdot(cms[g], hts[g]) * eacs_e[:, gsl(g)] for g in groups]
    for g in groups:
        ht_s[g] = eacs_e[c - 1:c, gsl(g)] * hts[g] + _bdot(bms[g].T, xd[:, gsl(g)])
    for g in groups:
        y_parts = []
        for pr in range(hpg // 2):
            ms = []
            for hh in (2 * pr, 2 * pr + 1):
                h = g * hpg + hh
                seg = da[:, nh + h:nh + h + 1] - at_ref[h:h + 1, :]
                lm = jnp.where(causal, jnp.exp(jnp.where(causal, seg, 0.0)), 0.0)
                ms.append((cbs[g] * lm * dtt_ref[h:h + 1, :]).astype(BF16))
            ps = slice(g * gw + pr * 2 * pdim, g * gw + (pr + 1) * 2 * pdim)
            xp = xs[:, ps].astype(BF16)
            zero = jnp.zeros_like(xp)
            rhs = jnp.concatenate([jnp.where(lane < pdim, xp, zero), jnp.where(lane >= pdim, xp, zero)], axis=0)
            y_d = jnp.dot(jnp.concatenate(ms, axis=1), rhs, preferred_element_type=F32)
            os_ = slice(pr * 2 * pdim, (pr + 1) * 2 * pdim)
            y_parts.append(y_d + y_offs[g][:, os_] + dsk_ref[:, ps] * xs[:, ps])
        yz = jnp.concatenate(y_parts, axis=1) * zs_ref[:, gsl(g)]
        ms_ = jnp.mean(yz * yz, axis=-1, keepdims=True)
        y_ref[:, gsl(g)] = (yz * lax.rsqrt(ms_ + EPS) * ng_ref[:, gsl(g)]).astype(BF16)

    @pl.when(ci == pl.num_programs(1) - 1)
    def _():
        for g in groups:
            hout_ref[gsl(g), :] = ht_s[g].T


def _ssd_scan(zs, xs, bm, cm, da, at, dtt, p):
    b, l, di = zs.shape
    gn = bm.shape[-1]
    nh = p["nh"]
    gw = di // SSM_GROUPS
    t = SSD_CHUNK
    tok = lambda width: pl.BlockSpec((None, t, width), lambda i, c: (i, c, 0))
    rows_spec = pl.BlockSpec((None, nh, t), lambda i, c: (i, 0, c))
    w_arrs, w_specs = _unzip([p["e2"], p["dsk"], p["ng"]])
    return pl.pallas_call(
        _ssd_scan_kernel,
        grid=(b, l // t),
        in_specs=[tok(di), tok(di), tok(gn), tok(gn), tok(LANES), rows_spec, rows_spec] + w_specs,
        out_specs=[tok(di), pl.BlockSpec((None, di, SSM_STATE), lambda i, c: (i, 0, 0))],
        out_shape=[jax.ShapeDtypeStruct((b, l, di), BF16),
                   jax.ShapeDtypeStruct((b, di, SSM_STATE), F32)],
        scratch_shapes=[pltpu.VMEM((SSM_GROUPS, SSM_STATE, gw), F32)],
        compiler_params=_params("parallel", "arbitrary"),
        name="ssd_scan",
    )(zs, xs, bm, cm, da, at, dtt, *w_arrs)


def _ssd_dec_in_kernel(z_ref, x_ref, mod_ref, wz_ref, wx0_ref, wx1_ref, wx2_ref, wdt_ref, cw_ref, cb_ref,
                       dtr_ref, alr_ref, e1_ref, cs_ref,
                       zs_ref, xs_ref, bm_ref, cm_ref, xt_ref, ea_ref, xdt_ref, eae_ref, nb_ref, p_s):
    n = x_ref.shape[0]
    di = zs_ref.shape[-1]
    nh = di // SSM_HEADDIM
    u = _modnorm(x_ref[...], mod_ref[1], mod_ref[0]).astype(BF16)

    def conv_fn(pre, cs):
        return _conv_step(pre, cs_ref, nb_ref, cw_ref, cs, cb_ref[:, cs])

    _ssd_proj(u, u, wz_ref, (wx0_ref, wx1_ref, wx2_ref), conv_fn, zs_ref, xs_ref, bm_ref, cm_ref, p_s, z_ref[0])
    dt = jax.nn.softplus(jnp.dot(u, wdt_ref[...].astype(BF16), preferred_element_type=F32) + dtr_ref[...])
    e_a = jnp.exp(dt * -jnp.exp(alr_ref[...]))
    xdt_ref[...] = xs_ref[...] * _sdot_l(dt, e1_ref[...])
    eae_ref[...] = _sdot_l(e_a, e1_ref[...])
    for j in range(di // LANES):
        xt_ref[j * LANES:(j + 1) * LANES, :] = xdt_ref[:, j * LANES:(j + 1) * LANES].T
    for h in range(nh):
        ea_ref[h] = jnp.broadcast_to(e_a[:, h:h + 1], (n, LANES))


def _ssd_dec_in(x, mod, p, conv_state, j):
    n, d = x.shape
    di, cd, nh = p["di"], p["cd"], p["nh"]
    gn = (cd - di) // 2
    w_arrs, w_specs = _unzip([p["wz"], *p["wx"], p["wdt"], p["cw"], p["cb"], p["dtr"], p["alr"], p["e1"]])
    zero, zero_spec = _zero_operand()
    return pl.pallas_call(
        _ssd_dec_in_kernel,
        grid=(1,),
        in_specs=[zero_spec, _full((n, d)), _mod_operand(mod)[1]] + w_specs + [_state_spec(conv_state, j)],
        out_specs=[_full((n, di)), _full((n, di)), _full((n, gn)), _full((n, gn)), _full((di, n)),
                   _full((nh, n, LANES)), _full((n, di)), _full((n, di)), _full((n, 3, cd))],
        out_shape=[jax.ShapeDtypeStruct((n, di), F32), jax.ShapeDtypeStruct((n, di), F32),
                   jax.ShapeDtypeStruct((n, gn), F32), jax.ShapeDtypeStruct((n, gn), F32),
                   jax.ShapeDtypeStruct((di, n), F32),
                   jax.ShapeDtypeStruct((nh, n, LANES), F32),
                   jax.ShapeDtypeStruct((n, di), F32), jax.ShapeDtypeStruct((n, di), F32),
                   jax.ShapeDtypeStruct((n, 3, cd), F32)],
        scratch_shapes=[pltpu.VMEM((DRAIN_SLOTS, n, WBLK), F32)],
        compiler_params=_params("arbitrary"),
        name="ssd_decode_in",
    )(zero, x, mod[0], *w_arrs, conv_state)


def _ssd_dec_state_kernel(xt_ref, bm_ref, cm_ref, ea_ref, h0_ref, yo_ref, h_ref):
    n = h0_ref.shape[0]
    pdim = h0_ref.shape[2]
    for b in range(n):
        bb = bm_ref[b:b + 1, :]
        c8 = jnp.broadcast_to(cm_ref[b:b + 1, :], (SUBLANES, bb.shape[1]))
        outs = []
        for hh in range(2):
            h0 = h0_ref[b, hh]
            outs.append(_bdot_nt(c8, h0)[0:1, :])
            xc = xt_ref[hh * pdim:(hh + 1) * pdim, b:b + 1]
            h_ref[b, hh] = ea_ref[hh, b:b + 1, :] * h0 + xc * bb
        yo_ref[b:b + 1, :] = jnp.concatenate(outs, axis=1)


def _ssd_dec_state(xt, bm, cm, ea, h0, j):
    _, n, nh, pdim, ns = h0.shape
    hpg = nh // SSM_GROUPS
    return pl.pallas_call(
        _ssd_dec_state_kernel,
        grid=(nh // 2,),
        in_specs=[pl.BlockSpec((2 * pdim, n), lambda i: (i, 0)),
                  pl.BlockSpec((n, ns), lambda i: (0, (2 * i) // hpg)),
                  pl.BlockSpec((n, ns), lambda i: (0, (2 * i) // hpg)),
                  pl.BlockSpec((2, n, LANES), lambda i: (i, 0, 0)),
                  pl.BlockSpec((None, n, 2, pdim, ns), lambda i: (j, 0, i, 0, 0))],
        out_specs=[pl.BlockSpec((n, 2 * pdim), lambda i: (0, i)),
                   pl.BlockSpec((n, 2, pdim, ns), lambda i: (0, i, 0, 0))],
        out_shape=[jax.ShapeDtypeStruct((n, nh * pdim), F32), jax.ShapeDtypeStruct((n, nh, pdim, ns), F32)],
        compiler_params=_params("parallel"),
        name="ssd_decode_state",
    )(xt, bm, cm, ea, h0)


def _ssd_dec_out_kernel(yo_ref, xs_ref, bm_ref, cm_ref, zs_ref, eae_ref, xdt_ref, dsk_ref, ng_ref, y_ref):
    di = xs_ref.shape[1]
    gw = di // SSM_GROUPS
    n = SSM_STATE
    for g in range(SSM_GROUPS):
        gs = slice(g * gw, (g + 1) * gw)
        cb = jnp.sum(cm_ref[:, g * n:(g + 1) * n] * bm_ref[:, g * n:(g + 1) * n], axis=-1, keepdims=True)
        y = cb * xdt_ref[:, gs] + eae_ref[:, gs] * yo_ref[:, gs] + dsk_ref[:, gs] * xs_ref[:, gs]
        yz = y * zs_ref[:, gs]
        ms = jnp.mean(yz * yz, axis=-1, keepdims=True)
        y_ref[:, gs] = (yz * lax.rsqrt(ms + EPS) * ng_ref[:, gs]).astype(BF16)


def _ssd_dec_out(yo, xs, bm, cm, zs, eae, xdt, p):
    n, di = xs.shape
    gn = bm.shape[1]
    w_arrs, w_specs = _unzip([p["dsk"], p["ng"]])
    return pl.pallas_call(
        _ssd_dec_out_kernel,
        grid=(1,),
        in_specs=[_full((n, di)), _full((n, di)), _full((n, gn)), _full((n, gn)), _full((n, di)),
                  _full((n, di)), _full((n, di))] + w_specs,
        out_specs=_full((n, di)),
        out_shape=jax.ShapeDtypeStruct((n, di), BF16),
        compiler_params=_params("arbitrary"),
        name="ssd_decode_out",
    )(yo, xs, bm, cm, zs, eae, xdt, *w_arrs)


def _pad_lanes(v):
    return jnp.pad(v.astype(F32), (0, LANES - v.shape[0])).reshape(1, LANES)


def _pad_cols(w):
    return jnp.pad(w, ((0, 0), (0, LANES - w.shape[1])))


def _small(a):
    return _w(a, a.shape, (0,) * a.ndim)


def _rg_params(j, w_in, conv_w, conv_b, gate_w, gate_b, lam, w_out):
    nl, d, w2 = w_in.shape
    w = w2 // 2
    nblk = w // RG_BLOCK
    gate = (None, None, nblk, RG_BLOCK, RG_BLOCK)
    return dict(width=w,
                wy=_w(w_in, (None, d, w), (j, 0, 0)), wx=_w(w_in, (None, d, w), (j, 0, 1)),
                cw=_w(conv_w, (None, 4, w), (j, 0, 0)), cb=_w(conv_b.reshape(nl, 1, w), (None, 1, w), (j, 0, 0)),
                gwr=_w(gate_w, gate, (j, 0, 0, 0, 0)), gwi=_w(gate_w, gate, (j, 1, 0, 0, 0)),
                gb=_w(gate_b, (None, 2, w), (j, 0, 0)), lam=_w(lam.reshape(nl, 1, w), (None, 1, w), (j, 0, 0)),
                wo=_w(w_out, (None, w, d), (j, 0, 0)))


def _gdn_params(j, w_in, w_in_f32, conv_w, a_log, dt_bias, norm_g, w_out):
    nl, d, _ = w_in.shape
    qkv = conv_w.shape[-1]
    hk = qkv // 3
    wab = _pad_cols(w_in_f32[j, :, qkv + hk:])
    alr, dtr = _pad_lanes(a_log[j]), _pad_lanes(dt_bias[j])
    return dict(hk=hk,
                wqkv=_w(w_in, (None, d, qkv), (j, 0, 0)), wg=_w(w_in, (None, d, hk), (j, 0, qkv // hk)),
                wab=_small(wab), wabt=_small(wab.T), cw=_w(conv_w, (None, 4, qkv), (j, 0, 0)),
                alr=_small(alr), dtr=_small(dtr), alc=_small(alr.T), dtc=_small(dtr.T),
                ng=_w(norm_g.reshape(nl, 1, GDN_DV), (None, 1, GDN_DV), (j, 0, 0)),
                wo=_w(w_out, (None, hk, d), (j, 0, 0)))


def _ssd_params(j, w_in, w_in_f32, conv_w, conv_b, a_log, dt_bias, d_skip, norm_g, w_out):
    nl, d, _ = w_in.shape
    cd = conv_w.shape[-1]
    nh = a_log.shape[-1]
    di = nh * SSM_HEADDIM
    assert di % WBLK == 0 and (cd - di) == WBLK
    wdt = _pad_cols(w_in_f32[j, :, di + cd:])
    alr, dtr = _pad_lanes(a_log[j]), _pad_lanes(dt_bias[j])
    head_of = jnp.arange(di, dtype=jnp.int32) // SSM_HEADDIM
    rows = jnp.arange(LANES, dtype=jnp.int32)[:, None]
    e1 = (rows == head_of[None, :]).astype(BF16)
    e2 = (rows == head_of[None, :] + nh).astype(BF16)
    return dict(di=di, cd=cd, nh=nh,
                wz=_w(w_in, (None, d, di), (j, 0, 0)),
                wx=[_w(w_in, (None, d, WBLK), (j, 0, di // WBLK + i)) for i in range(cd // WBLK)],
                wdt=_small(wdt), wdtt=_small(wdt.T), cw=_w(conv_w, (None, 4, cd), (j, 0, 0)),
                cb=_w(conv_b.reshape(nl, 1, cd), (None, 1, cd), (j, 0, 0)),
                alr=_small(alr), dtr=_small(dtr), alc=_small(alr.T), dtc=_small(dtr.T),
                e1=_small(e1), e2=_small(e2),
                dsk=_small(jnp.repeat(d_skip[j], SSM_HEADDIM).reshape(1, di)),
                ng=_w(norm_g.reshape(nl, 1, di), (None, 1, di), (j, 0, 0)),
                wo=_w(w_out, (None, di, d), (j, 0, 0)))


def kernel(x_prompt, x_sample, state_rglru_conv, state_rglru_h, state_gdn_conv, state_gdn_S, state_ssd_conv, state_ssd_h, c_prompt, c_sample, w_mod, b_mod, w_mlp_up, w_mlp_down, final_norm_g, rg_w_in, rg_conv_w, rg_conv_b, rg_gate_w, rg_gate_b, rg_lambda, rg_w_out, gdn_w_in, gdn_conv_w, gdn_A_log, gdn_dt_bias, gdn_norm_g, gdn_w_out, ssd_w_in, ssd_conv_w, ssd_conv_b, ssd_A_log, ssd_dt_bias, ssd_D, ssd_norm_g, ssd_w_out):
    bp, l, d = x_prompt.shape
    ns = x_sample.shape[0]
    hid = w_mlp_up.shape[-1]
    assert x_sample.shape[1] == 1 and l % ROW_TILE == 0 and ns % SUBLANES == 0 and bp % GDN_SEQS == 0

    assert ns % bp == 0 and bp % SUBLANES == 0
    mod = _modulation(jnp.concatenate([c_sample, c_prompt], axis=0), w_mod, b_mod)
    mod_p = [(mod, i, bp, ns) for i in range(DEPTH)]
    mod_s = [(mod, i, ns, 0) for i in range(DEPTH)]
    fg = _small(final_norm_g.reshape(1, d))

    w_up, w_down = w_mlp_up, w_mlp_down
    rg_in, rg_gate, rg_out = rg_w_in.astype(BF16), rg_gate_w.astype(BF16), rg_w_out.astype(BF16)
    gdn_in, gdn_out = gdn_w_in.astype(BF16), gdn_w_out.astype(BF16)
    ssd_in, ssd_out = ssd_w_in.astype(BF16), ssd_w_out.astype(BF16)

    xp = x_prompt
    xs = x_sample.reshape(ns, d)
    tails = lambda t: t[:, SUBLANES - 3:, :]
    tails_interleaved = lambda t: t[:, SUBLANES - 1::SUBLANES, :]
    out = {k: [] for k in ("p_rg_conv", "p_rg_h", "p_gdn_conv", "p_gdn_S", "p_ssd_conv", "p_ssd_h",
                           "s_rg_conv", "s_rg_h", "s_gdn_conv", "s_gdn_S", "s_ssd_conv", "s_ssd_h")}
    for i in range(DEPTH):
        j = i // N_MIXERS
        kind = i % N_MIXERS
        if kind == 0:
            p = _rg_params(j, rg_in, rg_conv_w, rg_conv_b, rg_gate, rg_gate_b, rg_lambda, rg_out)
            yp, tail, h_last = _rg_prompt(xp, mod_p[i], p)
            out["p_rg_conv"].append(tails_interleaved(tail))
            out["p_rg_h"].append(h_last[:, 0, :])
            ysm, nb, h_new = _rg_decode(xs, mod_s[i], p, state_rglru_conv, state_rglru_h, j)
            out["s_rg_conv"].append(nb)
            out["s_rg_h"].append(h_new)
        elif kind == 1:
            p = _gdn_params(j, gdn_in, gdn_w_in, gdn_conv_w, gdn_A_log, gdn_dt_bias, gdn_norm_g, gdn_out)
            q, k, v, gz, gb, gr, tail = _gdn_in(xp, mod_p[i], p)
            yp, s_fin = _gdn_scan(q, k, v, gz, gb, gr, p["ng"])
            out["p_gdn_conv"].append(tails(tail))
            out["p_gdn_S"].append(s_fin)
            qt, kt, v1, gz1, eg, be, nb = _gdn_dec_in(xs, mod_s[i], p, state_gdn_conv, j)
            ysm, s_new = _gdn_dec_state(qt, kt, v1, gz1, eg, be, p["ng"], state_gdn_S, j)
            out["s_gdn_conv"].append(nb)
            out["s_gdn_S"].append(s_new)
        else:
            p = _ssd_params(j, ssd_in, ssd_w_in, ssd_conv_w, ssd_conv_b, ssd_A_log, ssd_dt_bias, ssd_D, ssd_norm_g,
                            ssd_out)
            zs, xc, bm, cm, da, at, dtt, tail = _ssd_in(xp, mod_p[i], p)
            yp, h_fin = _ssd_scan(zs, xc, bm, cm, da, at, dtt, p)
            out["p_ssd_conv"].append(tails(tail))
            out["p_ssd_h"].append(h_fin.reshape(bp, -1, SSM_HEADDIM, SSM_STATE))
            zs1, xc1, bm1, cm1, xt, ea, xdt, eae, nb = _ssd_dec_in(xs, mod_s[i], p, state_ssd_conv, j)
            yo, h_new = _ssd_dec_state(xt, bm1, cm1, ea, state_ssd_h, j)
            ysm = _ssd_dec_out(yo, xc1, bm1, cm1, zs1, eae, xdt, p)
            out["s_ssd_conv"].append(nb)
            out["s_ssd_h"].append(h_new)
        final = i == DEPTH - 1
        weights = [p["wo"], _w(w_up, (None, d, hid), (i, 0, 0)), _w(w_down, (None, hid, d), (i, 0, 0)), fg]
        y_rows = yp if yp.ndim == 4 else yp.reshape(bp * l, -1)
        xp = _outproj_mlp(xp.reshape(bp * l, d), y_rows, mod_p[i], weights,
                          tm=ROW_TILE, rows_per_mod=l, final=final).reshape(bp, l, d)
        xs = _outproj_mlp(xs, ysm, mod_s[i], weights, tm=ns, rows_per_mod=1, final=final)

    st = {k: jnp.stack(v) for k, v in out.items()}
    return (xp, xs.reshape(ns, 1, d),
            st["p_rg_conv"], st["p_rg_h"], st["p_gdn_conv"], st["p_gdn_S"], st["p_ssd_conv"], st["p_ssd_h"],
            st["s_rg_conv"], st["s_rg_h"], st["s_gdn_conv"], st["s_gdn_S"], st["s_ssd_conv"], st["s_ssd_h"])
```
